```python
import math
import jax, jax.numpy as jnp
from jax import lax
import numpy as np

D_MODEL = 1024
BATCH = 4
SEQ = 4096
DEPTH = 2

MEM_LEN = 256
RMS_EPS = 1e-6

A_HEADS = 8
A_HEAD_DIM = 64
A_WIDTH = A_HEADS * A_HEAD_DIM
IDX_HEADS = 8
IDX_DIM = 64
TOPK_MAX = 256
Q_BLOCK = 128

REL_BUCKETS = 32
REL_MAX_EXACT = 16
REL_MAX_DIST = 128

B_WIDTH = 512
CONV_K = 3

POOL_WINDOWS = (2, 4, 8, 16)
POOL_GROUPS = 4
POOL_GROUP = D_MODEL // POOL_GROUPS

X_HEADS = 4
X_HEAD_DIM = 128
X_WIDTH = X_HEADS * X_HEAD_DIM

D_FF = 2816

Q_END = A_WIDTH
K_END = 2 * A_WIDTH
V_END = 3 * A_WIDTH
IQ_END = V_END + IDX_HEADS * IDX_DIM
IK_END = IQ_END + IDX_DIM
IW_END = IK_END + IDX_HEADS
U_END = IW_END + B_WIDTH
GB_END = U_END + B_WIDTH
IN_COLS = GB_END + B_WIDTH
IN_SPLITS = (Q_END, K_END, V_END, IQ_END, IK_END, IW_END, U_END, GB_END)
MIX_WIDTH = A_WIDTH + B_WIDTH

N_EVEN = (DEPTH + 1) // 2
N_ODD = DEPTH // 2

kernel_name = "hybrid_dsa_shortconv_pool_macaron"


def rmsnorm(x, g):
    xf = x.astype(jnp.float32)
    y = xf * lax.rsqrt(jnp.mean(xf * xf, axis=-1, keepdims=True) + RMS_EPS)
    return (y * g.astype(jnp.float32)).astype(x.dtype)


def swiglu(h, w_gate, w_up, w_down):
    return (jax.nn.silu(h @ w_gate) * (h @ w_up)) @ w_down


def t5_bucket(dist):
    n = jnp.maximum(dist, 0)
    nf = jnp.maximum(n, 1).astype(jnp.float32)
    large = REL_MAX_EXACT + (jnp.log(nf / REL_MAX_EXACT)
                             / math.log(REL_MAX_DIST / REL_MAX_EXACT)
                             * (REL_BUCKETS - REL_MAX_EXACT)).astype(jnp.int32)
    large = jnp.minimum(large, REL_BUCKETS - 1)
    return jnp.where(n < REL_MAX_EXACT, n, large)


def dsa_attention(q, k, v, iq, ik, iw, rel_bias):
    b, s = q.shape[0], q.shape[1]
    n_sel = min(TOPK_MAX, s // 4)
    nblk = s // Q_BLOCK
    kf = k.reshape(b, s, A_WIDTH)
    vf = v.reshape(b, s, A_WIDTH)
    key_pos = jnp.arange(s, dtype=jnp.int32)
    iw = iw * (IDX_HEADS ** -0.5 * IDX_DIM ** -0.5)
    scale = A_HEAD_DIM ** -0.5

    def to_blocks(a):
        return jnp.moveaxis(a.reshape((b, nblk, Q_BLOCK) + a.shape[2:]), 1, 0)

    def block(args):
        qb, iqb, iwb, qpos = args
        dots = jnp.einsum('bqhd,bsd->bqhs', iqb, ik)
        idx_score = jnp.einsum('bqhs,bqh->bqs', jax.nn.relu(dots), iwb).astype(jnp.float32)
        causal = key_pos[None, :] <= qpos[:, None]
        idx_score = jnp.where(causal[None], idx_score, -jnp.inf)
        _, sel = lax.top_k(idx_score, n_sel)
        flat = sel.reshape(b, Q_BLOCK * n_sel)
        kg = jax.vmap(lambda a, i: a[i])(kf, flat).reshape(b, Q_BLOCK, n_sel, A_HEADS, A_HEAD_DIM)
        vg = jax.vmap(lambda a, i: a[i])(vf, flat).reshape(b, Q_BLOCK, n_sel, A_HEADS, A_HEAD_DIM)
        logits = jnp.einsum('bqhd,bqkhd->bqhk', qb, kg).astype(jnp.float32) * scale
        dist = qpos[None, :, None] - sel
        bias = rel_bias[t5_bucket(dist)].astype(jnp.float32)
        logits = logits + jnp.swapaxes(bias, -1, -2)
        logits = jnp.where((dist >= 0)[:, :, None, :], logits, -jnp.inf)
        p = jax.nn.softmax(logits, axis=-1).astype(vg.dtype)
        return jnp.einsum('bqhk,bqkhd->bqhd', p, vg)

    qpos_blocks = jnp.arange(s, dtype=jnp.int32).reshape(nblk, Q_BLOCK)
    out = lax.map(block, (to_blocks(q), to_blocks(iq), to_blocks(iw), qpos_blocks))
    return jnp.moveaxis(out, 0, 1).reshape(b, s, A_WIDTH)


def short_conv(u, w):
    s = u.shape[1]
    up = jnp.pad(u, ((0, 0), (CONV_K - 1, 0), (0, 0)))
    y = w[CONV_K - 1] * u
    for j in range(CONV_K - 1):
        y = y + w[j] * up[:, j:j + s]
    return y


def attn_conv_mixer(h, w_in, conv_w, w_out, rel_bias):
    b, s, _ = h.shape
    proj = h @ w_in
    q, k, v, iq, ik, iw, u, gb, gc = jnp.split(proj, IN_SPLITS, axis=-1)
    a = dsa_attention(q.reshape(b, s, A_HEADS, A_HEAD_DIM),
                      k.reshape(b, s, A_HEADS, A_HEAD_DIM),
                      v.reshape(b, s, A_HEADS, A_HEAD_DIM),
                      iq.reshape(b, s, IDX_HEADS, IDX_DIM), ik, iw, rel_bias)
    c = gb * short_conv(gc * u, conv_w)
    return jnp.concatenate([a, c], axis=-1) @ w_out


def pool_mixer(h, pool_w, pool_scale):
    b, s, d = h.shape
    cs = jnp.cumsum(h.astype(jnp.float32), axis=1)
    cs = jnp.pad(cs, ((0, 0), (1, 0), (0, 0)))
    t = jnp.arange(s)
    outs = []
    for g, win in enumerate(POOL_WINDOWS):
        sl = slice(g * POOL_GROUP, (g + 1) * POOL_GROUP)
        hi = cs[:, 1:, sl]
        lo = cs[:, jnp.maximum(t + 1 - win, 0), sl]
        cnt = jnp.minimum(t + 1, win).astype(jnp.float32)
        outs.append((hi - lo) / cnt[None, :, None])
    pooled = jnp.concatenate(outs, axis=-1).astype(h.dtype) - h
    y = jnp.einsum('bsgc,gcd->bsgd', pooled.reshape(b, s, POOL_GROUPS, POOL_GROUP), pool_w)
    return y.reshape(b, s, d) * pool_scale


def memory_cross_attention(h, mem_n, wq, wk, wv, wo):
    b, s, _ = h.shape
    m = mem_n.shape[1]
    q = (h @ wq).reshape(b, s, X_HEADS, X_HEAD_DIM)
    k = (mem_n @ wk).reshape(b, m, X_HEADS, X_HEAD_DIM)
    v = (mem_n @ wv).reshape(b, m, X_HEADS, X_HEAD_DIM)
    logits = jnp.einsum('bqhd,bkhd->bhqk', q, k).astype(jnp.float32) * (X_HEAD_DIM ** -0.5)
    p = jax.nn.softmax(logits, axis=-1).astype(v.dtype)
    o = jnp.einsum('bhqk,bkhd->bqhd', p, v).reshape(b, s, X_WIDTH)
    return o @ wo


def setup_inputs(seed: int = 0) -> dict:
    key = jax.random.key(seed)
    ks = jax.random.split(key, 24)
    nrm = jax.random.normal
    f32 = jnp.float32

    def gain(k, shape):
        return 1.0 + 0.05 * nrm(k, shape, f32)

    return {
        "x": nrm(ks[0], (BATCH, SEQ, D_MODEL), f32),
        "mem": nrm(ks[1], (BATCH, MEM_LEN, D_MODEL), f32),
        "ffn_w_gate": nrm(ks[2], (DEPTH, 2, D_MODEL, D_FF), f32) * D_MODEL ** -0.5,
        "ffn_w_up": nrm(ks[3], (DEPTH, 2, D_MODEL, D_FF), f32) * D_MODEL ** -0.5,
        "ffn_w_down": nrm(ks[4], (DEPTH, 2, D_FF, D_MODEL), f32) * D_FF ** -0.5,
        "ffn_norm_pre": gain(ks[5], (DEPTH, 2, D_MODEL)),
        "ffn_norm_post": gain(ks[6], (DEPTH, 2, D_MODEL)),
        "mix_norm_pre": gain(ks[7], (DEPTH, D_MODEL)),
        "mix_norm_post": gain(ks[8], (DEPTH, D_MODEL)),
        "even_w_in": nrm(ks[9], (N_EVEN, D_MODEL, IN_COLS), f32) * D_MODEL ** -0.5,
        "even_conv_w": nrm(ks[10], (N_EVEN, CONV_K, B_WIDTH), f32) * CONV_K ** -0.5,
        "even_w_out": nrm(ks[11], (N_EVEN, MIX_WIDTH, D_MODEL), f32) * MIX_WIDTH ** -0.5,
        "rel_bias": 0.5 * nrm(ks[12], (REL_BUCKETS, A_HEADS), f32),
        "pool_w": nrm(ks[13], (N_ODD, POOL_GROUPS, POOL_GROUP, POOL_GROUP), f32) * POOL_GROUP ** -0.5,
        "pool_scale": 1.0 + 0.1 * nrm(ks[14], (N_ODD, D_MODEL), f32),
        "xattn_norm_pre": gain(ks[15], (DEPTH, D_MODEL)),
        "xattn_mem_norm": gain(ks[16], (DEPTH, D_MODEL)),
        "xattn_norm_post": gain(ks[17], (DEPTH, D_MODEL)),
        "xattn_wq": nrm(ks[18], (DEPTH, D_MODEL, X_WIDTH), f32) * D_MODEL ** -0.5,
        "xattn_wk": nrm(ks[19], (DEPTH, D_MODEL, X_WIDTH), f32) * D_MODEL ** -0.5,
        "xattn_wv": nrm(ks[20], (DEPTH, D_MODEL, X_WIDTH), f32) * D_MODEL ** -0.5,
        "xattn_wo": nrm(ks[21], (DEPTH, X_WIDTH, D_MODEL), f32) * X_WIDTH ** -0.5,
    }


def reference(x, mem, ffn_w_gate, ffn_w_up, ffn_w_down, ffn_norm_pre, ffn_norm_post,
              mix_norm_pre, mix_norm_post, even_w_in, even_conv_w, even_w_out, rel_bias,
              pool_w, pool_scale, xattn_norm_pre, xattn_mem_norm, xattn_norm_post,
              xattn_wq, xattn_wk, xattn_wv, xattn_wo):
    for layer in range(DEPTH):
        h = rmsnorm(x, ffn_norm_pre[layer, 0])
        f = swiglu(h, ffn_w_gate[layer, 0], ffn_w_up[layer, 0], ffn_w_down[layer, 0])
        x = x + 0.5 * rmsnorm(f, ffn_norm_post[layer, 0])
        h = rmsnorm(x, mix_norm_pre[layer])
        if layer % 2 == 0:
            e = layer // 2
            y = attn_conv_mixer(h, even_w_in[e], even_conv_w[e], even_w_out[e], rel_bias)
        else:
            o = layer // 2
            y = pool_mixer(h, pool_w[o], pool_scale[o])
        x = x + rmsnorm(y, mix_norm_post[layer])
        h = rmsnorm(x, xattn_norm_pre[layer])
        mem_n = rmsnorm(mem, xattn_mem_norm[layer])
        c = memory_cross_attention(h, mem_n, xattn_wq[layer], xattn_wk[layer],
                                   xattn_wv[layer], xattn_wo[layer])
        x = x + rmsnorm(c, xattn_norm_post[layer])
        h = rmsnorm(x, ffn_norm_pre[layer, 1])
        f = swiglu(h, ffn_w_gate[layer, 1], ffn_w_up[layer, 1], ffn_w_down[layer, 1])
        x = x + 0.5 * rmsnorm(f, ffn_norm_post[layer, 1])
    return x
```

```python
import functools
import math

import jax
import jax.numpy as jnp
import numpy as np
from jax import lax
from jax.experimental import pallas as pl
from jax.experimental.pallas import tpu as pltpu

F32 = jnp.float32
BF16 = jnp.bfloat16
I32 = jnp.int32

RMS_EPS = 1e-6

A_HEADS = 8
A_HEAD_DIM = 64
A_WIDTH = A_HEADS * A_HEAD_DIM
IDX_HEADS = 8
IDX_DIM = 64
TOPK_MAX = 256
REL_BUCKETS = 32
REL_MAX_EXACT = 16
REL_MAX_DIST = 128
B_WIDTH = 512
CONV_K = 3
POOL_WINDOWS = (2, 4, 8, 16)
POOL_HALO = 16
X_HEADS = 4
X_HEAD_DIM = 128
X_WIDTH = X_HEADS * X_HEAD_DIM

LANES = 128
SUBLANES = 8
MXU_DIM = 256
VMEM_BYTES_V7X = 64 * 1024 * 1024
VMEM_LIMIT = VMEM_BYTES_V7X - 8 * 1024 * 1024

ATT_TILE = MXU_DIM
FFN_ROWS = 1024
FFN_COLS = 256
ROW_TILE = 512

MASKED = -1e30
KEY_MIN = -(2 ** 31)


def _t5_bucket_table(n):
    d = np.arange(n)
    nf = np.maximum(d, 1).astype(np.float32)
    ratio = np.log(nf / np.float32(REL_MAX_EXACT)) / np.float32(math.log(REL_MAX_DIST / REL_MAX_EXACT))
    large = REL_MAX_EXACT + (ratio * np.float32(REL_BUCKETS - REL_MAX_EXACT)).astype(np.int32)
    large = np.minimum(large, REL_BUCKETS - 1)
    return np.where(d < REL_MAX_EXACT, d, large)


def _rms(x, g):
    ms = jnp.mean(x * x, axis=-1, keepdims=True)
    return x * lax.rsqrt(ms + RMS_EPS) * g


def _params(*semantics):
    return pltpu.CompilerParams(dimension_semantics=semantics, vmem_limit_bytes=VMEM_LIMIT)


def _ffn_kernel(x_ref, gpre_ref, gpost_ref, wg_ref, wu_ref, wd_ref, o_ref, h_ref):
    f = pl.program_id(1)

    @pl.when(f == 0)
    def _():
        h_ref[...] = _rms(x_ref[...], gpre_ref[...]).astype(BF16)

    h = h_ref[...]
    g = jnp.dot(h, wg_ref[...], preferred_element_type=F32)
    u = jnp.dot(h, wu_ref[...], preferred_element_type=F32)
    a = (g * jax.nn.sigmoid(g) * u).astype(BF16)
    d = jnp.dot(a, wd_ref[...], preferred_element_type=F32)

    @pl.when(f == 0)
    def _():
        o_ref[...] = d

    @pl.when(f > 0)
    def _():
        o_ref[...] += d

    @pl.when(f == pl.num_programs(1) - 1)
    def _():
        o_ref[...] = x_ref[...] + 0.5 * _rms(o_ref[...], gpost_ref[...])


def _ffn(x, g_pre, g_post, w_gate, w_up, w_down):
    n, d = x.shape
    dff = w_gate.shape[1]
    return pl.pallas_call(
        _ffn_kernel,
        out_shape=jax.ShapeDtypeStruct((n, d), F32),
        grid=(n // FFN_ROWS, dff // FFN_COLS),
        in_specs=[
            pl.BlockSpec((FFN_ROWS, d), lambda i, f: (i, 0)),
            pl.BlockSpec((1, d), lambda i, f: (0, 0)),
            pl.BlockSpec((1, d), lambda i, f: (0, 0)),
            pl.BlockSpec((d, FFN_COLS), lambda i, f: (0, f)),
            pl.BlockSpec((d, FFN_COLS), lambda i, f: (0, f)),
            pl.BlockSpec((FFN_COLS, d), lambda i, f: (f, 0)),
        ],
        out_specs=pl.BlockSpec((FFN_ROWS, d), lambda i, f: (i, 0)),
        scratch_shapes=[pltpu.VMEM((FFN_ROWS, d), BF16)],
        compiler_params=_params("parallel", "arbitrary"),
        name="ffn",
    )(x, g_pre.reshape(1, d), g_post.reshape(1, d), w_gate.astype(BF16), w_up.astype(BF16),
      w_down.astype(BF16))


def _proj_kernel(x_ref, g_ref, wk_ref, wik_ref, wugg_ref, wqt_ref, wvt_ref, wiqt_ref, wiwt_ref,
                 convw_ref, qt_ref, vt_ref, iqt_ref, iwt_ref, k_ref, ik_ref, c_ref, carry_ref):
    i = pl.program_id(1)
    h = _rms(x_ref[...], g_ref[...]).astype(BF16)
    nt = (((1,), (1,)), ((), ()))
    qt = lax.dot_general(wqt_ref[...], h, nt, preferred_element_type=F32)
    qt_ref[...] = (qt * (A_HEAD_DIM ** -0.5)).astype(BF16)
    vt_ref[...] = lax.dot_general(wvt_ref[...], h, nt, preferred_element_type=F32).astype(BF16)
    iqt_ref[...] = lax.dot_general(wiqt_ref[...], h, nt, preferred_element_type=F32).astype(BF16)
    iwt = lax.dot_general(wiwt_ref[...], h, nt, preferred_element_type=F32)
    iwt_ref[...] = iwt[:IDX_HEADS] * (IDX_HEADS ** -0.5 * IDX_DIM ** -0.5)
    k_ref[...] = jnp.dot(h, wk_ref[...], preferred_element_type=F32).astype(BF16)
    ik_ref[...] = jnp.dot(h, wik_ref[...], preferred_element_type=F32).astype(BF16)

    ugg = jnp.dot(h, wugg_ref[...], preferred_element_type=F32)
    u = ugg[:, :B_WIDTH]
    gb = ugg[:, B_WIDTH:2 * B_WIDTH]
    gc = ugg[:, 2 * B_WIDTH:]
    gu = gc * u

    @pl.when(i == 0)
    def _():
        carry_ref[...] = jnp.zeros_like(carry_ref)

    t = gu.shape[0]
    ext = jnp.concatenate([carry_ref[...], gu], axis=0)
    w = convw_ref[...]
    y = w[2:3] * gu
    for j in range(CONV_K - 1):
        lo = SUBLANES - (CONV_K - 1) + j
        y = y + w[j:j + 1] * ext[lo:lo + t]
    c_ref[...] = (gb * y).astype(BF16)
    carry_ref[...] = gu[t - SUBLANES:]


def _proj_even(x, g, w_in, conv_w):
    b, s, d = x.shape
    t = ATT_TILE
    nchunk = s // t
    q_end, k_end, v_end = A_WIDTH, 2 * A_WIDTH, 3 * A_WIDTH
    iq_end = v_end + IDX_HEADS * IDX_DIM
    ik_end = iq_end + IDX_DIM
    iw_end = ik_end + IDX_HEADS
    wb = w_in.astype(BF16)
    wqt = wb[:, :q_end].T
    wk = wb[:, q_end:k_end]
    wvt = wb[:, k_end:v_end].T
    wiqt = wb[:, v_end:iq_end].T
    wik = wb[:, iq_end:ik_end]
    wiwt = jnp.pad(wb[:, ik_end:iw_end].T, ((0, 2 * SUBLANES - IDX_HEADS), (0, 0)))
    wugg = wb[:, iw_end:]
    const = lambda shape: pl.BlockSpec(shape, lambda bi, i: (0,) * len(shape))
    return pl.pallas_call(
        _proj_kernel,
        out_shape=(
            jax.ShapeDtypeStruct((b, A_WIDTH, s), BF16),
            jax.ShapeDtypeStruct((b, nchunk, A_WIDTH, t), BF16),
            jax.ShapeDtypeStruct((b, IDX_HEADS * IDX_DIM, s), BF16),
            jax.ShapeDtypeStruct((b, IDX_HEADS, s), F32),
            jax.ShapeDtypeStruct((b, nchunk, t, A_WIDTH), BF16),
            jax.ShapeDtypeStruct((b, nchunk, t, IDX_DIM), BF16),
            jax.ShapeDtypeStruct((b, s, B_WIDTH), BF16),
        ),
        grid=(b, nchunk),
        in_specs=[
            pl.BlockSpec((None, t, d), lambda bi, i: (bi, i, 0)),
            const((1, d)),
            const(wk.shape), const(wik.shape), const(wugg.shape), const(wqt.shape),
            const(wvt.shape), const(wiqt.shape), const(wiwt.shape), const(conv_w.shape),
        ],
        out_specs=(
            pl.BlockSpec((None, A_WIDTH, t), lambda bi, i: (bi, 0, i)),
            pl.BlockSpec((None, None, A_WIDTH, t), lambda bi, i: (bi, i, 0, 0)),
            pl.BlockSpec((None, IDX_HEADS * IDX_DIM, t), lambda bi, i: (bi, 0, i)),
            pl.BlockSpec((None, IDX_HEADS, t), lambda bi, i: (bi, 0, i)),
            pl.BlockSpec((None, None, t, A_WIDTH), lambda bi, i: (bi, i, 0, 0)),
            pl.BlockSpec((None, None, t, IDX_DIM), lambda bi, i: (bi, i, 0, 0)),
            pl.BlockSpec((None, t, B_WIDTH), lambda bi, i: (bi, i, 0)),
        ),
        scratch_shapes=[pltpu.VMEM((SUBLANES, B_WIDTH), F32)],
        compiler_params=_params("arbitrary", "arbitrary"),
        name="proj_even",
    )(x, g.reshape(1, d), wk, wik, wugg, wqt, wvt, wiqt, wiwt, conv_w)


def _bias_kernel(rb_ref, o_ref, *, thresholds, far_bucket):
    hd = pl.program_id(0)
    t = o_ref.shape[-1]
    r = lax.broadcasted_iota(I32, (t, t), 1)
    c = lax.broadcasted_iota(I32, (t, t), 0)
    for m in range(2):
        dist = t * m + r - c
        val = jnp.full((t, t), rb_ref[0, hd], F32)
        for bucket in range(1, REL_BUCKETS):
            val = jnp.where(dist >= thresholds[bucket], rb_ref[bucket, hd], val)
        if m == 0:
            val = jnp.where(dist < 0, MASKED, val)
        o_ref[m] = val
    o_ref[2] = jnp.full((t, t), rb_ref[far_bucket, hd], F32)


def _bias_tiles(rel_bias, seq):
    t = ATT_TILE
    table = _t5_bucket_table(seq)
    assert np.all(np.diff(table) >= 0) and table[0] == 0 and table.max() == REL_BUCKETS - 1
    thresholds = tuple(int(np.argmax(table >= bkt)) for bkt in range(REL_BUCKETS))
    far_bucket = int(table[2 * t - 1])
    assert np.all(table[2 * t - 1:] == far_bucket)
    return pl.pallas_call(
        functools.partial(_bias_kernel, thresholds=thresholds, far_bucket=far_bucket),
        out_shape=jax.ShapeDtypeStruct((A_HEADS, 3, t, t), F32),
        grid=(A_HEADS,),
        in_specs=[pl.BlockSpec(memory_space=pltpu.SMEM)],
        out_specs=pl.BlockSpec((None, 3, t, t), lambda h: (h, 0, 0, 0)),
        compiler_params=_params("parallel"),
        name="rel_bias_tiles",
    )(rel_bias)


def _dsa_kernel(qt_ref, iqt_ref, iwt_ref, k_ref, vt_ref, ik_ref, bias_ref, o_ref,
                key_ref, acc_ref, m_ref, l_ref, qz_ref):
    i = pl.program_id(1)
    t = ATT_TILE
    nk = i + 1
    iw = iwt_ref[...]

    def sort_keys(j):
        ikc = ik_ref[j]
        sc = jnp.zeros((t, t), F32)
        for hd in range(IDX_HEADS):
            d = jnp.dot(ikc, iqt_ref[hd * IDX_DIM:(hd + 1) * IDX_DIM, :],
                        preferred_element_type=F32)
            sc = sc + jnp.maximum(d, 0.0) * iw[hd:hd + 1, :]
        sc = sc + 0.0
        bits = pltpu.bitcast(sc, I32)
        return bits ^ ((bits >> 31) & 0x7FFFFFFF)

    def phase_a(j, carry):
        key_ref[j] = sort_keys(j)
        return carry

    lax.fori_loop(0, i, phase_a, 0)
    qpos = lax.broadcasted_iota(I32, (t, t), 1)
    kpos = lax.broadcasted_iota(I32, (t, t), 0)
    key_ref[i] = jnp.where(kpos <= qpos, sort_keys(i), KEY_MIN)

    def count_ge(cand):
        def body(j, cnt):
            ind = jnp.where(key_ref[j] >= cand, 1.0, 0.0)
            return cnt + jnp.sum(ind.reshape(t // SUBLANES, SUBLANES, t), axis=0)
        cnt = lax.fori_loop(0, nk, body, jnp.zeros((SUBLANES, t), F32))
        return jnp.sum(cnt, axis=0, keepdims=True)

    def phase_b(it, thr):
        cand = thr + jnp.left_shift(jnp.int32(1), 31 - it)
        return jnp.where(count_ge(cand) >= float(TOPK_MAX), cand, thr)

    thr = lax.fori_loop(0, 32, phase_b, jnp.full((1, t), KEY_MIN, I32))
    need = float(TOPK_MAX) - count_ge(thr + 1)

    m_ref[...] = jnp.full(m_ref.shape, MASKED, F32)
    l_ref[...] = jnp.zeros(l_ref.shape, F32)
    acc_ref[...] = jnp.zeros(acc_ref.shape, F32)
    row = lax.broadcasted_iota(I32, (2 * A_HEAD_DIM, t), 0)
    for hd in range(A_HEADS):
        pair = hd // 2
        blk = qt_ref[2 * A_HEAD_DIM * pair:2 * A_HEAD_DIM * (pair + 1), :]
        own = (row >= A_HEAD_DIM) if hd % 2 else (row < A_HEAD_DIM)
        qz_ref[hd] = jnp.where(own, blk, jnp.zeros_like(blk))
    tri = (lax.broadcasted_iota(I32, (t, t), 0) >= lax.broadcasted_iota(I32, (t, t), 1)).astype(BF16)

    def phase_c(j, ties_before):
        kk = key_ref[j]
        is_tie = kk == thr
        tie_rank = jnp.dot(tri, jnp.where(is_tie, 1.0, 0.0).astype(BF16),
                           preferred_element_type=F32) + ties_before
        tie_add = jnp.where(is_tie, jnp.where(tie_rank <= need, 0.0, MASKED), MASKED)
        mask_add = jnp.where(kk > thr, 0.0, tie_add)
        far = jnp.minimum(i - j, 2)
        for hd in range(A_HEADS):
            pair = hd // 2
            s = jnp.dot(k_ref[j, :, 2 * A_HEAD_DIM * pair:2 * A_HEAD_DIM * (pair + 1)], qz_ref[hd],
                        preferred_element_type=F32)
            s = s + bias_ref[hd, far] + mask_add
            m_old = m_ref[hd:hd + 1, :]
            m_new = jnp.maximum(m_old, jnp.max(s, axis=0, keepdims=True))
            alpha = jnp.exp(m_old - m_new)
            p = jnp.exp(s - m_new)
            l_ref[hd:hd + 1, :] = alpha * l_ref[hd:hd + 1, :] + jnp.sum(p, axis=0, keepdims=True)
            rows = slice(hd * A_HEAD_DIM, (hd + 1) * A_HEAD_DIM)
            pv = jnp.dot(vt_ref[j, rows, :], p.astype(BF16), preferred_element_type=F32)
            acc_ref[rows, :] = alpha * acc_ref[rows, :] + pv
            m_ref[hd:hd + 1, :] = m_new
        return tie_rank[t - 1:t, :]

    lax.fori_loop(0, nk, phase_c, jnp.zeros((1, t), F32))

    outs = []
    for hd in range(A_HEADS):
        rows = slice(hd * A_HEAD_DIM, (hd + 1) * A_HEAD_DIM)
        outs.append(acc_ref[rows, :] / l_ref[hd:hd + 1, :])
    o_ref[...] = jnp.concatenate(outs, axis=0).T.astype(BF16)


def _dsa_attention(qt, vt, iqt, iwt, k, ik, bias):
    b, nchunk, t, _ = k.shape
    s = nchunk * t
    return pl.pallas_call(
        _dsa_kernel,
        out_shape=jax.ShapeDtypeStruct((b, s, A_WIDTH), BF16),
        grid=(b, nchunk),
        in_specs=[
            pl.BlockSpec((None, A_WIDTH, t), lambda bi, i: (bi, 0, i)),
            pl.BlockSpec((None, IDX_HEADS * IDX_DIM, t), lambda bi, i: (bi, 0, i)),
            pl.BlockSpec((None, IDX_HEADS, t), lambda bi, i: (bi, 0, i)),
            pl.BlockSpec((None, nchunk, t, A_WIDTH), lambda bi, i: (bi, 0, 0, 0)),
            pl.BlockSpec((None, nchunk, A_WIDTH, t), lambda bi, i: (bi, 0, 0, 0)),
            pl.BlockSpec((None, nchunk, t, IDX_DIM), lambda bi, i: (bi, 0, 0, 0)),
            pl.BlockSpec(bias.shape, lambda bi, i: (0, 0, 0, 0)),
        ],
        out_specs=pl.BlockSpec((None, t, A_WIDTH), lambda bi, i: (bi, i, 0)),
        scratch_shapes=[
            pltpu.VMEM((nchunk, t, t), I32),
            pltpu.VMEM((A_WIDTH, t), F32),
            pltpu.VMEM((A_HEADS, t), F32),
            pltpu.VMEM((A_HEADS, t), F32),
            pltpu.VMEM((A_HEADS, 2 * A_HEAD_DIM, t), BF16),
        ],
        compiler_params=_params("parallel", "parallel"),
        name="dsa_attention",
    )(qt, iqt, iwt, k, vt, ik, bias)


def _mix_out_kernel(x_ref, a_ref, c_ref, wa_ref, wc_ref, g_ref, o_ref):
    y = jnp.dot(a_ref[...], wa_ref[...], preferred_element_type=F32)
    y = y + jnp.dot(c_ref[...], wc_ref[...], preferred_element_type=F32)
    o_ref[...] = x_ref[...] + _rms(y, g_ref[...])


def _mix_out_even(x, a, c, w_out, g_post):
    n, d = x.shape
    t = ROW_TILE
    wb = w_out.astype(BF16)
    row = lambda width: pl.BlockSpec((t, width), lambda i: (i, 0))
    const = lambda shape: pl.BlockSpec(shape, lambda i: (0, 0))
    return pl.pallas_call(
        _mix_out_kernel,
        out_shape=jax.ShapeDtypeStruct((n, d), F32),
        grid=(n // t,),
        in_specs=[row(d), row(A_WIDTH), row(B_WIDTH), const((A_WIDTH, d)), const((B_WIDTH, d)),
                  const((1, d))],
        out_specs=row(d),
        compiler_params=_params("parallel"),
        name="mix_out_even",
    )(x, a, c, wb[:A_WIDTH], wb[A_WIDTH:], g_post.reshape(1, d))


def _pool_kernel(x_ref, xprev_ref, gpre_ref, gpost_ref, w_ref, scale_ref, o_ref):
    i = pl.program_id(1)
    x = x_ref[...]
    t, d = x.shape
    gpre = gpre_ref[...]
    h = _rms(x, gpre)
    hp = _rms(xprev_ref[...], gpre) * jnp.where(i > 0, 1.0, 0.0)
    ext = jnp.concatenate([hp, h], axis=0)
    group = d // len(POOL_WINDOWS)
    pos = i * t + lax.broadcasted_iota(I32, (t, group), 0)
    ys = []
    for gi, win in enumerate(POOL_WINDOWS):
        sums = ext[:, gi * group:(gi + 1) * group]
        width = 1
        while width < win:
            sums = sums[width:] + sums[:-width]
            width *= 2
        wsum = sums[POOL_HALO - (win - 1):POOL_HALO - (win - 1) + t]
        cnt = jnp.minimum(pos + 1, win).astype(F32)
        pooled = wsum / cnt - h[:, gi * group:(gi + 1) * group]
        ys.append(jnp.dot(pooled.astype(BF16), w_ref[gi], preferred_element_type=F32))
    y = jnp.concatenate(ys, axis=-1) * scale_ref[...]
    o_ref[...] = x + _rms(y, gpost_ref[...])


def _pool_mixer(x, g_pre, g_post, pool_w, pool_scale):
    b, s, d = x.shape
    t = ROW_TILE
    const = lambda shape: pl.BlockSpec(shape, lambda bi, i: (0,) * len(shape))
    prev = t // POOL_HALO
    return pl.pallas_call(
        _pool_kernel,
        out_shape=jax.ShapeDtypeStruct((b, s, d), F32),
        grid=(b, s // t),
        in_specs=[
            pl.BlockSpec((None, t, d), lambda bi, i: (bi, i, 0)),
            pl.BlockSpec((None, POOL_HALO, d), lambda bi, i: (bi, jnp.maximum(i * prev - 1, 0), 0)),
            const((1, d)), const((1, d)), const(pool_w.shape), const((1, d)),
        ],
        out_specs=pl.BlockSpec((None, t, d), lambda bi, i: (bi, i, 0)),
        compiler_params=_params("parallel", "parallel"),
        name="pool_mixer",
    )(x, x, g_pre.reshape(1, d), g_post.reshape(1, d), pool_w.astype(BF16), pool_scale.reshape(1, d))


def _mem_kv_kernel(mem_ref, g_ref, wkt_ref, wv_ref, kt_ref, v_ref):
    mem_n = _rms(mem_ref[...], g_ref[...]).astype(BF16)
    kt = lax.dot_general(wkt_ref[...], mem_n, (((1,), (1,)), ((), ())), preferred_element_type=F32)
    kt_ref[...] = kt.astype(BF16)
    v_ref[...] = jnp.dot(mem_n, wv_ref[...], preferred_element_type=F32).astype(BF16)


def _mem_kv(mem, g, wk, wv):
    b, m, d = mem.shape
    const = lambda shape: pl.BlockSpec(shape, lambda bi: (0, 0))
    return pl.pallas_call(
        _mem_kv_kernel,
        out_shape=(jax.ShapeDtypeStruct((b, X_WIDTH, m), BF16), jax.ShapeDtypeStruct((b, m, X_WIDTH), BF16)),
        grid=(b,),
        in_specs=[pl.BlockSpec((None, m, d), lambda bi: (bi, 0, 0)), const((1, d)), const((X_WIDTH, d)),
                  const((d, X_WIDTH))],
        out_specs=(pl.BlockSpec((None, X_WIDTH, m), lambda bi: (bi, 0, 0)),
                   pl.BlockSpec((None, m, X_WIDTH), lambda bi: (bi, 0, 0))),
        compiler_params=_params("parallel"),
        name="mem_kv",
    )(mem, g.reshape(1, d), wk.astype(BF16).T, wv.astype(BF16))


def _xattn_kernel(x_ref, gpre_ref, gpost_ref, wq_ref, kt_ref, v_ref, wo_ref, o_ref):
    x = x_ref[...]
    h = _rms(x, gpre_ref[...]).astype(BF16)
    q = jnp.dot(h, wq_ref[...], preferred_element_type=F32)
    outs = []
    for hd in range(X_HEADS):
        cols = slice(hd * X_HEAD_DIM, (hd + 1) * X_HEAD_DIM)
        logits = jnp.dot(q[:, cols].astype(BF16), kt_ref[cols, :], preferred_element_type=F32)
        logits = logits * (X_HEAD_DIM ** -0.5)
        p = jnp.exp(logits - jnp.max(logits, axis=-1, keepdims=True))
        den = jnp.sum(p, axis=-1, keepdims=True)
        outs.append(jnp.dot(p.astype(BF16), v_ref[:, cols], preferred_element_type=F32) / den)
    o = jnp.concatenate(outs, axis=-1).astype(BF16)
    y = jnp.dot(o, wo_ref[...], preferred_element_type=F32)
    o_ref[...] = x + _rms(y, gpost_ref[...])


def _xattn(x, kt, v, g_pre, g_post, wq, wo):
    b, s, d = x.shape
    m = v.shape[1]
    t = ROW_TILE
    const = lambda shape: pl.BlockSpec(shape, lambda bi, i: (0, 0))
    return pl.pallas_call(
        _xattn_kernel,
        out_shape=jax.ShapeDtypeStruct((b, s, d), F32),
        grid=(b, s // t),
        in_specs=[
            pl.BlockSpec((None, t, d), lambda bi, i: (bi, i, 0)),
            const((1, d)), const((1, d)), const((d, X_WIDTH)),
            pl.BlockSpec((None, X_WIDTH, m), lambda bi, i: (bi, 0, 0)),
            pl.BlockSpec((None, m, X_WIDTH), lambda bi, i: (bi, 0, 0)),
            const((X_WIDTH, d)),
        ],
        out_specs=pl.BlockSpec((None, t, d), lambda bi, i: (bi, i, 0)),
        compiler_params=_params("parallel", "parallel"),
        name="xattn",
    )(x, g_pre.reshape(1, d), g_post.reshape(1, d), wq.astype(BF16), kt, v, wo.astype(BF16))


def kernel(x, mem, ffn_w_gate, ffn_w_up, ffn_w_down, ffn_norm_pre, ffn_norm_post, mix_norm_pre,
           mix_norm_post, even_w_in, even_conv_w, even_w_out, rel_bias, pool_w, pool_scale,
           xattn_norm_pre, xattn_mem_norm, xattn_norm_post, xattn_wq, xattn_wk, xattn_wv, xattn_wo):
    b, s, d = x.shape
    depth = ffn_w_gate.shape[0]
    assert s % ATT_TILE == 0 and s // 4 >= TOPK_MAX and (b * s) % FFN_ROWS == 0 and s % ROW_TILE == 0
    bias = _bias_tiles(rel_bias, s)

    def ffn(xx, layer, j):
        out = _ffn(xx.reshape(b * s, d), ffn_norm_pre[layer, j], ffn_norm_post[layer, j],
                   ffn_w_gate[layer, j], ffn_w_up[layer, j], ffn_w_down[layer, j])
        return out.reshape(b, s, d)

    for layer in range(depth):
        x = ffn(x, layer, 0)
        if layer % 2 == 0:
            e = layer // 2
            qt, vt, iqt, iwt, k, ik, c = _proj_even(x, mix_norm_pre[layer], even_w_in[e], even_conv_w[e])
            a = _dsa_attention(qt, vt, iqt, iwt, k, ik, bias)
            x = _mix_out_even(x.reshape(b * s, d), a.reshape(b * s, A_WIDTH), c.reshape(b * s, B_WIDTH),
                              even_w_out[e], mix_norm_post[layer]).reshape(b, s, d)
        else:
            o = layer // 2
            x = _pool_mixer(x, mix_norm_pre[layer], mix_norm_post[layer], pool_w[o], pool_scale[o])
        kt, v = _mem_kv(mem, xattn_mem_norm[layer], xattn_wk[layer], xattn_wv[layer])
        x = _xattn(x, kt, v, xattn_norm_pre[layer], xattn_norm_post[layer], xattn_wq[layer],
                   xattn_wo[layer])
        x = ffn(x, layer, 1)
    return x
```

```python
import functools
import math

import jax
import jax.numpy as jnp
import numpy as np
from jax import lax
from jax.experimental import pallas as pl
from jax.experimental.pallas import tpu as pltpu

F32 = jnp.float32
BF16 = jnp.bfloat16
I32 = jnp.int32

RMS_EPS = 1e-6

A_HEADS = 8
A_HEAD_DIM = 64
A_WIDTH = A_HEADS * A_HEAD_DIM
IDX_HEADS = 8
IDX_DIM = 64
TOPK_MAX = 256
REL_BUCKETS = 32
REL_MAX_EXACT = 16
REL_MAX_DIST = 128
B_WIDTH = 512
CONV_K = 3
POOL_WINDOWS = (2, 4, 8, 16)
POOL_HALO = 16
X_HEADS = 4
X_HEAD_DIM = 128
X_WIDTH = X_HEADS * X_HEAD_DIM

LANES = 128
SUBLANES = 8
MXU_DIM = 256
VMEM_BYTES_V7X = 64 * 1024 * 1024
VMEM_LIMIT = VMEM_BYTES_V7X - 8 * 1024 * 1024

ATT_TILE = MXU_DIM
FFN_ROWS = 1024
FFN_COLS = 256
ROW_TILE = 512

BF16_SUBLANES = 16
PV_ROWS = A_HEAD_DIM + BF16_SUBLANES
MASKED = -1e30
KEY_MIN = -(2 ** 31)


def _t5_bucket_table(n):
    d = np.arange(n)
    nf = np.maximum(d, 1).astype(np.float32)
    ratio = np.log(nf / np.float32(REL_MAX_EXACT)) / np.float32(math.log(REL_MAX_DIST / REL_MAX_EXACT))
    large = REL_MAX_EXACT + (ratio * np.float32(REL_BUCKETS - REL_MAX_EXACT)).astype(np.int32)
    large = np.minimum(large, REL_BUCKETS - 1)
    return np.where(d < REL_MAX_EXACT, d, large)


def _rms(x, g):
    ms = jnp.mean(x * x, axis=-1, keepdims=True)
    return x * lax.rsqrt(ms + RMS_EPS) * g


def _params(*semantics):
    return pltpu.CompilerParams(dimension_semantics=semantics, vmem_limit_bytes=VMEM_LIMIT)


def _ffn_kernel(x_ref, gpre_ref, gpost_ref, wg_ref, wu_ref, wd_ref, o_ref, h_ref):
    f = pl.program_id(1)

    @pl.when(f == 0)
    def _():
        h_ref[...] = _rms(x_ref[...], gpre_ref[...]).astype(BF16)

    h = h_ref[...]
    g = jnp.dot(h, wg_ref[...], preferred_element_type=F32)
    u = jnp.dot(h, wu_ref[...], preferred_element_type=F32)
    a = (g * jax.nn.sigmoid(g) * u).astype(BF16)
    d = jnp.dot(a, wd_ref[...], preferred_element_type=F32)

    @pl.when(f == 0)
    def _():
        o_ref[...] = d

    @pl.when(f > 0)
    def _():
        o_ref[...] += d

    @pl.when(f == pl.num_programs(1) - 1)
    def _():
        o_ref[...] = x_ref[...] + 0.5 * _rms(o_ref[...], gpost_ref[...])


def _ffn(x, g_pre, g_post, w_gate, w_up, w_down):
    n, d = x.shape
    dff = w_gate.shape[1]
    return pl.pallas_call(
        _ffn_kernel,
        out_shape=jax.ShapeDtypeStruct((n, d), F32),
        grid=(n // FFN_ROWS, dff // FFN_COLS),
        in_specs=[
            pl.BlockSpec((FFN_ROWS, d), lambda i, f: (i, 0)),
            pl.BlockSpec((1, d), lambda i, f: (0, 0)),
            pl.BlockSpec((1, d), lambda i, f: (0, 0)),
            pl.BlockSpec((d, FFN_COLS), lambda i, f: (0, f)),
            pl.BlockSpec((d, FFN_COLS), lambda i, f: (0, f)),
            pl.BlockSpec((FFN_COLS, d), lambda i, f: (f, 0)),
        ],
        out_specs=pl.BlockSpec((FFN_ROWS, d), lambda i, f: (i, 0)),
        scratch_shapes=[pltpu.VMEM((FFN_ROWS, d), BF16)],
        compiler_params=_params("parallel", "arbitrary"),
        name="ffn",
    )(x, g_pre.reshape(1, d), g_post.reshape(1, d), w_gate.astype(BF16), w_up.astype(BF16),
      w_down.astype(BF16))


def _proj_kernel(x_ref, g_ref, wk_ref, wik_ref, wugg_ref, wqt_ref, wvt_ref, wiqt_ref, wiwt_ref,
                 convw_ref, qt_ref, vt_ref, iqt_ref, iwt_ref, k_ref, ik_ref, c_ref, carry_ref):
    i = pl.program_id(1)
    h = _rms(x_ref[...], g_ref[...]).astype(BF16)
    nt = (((1,), (1,)), ((), ()))
    qt = lax.dot_general(wqt_ref[...], h, nt, preferred_element_type=F32)
    qt_ref[...] = (qt * (A_HEAD_DIM ** -0.5)).astype(BF16)
    vt = lax.dot_general(wvt_ref[...], h, nt, preferred_element_type=F32).astype(BF16)
    ones = jnp.ones((PV_ROWS - A_HEAD_DIM, vt.shape[1]), BF16)
    for hd in range(A_HEADS):
        vt_ref[hd * PV_ROWS:hd * PV_ROWS + A_HEAD_DIM, :] = vt[hd * A_HEAD_DIM:(hd + 1) * A_HEAD_DIM]
        vt_ref[hd * PV_ROWS + A_HEAD_DIM:(hd + 1) * PV_ROWS, :] = ones
    iqt_ref[...] = lax.dot_general(wiqt_ref[...], h, nt, preferred_element_type=F32).astype(BF16)
    iwt = lax.dot_general(wiwt_ref[...], h, nt, preferred_element_type=F32)
    iwt_ref[...] = iwt[:IDX_HEADS] * (IDX_HEADS ** -0.5 * IDX_DIM ** -0.5)
    k_ref[...] = jnp.dot(h, wk_ref[...], preferred_element_type=F32).astype(BF16)
    ik_ref[...] = jnp.dot(h, wik_ref[...], preferred_element_type=F32).astype(BF16)

    ugg = jnp.dot(h, wugg_ref[...], preferred_element_type=F32)
    u = ugg[:, :B_WIDTH]
    gb = ugg[:, B_WIDTH:2 * B_WIDTH]
    gc = ugg[:, 2 * B_WIDTH:]
    gu = gc * u

    @pl.when(i == 0)
    def _():
        carry_ref[...] = jnp.zeros_like(carry_ref)

    t = gu.shape[0]
    ext = jnp.concatenate([carry_ref[...], gu], axis=0)
    w = convw_ref[...]
    y = w[2:3] * gu
    for j in range(CONV_K - 1):
        lo = SUBLANES - (CONV_K - 1) + j
        y = y + w[j:j + 1] * ext[lo:lo + t]
    c_ref[...] = (gb * y).astype(BF16)
    carry_ref[...] = gu[t - SUBLANES:]


def _proj_even(x, g, w_in, conv_w):
    b, s, d = x.shape
    t = ATT_TILE
    nchunk = s // t
    q_end, k_end, v_end = A_WIDTH, 2 * A_WIDTH, 3 * A_WIDTH
    iq_end = v_end + IDX_HEADS * IDX_DIM
    ik_end = iq_end + IDX_DIM
    iw_end = ik_end + IDX_HEADS
    wb = w_in.astype(BF16)
    wqt = wb[:, :q_end].T
    wk = wb[:, q_end:k_end]
    wvt = wb[:, k_end:v_end].T
    wiqt = wb[:, v_end:iq_end].T
    wik = wb[:, iq_end:ik_end]
    wiwt = jnp.pad(wb[:, ik_end:iw_end].T, ((0, 2 * SUBLANES - IDX_HEADS), (0, 0)))
    wugg = wb[:, iw_end:]
    const = lambda shape: pl.BlockSpec(shape, lambda bi, i: (0,) * len(shape))
    return pl.pallas_call(
        _proj_kernel,
        out_shape=(
            jax.ShapeDtypeStruct((b, A_WIDTH, s), BF16),
            jax.ShapeDtypeStruct((b, nchunk, A_HEADS * PV_ROWS, t), BF16),
            jax.ShapeDtypeStruct((b, IDX_HEADS * IDX_DIM, s), BF16),
            jax.ShapeDtypeStruct((b, IDX_HEADS, s), F32),
            jax.ShapeDtypeStruct((b, nchunk, t, A_WIDTH), BF16),
            jax.ShapeDtypeStruct((b, nchunk, t, IDX_DIM), BF16),
            jax.ShapeDtypeStruct((b, s, B_WIDTH), BF16),
        ),
        grid=(b, nchunk),
        in_specs=[
            pl.BlockSpec((None, t, d), lambda bi, i: (bi, i, 0)),
            const((1, d)),
            const(wk.shape), const(wik.shape), const(wugg.shape), const(wqt.shape),
            const(wvt.shape), const(wiqt.shape), const(wiwt.shape), const(conv_w.shape),
        ],
        out_specs=(
            pl.BlockSpec((None, A_WIDTH, t), lambda bi, i: (bi, 0, i)),
            pl.BlockSpec((None, None, A_HEADS * PV_ROWS, t), lambda bi, i: (bi, i, 0, 0)),
            pl.BlockSpec((None, IDX_HEADS * IDX_DIM, t), lambda bi, i: (bi, 0, i)),
            pl.BlockSpec((None, IDX_HEADS, t), lambda bi, i: (bi, 0, i)),
            pl.BlockSpec((None, None, t, A_WIDTH), lambda bi, i: (bi, i, 0, 0)),
            pl.BlockSpec((None, None, t, IDX_DIM), lambda bi, i: (bi, i, 0, 0)),
            pl.BlockSpec((None, t, B_WIDTH), lambda bi, i: (bi, i, 0)),
        ),
        scratch_shapes=[pltpu.VMEM((SUBLANES, B_WIDTH), F32)],
        compiler_params=_params("arbitrary", "arbitrary"),
        name="proj_even",
    )(x, g.reshape(1, d), wk, wik, wugg, wqt, wvt, wiqt, wiwt, conv_w)


def _bias_kernel(rb_ref, o_ref, *, thresholds, far_bucket):
    hd = pl.program_id(0)
    t = o_ref.shape[-1]
    r = lax.broadcasted_iota(I32, (t, t), 1)
    c = lax.broadcasted_iota(I32, (t, t), 0)
    for m in range(2):
        dist = t * m + r - c
        val = jnp.full((t, t), rb_ref[0, hd], F32)
        for bucket in range(1, REL_BUCKETS):
            val = jnp.where(dist >= thresholds[bucket], rb_ref[bucket, hd], val)
        val = val - rb_ref[far_bucket, hd]
        if m == 0:
            val = jnp.where(dist < 0, MASKED, val)
        o_ref[m] = val


def _bias_tiles(rel_bias, seq):
    t = ATT_TILE
    table = _t5_bucket_table(seq)
    assert np.all(np.diff(table) >= 0) and table[0] == 0 and table.max() == REL_BUCKETS - 1
    thresholds = tuple(int(np.argmax(table >= bkt)) for bkt in range(REL_BUCKETS))
    far_bucket = int(table[2 * t - 1])
    assert np.all(table[2 * t - 1:] == far_bucket)
    return pl.pallas_call(
        functools.partial(_bias_kernel, thresholds=thresholds, far_bucket=far_bucket),
        out_shape=jax.ShapeDtypeStruct((A_HEADS, 2, t, t), F32),
        grid=(A_HEADS,),
        in_specs=[pl.BlockSpec(memory_space=pltpu.SMEM)],
        out_specs=pl.BlockSpec((None, 2, t, t), lambda h: (h, 0, 0, 0)),
        compiler_params=_params("parallel"),
        name="rel_bias_tiles",
    )(rel_bias)


def _dsa_kernel(qt_ref, iqt_ref, iwt_ref, k_ref, vt_ref, ik_ref, bias_ref, o_ref,
                key_ref, acc_ref, m_ref, qz_ref, mask_ref, s_ref):
    i = pl.program_id(1)
    t = ATT_TILE
    nk = i + 1
    iw = iwt_ref[...]

    def sort_keys(j):
        ikc = ik_ref[j]
        sc = jnp.zeros((t, t), F32)
        for hd in range(IDX_HEADS):
            d = jnp.dot(ikc, iqt_ref[hd * IDX_DIM:(hd + 1) * IDX_DIM, :],
                        preferred_element_type=F32)
            sc = sc + jnp.maximum(d, 0.0) * iw[hd:hd + 1, :]
        sc = sc + 0.0
        bits = pltpu.bitcast(sc, I32)
        return bits ^ ((bits >> 31) & 0x7FFFFFFF)

    def phase_a(j, carry):
        key_ref[j] = sort_keys(j)
        return carry

    lax.fori_loop(0, i, phase_a, 0)
    qpos = lax.broadcasted_iota(I32, (t, t), 1)
    kpos = lax.broadcasted_iota(I32, (t, t), 0)
    key_ref[i] = jnp.where(kpos <= qpos, sort_keys(i), KEY_MIN)

    def count_ge(cand):
        def body(j, cnt):
            ind = jnp.where(key_ref[j] >= cand, 1.0, 0.0)
            return cnt + jnp.sum(ind.reshape(t // SUBLANES, SUBLANES, t), axis=0)
        cnt = lax.fori_loop(0, nk, body, jnp.zeros((SUBLANES, t), F32))
        return jnp.sum(cnt, axis=0, keepdims=True)

    def phase_b(it, thr):
        cand = thr + jnp.left_shift(jnp.int32(1), 31 - it)
        return jnp.where(count_ge(cand) >= float(TOPK_MAX), cand, thr)

    thr = lax.fori_loop(0, 32, phase_b, jnp.full((1, t), KEY_MIN, I32))
    need = float(TOPK_MAX) - count_ge(thr + 1)

    m_ref[...] = jnp.full(m_ref.shape, MASKED, F32)
    acc_ref[...] = jnp.zeros(acc_ref.shape, F32)
    row = lax.broadcasted_iota(I32, (2 * A_HEAD_DIM, t), 0)
    for hd in range(A_HEADS):
        pair = hd // 2
        blk = qt_ref[2 * A_HEAD_DIM * pair:2 * A_HEAD_DIM * (pair + 1), :]
        own = (row >= A_HEAD_DIM) if hd % 2 else (row < A_HEAD_DIM)
        qz_ref[hd] = jnp.where(own, blk, jnp.zeros_like(blk))
    tri = (lax.broadcasted_iota(I32, (t, t), 0) >= lax.broadcasted_iota(I32, (t, t), 1)).astype(BF16)

    def attend(near, j, ties_before):
        kk = key_ref[j]
        is_tie = kk == thr
        tie_rank = jnp.dot(tri, jnp.where(is_tie, 1.0, 0.0).astype(BF16),
                           preferred_element_type=F32) + ties_before
        tie_add = jnp.where(is_tie, jnp.where(tie_rank <= need, 0.0, MASKED), MASKED)
        mask_ref[...] = jnp.where(kk > thr, 0.0, tie_add)
        col_max = []
        for hd in range(A_HEADS):
            pair = hd // 2
            s = jnp.dot(k_ref[j, :, 2 * A_HEAD_DIM * pair:2 * A_HEAD_DIM * (pair + 1)], qz_ref[hd],
                        preferred_element_type=F32)
            s = s + mask_ref[...]
            if near:
                s = s + bias_ref[hd, i - j]
            s_ref[hd] = s
            col_max.append(jnp.max(s, axis=0, keepdims=True))
        for hd in range(A_HEADS):
            m_old = m_ref[hd]
            m_new = jnp.maximum(m_old, col_max[hd])
            m_ref[hd] = m_new
            alpha = jnp.exp(m_old - m_new)
            p = jnp.exp(s_ref[hd] - m_new).astype(BF16)
            pv = jnp.dot(vt_ref[j, hd * PV_ROWS:(hd + 1) * PV_ROWS, :], p, preferred_element_type=F32)
            acc_ref[hd] = alpha * acc_ref[hd] + pv
        return tie_rank[t - 1:t, :]

    first_near = jnp.maximum(i - 1, 0)
    ties = lax.fori_loop(0, first_near, functools.partial(attend, False), jnp.zeros((1, t), F32))
    lax.fori_loop(first_near, nk, functools.partial(attend, True), ties)

    outs = []
    for hd in range(A_HEADS):
        outs.append(acc_ref[hd, :A_HEAD_DIM, :] / acc_ref[hd, A_HEAD_DIM:A_HEAD_DIM + 1, :])
    o_ref[...] = jnp.concatenate(outs, axis=0).T.astype(BF16)


def _dsa_attention(qt, vt, iqt, iwt, k, ik, bias):
    b, nchunk, t, _ = k.shape
    s = nchunk * t
    return pl.pallas_call(
        _dsa_kernel,
        out_shape=jax.ShapeDtypeStruct((b, s, A_WIDTH), BF16),
        grid=(b, nchunk),
        in_specs=[
            pl.BlockSpec((None, A_WIDTH, t), lambda bi, i: (bi, 0, i)),
            pl.BlockSpec((None, IDX_HEADS * IDX_DIM, t), lambda bi, i: (bi, 0, i)),
            pl.BlockSpec((None, IDX_HEADS, t), lambda bi, i: (bi, 0, i)),
            pl.BlockSpec((None, nchunk, t, A_WIDTH), lambda bi, i: (bi, 0, 0, 0)),
            pl.BlockSpec((None, nchunk, A_HEADS * PV_ROWS, t), lambda bi, i: (bi, 0, 0, 0)),
            pl.BlockSpec((None, nchunk, t, IDX_DIM), lambda bi, i: (bi, 0, 0, 0)),
            pl.BlockSpec(bias.shape, lambda bi, i: (0, 0, 0, 0)),
        ],
        out_specs=pl.BlockSpec((None, t, A_WIDTH), lambda bi, i: (bi, i, 0)),
        scratch_shapes=[
            pltpu.VMEM((nchunk, t, t), I32),
            pltpu.VMEM((A_HEADS, PV_ROWS, t), F32),
            pltpu.VMEM((A_HEADS, 1, t), F32),
            pltpu.VMEM((A_HEADS, 2 * A_HEAD_DIM, t), BF16),
            pltpu.VMEM((t, t), F32),
            pltpu.VMEM((A_HEADS, t, t), F32),
        ],
        compiler_params=_params("parallel", "parallel"),
        name="dsa_attention",
    )(qt, iqt, iwt, k, vt, ik, bias)


def _mix_out_kernel(x_ref, a_ref, c_ref, wa_ref, wc_ref, g_ref, o_ref):
    y = jnp.dot(a_ref[...], wa_ref[...], preferred_element_type=F32)
    y = y + jnp.dot(c_ref[...], wc_ref[...], preferred_element_type=F32)
    o_ref[...] = x_ref[...] + _rms(y, g_ref[...])


def _mix_out_even(x, a, c, w_out, g_post):
    n, d = x.shape
    t = ROW_TILE
    wb = w_out.astype(BF16)
    row = lambda width: pl.BlockSpec((t, width), lambda i: (i, 0))
    const = lambda shape: pl.BlockSpec(shape, lambda i: (0, 0))
    return pl.pallas_call(
        _mix_out_kernel,
        out_shape=jax.ShapeDtypeStruct((n, d), F32),
        grid=(n // t,),
        in_specs=[row(d), row(A_WIDTH), row(B_WIDTH), const((A_WIDTH, d)), const((B_WIDTH, d)),
                  const((1, d))],
        out_specs=row(d),
        compiler_params=_params("parallel"),
        name="mix_out_even",
    )(x, a, c, wb[:A_WIDTH], wb[A_WIDTH:], g_post.reshape(1, d))


def _pool_kernel(x_ref, xprev_ref, gpre_ref, gpost_ref, w_ref, scale_ref, o_ref):
    i = pl.program_id(1)
    x = x_ref[...]
    t, d = x.shape
    gpre = gpre_ref[...]
    h = _rms(x, gpre)
    hp = _rms(xprev_ref[...], gpre) * jnp.where(i > 0, 1.0, 0.0)
    ext = jnp.concatenate([hp, h], axis=0)
    group = d // len(POOL_WINDOWS)
    pos = i * t + lax.broadcasted_iota(I32, (t, group), 0)
    ys = []
    for gi, win in enumerate(POOL_WINDOWS):
        sums = ext[:, gi * group:(gi + 1) * group]
        width = 1
        while width < win:
            sums = sums[width:] + sums[:-width]
            width *= 2
        wsum = sums[POOL_HALO - (win - 1):POOL_HALO - (win - 1) + t]
        cnt = jnp.minimum(pos + 1, win).astype(F32)
        pooled = wsum / cnt - h[:, gi * group:(gi + 1) * group]
        ys.append(jnp.dot(pooled.astype(BF16), w_ref[gi], preferred_element_type=F32))
    y = jnp.concatenate(ys, axis=-1) * scale_ref[...]
    o_ref[...] = x + _rms(y, gpost_ref[...])


def _pool_mixer(x, g_pre, g_post, pool_w, pool_scale):
    b, s, d = x.shape
    t = ROW_TILE
    const = lambda shape: pl.BlockSpec(shape, lambda bi, i: (0,) * len(shape))
    prev = t // POOL_HALO
    return pl.pallas_call(
        _pool_kernel,
        out_shape=jax.ShapeDtypeStruct((b, s, d), F32),
        grid=(b, s // t),
        in_specs=[
            pl.BlockSpec((None, t, d), lambda bi, i: (bi, i, 0)),
            pl.BlockSpec((None, POOL_HALO, d), lambda bi, i: (bi, jnp.maximum(i * prev - 1, 0), 0)),
            const((1, d)), const((1, d)), const(pool_w.shape), const((1, d)),
        ],
        out_specs=pl.BlockSpec((None, t, d), lambda bi, i: (bi, i, 0)),
        compiler_params=_params("parallel", "parallel"),
        name="pool_mixer",
    )(x, x, g_pre.reshape(1, d), g_post.reshape(1, d), pool_w.astype(BF16), pool_scale.reshape(1, d))


def _mem_kv_kernel(mem_ref, g_ref, wkt_ref, wv_ref, kt_ref, v_ref):
    mem_n = _rms(mem_ref[...], g_ref[...]).astype(BF16)
    kt = lax.dot_general(wkt_ref[...], mem_n, (((1,), (1,)), ((), ())), preferred_element_type=F32)
    kt_ref[...] = kt.astype(BF16)
    v_ref[...] = jnp.dot(mem_n, wv_ref[...], preferred_element_type=F32).astype(BF16)


def _mem_kv(mem, g, wk, wv):
    b, m, d = mem.shape
    const = lambda shape: pl.BlockSpec(shape, lambda bi: (0, 0))
    return pl.pallas_call(
        _mem_kv_kernel,
        out_shape=(jax.ShapeDtypeStruct((b, X_WIDTH, m), BF16), jax.ShapeDtypeStruct((b, m, X_WIDTH), BF16)),
        grid=(b,),
        in_specs=[pl.BlockSpec((None, m, d), lambda bi: (bi, 0, 0)), const((1, d)), const((X_WIDTH, d)),
                  const((d, X_WIDTH))],
        out_specs=(pl.BlockSpec((None, X_WIDTH, m), lambda bi: (bi, 0, 0)),
                   pl.BlockSpec((None, m, X_WIDTH), lambda bi: (bi, 0, 0))),
        compiler_params=_params("parallel"),
        name="mem_kv",
    )(mem, g.reshape(1, d), wk.astype(BF16).T, wv.astype(BF16))


def _xattn_kernel(x_ref, gpre_ref, gpost_ref, wq_ref, kt_ref, v_ref, wo_ref, o_ref):
    x = x_ref[...]
    h = _rms(x, gpre_ref[...]).astype(BF16)
    q = jnp.dot(h, wq_ref[...], preferred_element_type=F32)
    outs = []
    for hd in range(X_HEADS):
        cols = slice(hd * X_HEAD_DIM, (hd + 1) * X_HEAD_DIM)
        logits = jnp.dot(q[:, cols].astype(BF16), kt_ref[cols, :], preferred_element_type=F32)
        logits = logits * (X_HEAD_DIM ** -0.5)
        p = jnp.exp(logits - jnp.max(logits, axis=-1, keepdims=True))
        den = jnp.sum(p, axis=-1, keepdims=True)
        outs.append(jnp.dot(p.astype(BF16), v_ref[:, cols], preferred_element_type=F32) / den)
    o = jnp.concatenate(outs, axis=-1).astype(BF16)
    y = jnp.dot(o, wo_ref[...], preferred_element_type=F32)
    o_ref[...] = x + _rms(y, gpost_ref[...])


def _xattn(x, kt, v, g_pre, g_post, wq, wo):
    b, s, d = x.shape
    m = v.shape[1]
    t = ROW_TILE
    const = lambda shape: pl.BlockSpec(shape, lambda bi, i: (0, 0))
    return pl.pallas_call(
        _xattn_kernel,
        out_shape=jax.ShapeDtypeStruct((b, s, d), F32),
        grid=(b, s // t),
        in_specs=[
            pl.BlockSpec((None, t, d), lambda bi, i: (bi, i, 0)),
            const((1, d)), const((1, d)), const((d, X_WIDTH)),
            pl.BlockSpec((None, X_WIDTH, m), lambda bi, i: (bi, 0, 0)),
            pl.BlockSpec((None, m, X_WIDTH), lambda bi, i: (bi, 0, 0)),
            const((X_WIDTH, d)),
        ],
        out_specs=pl.BlockSpec((None, t, d), lambda bi, i: (bi, i, 0)),
        compiler_params=_params("parallel", "parallel"),
        name="xattn",
    )(x, g_pre.reshape(1, d), g_post.reshape(1, d), wq.astype(BF16), kt, v, wo.astype(BF16))


def kernel(x, mem, ffn_w_gate, ffn_w_up, ffn_w_down, ffn_norm_pre, ffn_norm_post, mix_norm_pre,
           mix_norm_post, even_w_in, even_conv_w, even_w_out, rel_bias, pool_w, pool_scale,
           xattn_norm_pre, xattn_mem_norm, xattn_norm_post, xattn_wq, xattn_wk, xattn_wv, xattn_wo):
    b, s, d = x.shape
    depth = ffn_w_gate.shape[0]
    assert s % ATT_TILE == 0 and s // 4 >= TOPK_MAX and (b * s) % FFN_ROWS == 0 and s % ROW_TILE == 0
    bias = _bias_tiles(rel_bias, s)

    def ffn(xx, layer, j):
        out = _ffn(xx.reshape(b * s, d), ffn_norm_pre[layer, j], ffn_norm_post[layer, j],
                   ffn_w_gate[layer, j], ffn_w_up[layer, j], ffn_w_down[layer, j])
        return out.reshape(b, s, d)

    for layer in range(depth):
        x = ffn(x, layer, 0)
        if layer % 2 == 0:
            e = layer // 2
            qt, vt, iqt, iwt, k, ik, c = _proj_even(x, mix_norm_pre[layer], even_w_in[e], even_conv_w[e])
            a = _dsa_attention(qt, vt, iqt, iwt, k, ik, bias)
            x = _mix_out_even(x.reshape(b * s, d), a.reshape(b * s, A_WIDTH), c.reshape(b * s, B_WIDTH),
                              even_w_out[e], mix_norm_post[layer]).reshape(b, s, d)
        else:
            o = layer // 2
            x = _pool_mixer(x, mix_norm_pre[layer], mix_norm_post[layer], pool_w[o], pool_scale[o])
        kt, v = _mem_kv(mem, xattn_mem_norm[layer], xattn_wk[layer], xattn_wv[layer])
        x = _xattn(x, kt, v, xattn_norm_pre[layer], xattn_norm_post[layer], xattn_wq[layer],
                   xattn_wo[layer])
        x = ffn(x, layer, 1)
    return x
```

```python
import functools
import math

import jax
import jax.numpy as jnp
import numpy as np
from jax import lax
from jax.experimental import pallas as pl
from jax.experimental.pallas import tpu as pltpu

F32 = jnp.float32
BF16 = jnp.bfloat16
I32 = jnp.int32
I16 = jnp.int16

RMS_EPS = 1e-6

A_HEADS = 8
A_HEAD_DIM = 64
A_WIDTH = A_HEADS * A_HEAD_DIM
IDX_HEADS = 8
IDX_DIM = 64
TOPK_MAX = 256
REL_BUCKETS = 32
REL_MAX_EXACT = 16
REL_MAX_DIST = 128
B_WIDTH = 512
CONV_K = 3
POOL_WINDOWS = (2, 4, 8, 16)
POOL_HALO = 16
X_HEADS = 4
X_HEAD_DIM = 128
X_WIDTH = X_HEADS * X_HEAD_DIM

LANES = 128
SUBLANES = 8
MXU_DIM = 256
VMEM_BYTES_V7X = 64 * 1024 * 1024
VMEM_LIMIT = VMEM_BYTES_V7X - 8 * 1024 * 1024

ATT_TILE = MXU_DIM
FFN_ROWS = 1024
FFN_COLS = 256
ROW_TILE = 512

BF16_SUBLANES = 16
PV_ROWS = A_HEAD_DIM + BF16_SUBLANES
MASKED = -1e30
KEY_MIN = -(2 ** 31)
HALF_MIN_OFFSET = 2 ** 15


def _t5_bucket_table(n):
    d = np.arange(n)
    nf = np.maximum(d, 1).astype(np.float32)
    ratio = np.log(nf / np.float32(REL_MAX_EXACT)) / np.float32(math.log(REL_MAX_DIST / REL_MAX_EXACT))
    large = REL_MAX_EXACT + (ratio * np.float32(REL_BUCKETS - REL_MAX_EXACT)).astype(np.int32)
    large = np.minimum(large, REL_BUCKETS - 1)
    return np.where(d < REL_MAX_EXACT, d, large)


def _rms(x, g):
    ms = jnp.mean(x * x, axis=-1, keepdims=True)
    return x * lax.rsqrt(ms + RMS_EPS) * g


def _params(*semantics):
    return pltpu.CompilerParams(dimension_semantics=semantics, vmem_limit_bytes=VMEM_LIMIT)


def _ffn_kernel(x_ref, gpre_ref, gpost_ref, wg_ref, wu_ref, wd_ref, o_ref, h_ref):
    f = pl.program_id(1)

    @pl.when(f == 0)
    def _():
        h_ref[...] = _rms(x_ref[...], gpre_ref[...]).astype(BF16)

        o_ref[...] = jnp.zeros_like(o_ref)

    half = h_ref.shape[0] // 2
    halves = (slice(0, half), slice(half, 2 * half))
    gu = []
    for rows in halves:
        h = h_ref[rows, :]
        gu.append((jnp.dot(h, wg_ref[...], preferred_element_type=F32),
                   jnp.dot(h, wu_ref[...], preferred_element_type=F32)))
    for rows, (g, u) in zip(halves, gu):
        a = (g * jax.nn.sigmoid(g) * u).astype(BF16)
        o_ref[rows, :] += jnp.dot(a, wd_ref[...], preferred_element_type=F32)

    @pl.when(f == pl.num_programs(1) - 1)
    def _():
        o_ref[...] = x_ref[...] + 0.5 * _rms(o_ref[...], gpost_ref[...])


def _ffn(x, g_pre, g_post, w_gate, w_up, w_down):
    n, d = x.shape
    dff = w_gate.shape[1]
    return pl.pallas_call(
        _ffn_kernel,
        out_shape=jax.ShapeDtypeStruct((n, d), F32),
        grid=(n // FFN_ROWS, dff // FFN_COLS),
        in_specs=[
            pl.BlockSpec((FFN_ROWS, d), lambda i, f: (i, 0)),
            pl.BlockSpec((1, d), lambda i, f: (0, 0)),
            pl.BlockSpec((1, d), lambda i, f: (0, 0)),
            pl.BlockSpec((d, FFN_COLS), lambda i, f: (0, f)),
            pl.BlockSpec((d, FFN_COLS), lambda i, f: (0, f)),
            pl.BlockSpec((FFN_COLS, d), lambda i, f: (f, 0)),
        ],
        out_specs=pl.BlockSpec((FFN_ROWS, d), lambda i, f: (i, 0)),
        scratch_shapes=[pltpu.VMEM((FFN_ROWS, d), BF16)],
        compiler_params=_params("parallel", "arbitrary"),
        name="ffn",
    )(x, g_pre.reshape(1, d), g_post.reshape(1, d), w_gate.astype(BF16), w_up.astype(BF16),
      w_down.astype(BF16))


def _proj_kernel(x_ref, g_ref, wk_ref, wik_ref, wugg_ref, wqt_ref, wvt_ref, wiqt_ref, wiwt_ref,
                 convw_ref, qt_ref, vt_ref, iqt_ref, iwt_ref, k_ref, ik_ref, c_ref, carry_ref):
    i = pl.program_id(1)
    h = _rms(x_ref[...], g_ref[...]).astype(BF16)
    nt = (((1,), (1,)), ((), ()))
    qt = lax.dot_general(wqt_ref[...], h, nt, preferred_element_type=F32)
    qt_ref[...] = (qt * (A_HEAD_DIM ** -0.5)).astype(BF16)
    vt = lax.dot_general(wvt_ref[...], h, nt, preferred_element_type=F32).astype(BF16)
    ones = jnp.ones((PV_ROWS - A_HEAD_DIM, vt.shape[1]), BF16)
    for hd in range(A_HEADS):
        vt_ref[hd * PV_ROWS:hd * PV_ROWS + A_HEAD_DIM, :] = vt[hd * A_HEAD_DIM:(hd + 1) * A_HEAD_DIM]
        vt_ref[hd * PV_ROWS + A_HEAD_DIM:(hd + 1) * PV_ROWS, :] = ones
    iqt_ref[...] = lax.dot_general(wiqt_ref[...], h, nt, preferred_element_type=F32).astype(BF16)
    iwt = lax.dot_general(wiwt_ref[...], h, nt, preferred_element_type=F32)
    iwt_ref[...] = iwt[:IDX_HEADS] * (IDX_HEADS ** -0.5 * IDX_DIM ** -0.5)
    k_ref[...] = jnp.dot(h, wk_ref[...], preferred_element_type=F32).astype(BF16)
    ik_ref[...] = jnp.dot(h, wik_ref[...], preferred_element_type=F32).astype(BF16)

    ugg = jnp.dot(h, wugg_ref[...], preferred_element_type=F32)
    u = ugg[:, :B_WIDTH]
    gb = ugg[:, B_WIDTH:2 * B_WIDTH]
    gc = ugg[:, 2 * B_WIDTH:]
    gu = gc * u

    @pl.when(i == 0)
    def _():
        carry_ref[...] = jnp.zeros_like(carry_ref)

    t = gu.shape[0]
    ext = jnp.concatenate([carry_ref[...], gu], axis=0)
    w = convw_ref[...]
    y = w[2:3] * gu
    for j in range(CONV_K - 1):
        lo = SUBLANES - (CONV_K - 1) + j
        y = y + w[j:j + 1] * ext[lo:lo + t]
    c_ref[...] = (gb * y).astype(BF16)
    carry_ref[...] = gu[t - SUBLANES:]


def _proj_even(x, g, w_in, conv_w):
    b, s, d = x.shape
    t = ATT_TILE
    nchunk = s // t
    q_end, k_end, v_end = A_WIDTH, 2 * A_WIDTH, 3 * A_WIDTH
    iq_end = v_end + IDX_HEADS * IDX_DIM
    ik_end = iq_end + IDX_DIM
    iw_end = ik_end + IDX_HEADS
    wb = w_in.astype(BF16)
    wqt = wb[:, :q_end].T
    wk = wb[:, q_end:k_end]
    wvt = wb[:, k_end:v_end].T
    wiqt = wb[:, v_end:iq_end].T
    wik = wb[:, iq_end:ik_end]
    wiwt = jnp.pad(wb[:, ik_end:iw_end].T, ((0, 2 * SUBLANES - IDX_HEADS), (0, 0)))
    wugg = wb[:, iw_end:]
    const = lambda shape: pl.BlockSpec(shape, lambda bi, i: (0,) * len(shape))
    return pl.pallas_call(
        _proj_kernel,
        out_shape=(
            jax.ShapeDtypeStruct((b, A_WIDTH, s), BF16),
            jax.ShapeDtypeStruct((b, nchunk, A_HEADS * PV_ROWS, t), BF16),
            jax.ShapeDtypeStruct((b, IDX_HEADS * IDX_DIM, s), BF16),
            jax.ShapeDtypeStruct((b, IDX_HEADS, s), F32),
            jax.ShapeDtypeStruct((b, nchunk, t, A_WIDTH), BF16),
            jax.ShapeDtypeStruct((b, nchunk, t, IDX_DIM), BF16),
            jax.ShapeDtypeStruct((b, s, B_WIDTH), BF16),
        ),
        grid=(b, nchunk),
        in_specs=[
            pl.BlockSpec((None, t, d), lambda bi, i: (bi, i, 0)),
            const((1, d)),
            const(wk.shape), const(wik.shape), const(wugg.shape), const(wqt.shape),
            const(wvt.shape), const(wiqt.shape), const(wiwt.shape), const(conv_w.shape),
        ],
        out_specs=(
            pl.BlockSpec((None, A_WIDTH, t), lambda bi, i: (bi, 0, i)),
            pl.BlockSpec((None, None, A_HEADS * PV_ROWS, t), lambda bi, i: (bi, i, 0, 0)),
            pl.BlockSpec((None, IDX_HEADS * IDX_DIM, t), lambda bi, i: (bi, 0, i)),
            pl.BlockSpec((None, IDX_HEADS, t), lambda bi, i: (bi, 0, i)),
            pl.BlockSpec((None, None, t, A_WIDTH), lambda bi, i: (bi, i, 0, 0)),
            pl.BlockSpec((None, None, t, IDX_DIM), lambda bi, i: (bi, i, 0, 0)),
            pl.BlockSpec((None, t, B_WIDTH), lambda bi, i: (bi, i, 0)),
        ),
        scratch_shapes=[pltpu.VMEM((SUBLANES, B_WIDTH), F32)],
        compiler_params=_params("arbitrary", "arbitrary"),
        name="proj_even",
    )(x, g.reshape(1, d), wk, wik, wugg, wqt, wvt, wiqt, wiwt, conv_w)


def _bias_kernel(rb_ref, o_ref, *, thresholds, far_bucket):
    hd = pl.program_id(0)
    t = o_ref.shape[-1]
    r = lax.broadcasted_iota(I32, (t, t), 1)
    c = lax.broadcasted_iota(I32, (t, t), 0)
    for m in range(2):
        dist = t * m + r - c
        val = jnp.full((t, t), rb_ref[0, hd], F32)
        for bucket in range(1, REL_BUCKETS):
            val = jnp.where(dist >= thresholds[bucket], rb_ref[bucket, hd], val)
        val = val - rb_ref[far_bucket, hd]
        if m == 0:
            val = jnp.where(dist < 0, MASKED, val)
        o_ref[m] = val


def _bias_tiles(rel_bias, seq):
    t = ATT_TILE
    table = _t5_bucket_table(seq)
    assert np.all(np.diff(table) >= 0) and table[0] == 0 and table.max() == REL_BUCKETS - 1
    thresholds = tuple(int(np.argmax(table >= bkt)) for bkt in range(REL_BUCKETS))
    far_bucket = int(table[2 * t - 1])
    assert np.all(table[2 * t - 1:] == far_bucket)
    return pl.pallas_call(
        functools.partial(_bias_kernel, thresholds=thresholds, far_bucket=far_bucket),
        out_shape=jax.ShapeDtypeStruct((A_HEADS, 2, t, t), F32),
        grid=(A_HEADS,),
        in_specs=[pl.BlockSpec(memory_space=pltpu.SMEM)],
        out_specs=pl.BlockSpec((None, 2, t, t), lambda h: (h, 0, 0, 0)),
        compiler_params=_params("parallel"),
        name="rel_bias_tiles",
    )(rel_bias)


def _dsa_kernel(qt_ref, iqt_ref, iwt_ref, k_ref, vt_ref, ik_ref, bias_ref, o_ref,
                key_ref, hi_ref, lo_ref, acc_ref, m_ref, qz_ref, mask_ref, s_ref):
    i = pl.program_id(1)
    t = ATT_TILE
    nk = i + 1
    iw = iwt_ref[...]

    def sort_keys(j):
        ikc = ik_ref[j]
        sc = jnp.zeros((t, t), F32)
        for hd in range(IDX_HEADS):
            d = jnp.dot(ikc, iqt_ref[hd * IDX_DIM:(hd + 1) * IDX_DIM, :],
                        preferred_element_type=F32)
            sc = sc + jnp.maximum(d, 0.0) * iw[hd:hd + 1, :]
        sc = sc + 0.0
        bits = pltpu.bitcast(sc, I32)
        return bits ^ ((bits >> 31) & 0x7FFFFFFF)

    def store_keys(j, keys):
        key_ref[j] = keys
        hi_ref[j] = (keys >> 16).astype(I16)
        lo_ref[j] = ((keys & 0xFFFF) - HALF_MIN_OFFSET).astype(I16)

    def phase_a(j, carry):
        store_keys(j, sort_keys(j))
        return carry

    lax.fori_loop(0, i, phase_a, 0)
    qpos = lax.broadcasted_iota(I32, (t, t), 1)
    kpos = lax.broadcasted_iota(I32, (t, t), 0)
    store_keys(i, jnp.where(kpos <= qpos, sort_keys(i), KEY_MIN))

    def splat16(v):
        return jnp.broadcast_to(v, (BF16_SUBLANES, t)).astype(I16)

    def count16(ref, pred):
        def body(j, cnt):
            x = ref[j].reshape(t // BF16_SUBLANES, BF16_SUBLANES, t)
            ind = jnp.where(pred(x), jnp.int16(1), jnp.int16(0))
            parts = [ind[r] for r in range(ind.shape[0])]
            while len(parts) > 1:
                parts = [a + b for a, b in zip(parts[::2], parts[1::2])]
            return cnt + parts[0]
        cnt = lax.fori_loop(0, nk, body, jnp.zeros((BF16_SUBLANES, t), I16))
        return jnp.sum(cnt.astype(F32), axis=0, keepdims=True)

    def search16(ref):
        def step(it, thr):
            cand = thr + jnp.left_shift(jnp.int32(1), 15 - it)
            c16 = splat16(cand)
            return jnp.where(count16(ref, lambda x: x >= c16[None]) >= float(TOPK_MAX), cand, thr)
        return lax.fori_loop(0, 16, step, jnp.full((1, t), -HALF_MIN_OFFSET, I32))

    thr_hi = search16(hi_ref)
    h16 = splat16(thr_hi)

    def restrict(j, carry):
        hi = hi_ref[j].reshape(t // BF16_SUBLANES, BF16_SUBLANES, t)
        lo = lo_ref[j].reshape(t // BF16_SUBLANES, BF16_SUBLANES, t)
        other = jnp.where(hi > h16[None], jnp.int16(HALF_MIN_OFFSET - 1), jnp.int16(-HALF_MIN_OFFSET))
        lo_ref[j] = jnp.where(hi == h16[None], lo, other).reshape(t, t)
        return carry

    lax.fori_loop(0, nk, restrict, 0)
    thr_lo = search16(lo_ref)
    thr = thr_hi * (2 * HALF_MIN_OFFSET) + (thr_lo + HALF_MIN_OFFSET)

    def count_above(j, cnt):
        ind = jnp.where(key_ref[j] > thr, 1.0, 0.0)
        return cnt + jnp.sum(ind.reshape(t // SUBLANES, SUBLANES, t), axis=0)

    above = lax.fori_loop(0, nk, count_above, jnp.zeros((SUBLANES, t), F32))
    need = float(TOPK_MAX) - jnp.sum(above, axis=0, keepdims=True)

    m_ref[...] = jnp.full(m_ref.shape, MASKED, F32)
    acc_ref[...] = jnp.zeros(acc_ref.shape, F32)
    row = lax.broadcasted_iota(I32, (2 * A_HEAD_DIM, t), 0)
    for hd in range(A_HEADS):
        pair = hd // 2
        blk = qt_ref[2 * A_HEAD_DIM * pair:2 * A_HEAD_DIM * (pair + 1), :]
        own = (row >= A_HEAD_DIM) if hd % 2 else (row < A_HEAD_DIM)
        qz_ref[hd] = jnp.where(own, blk, jnp.zeros_like(blk))
    tri = (lax.broadcasted_iota(I32, (t, t), 0) >= lax.broadcasted_iota(I32, (t, t), 1)).astype(BF16)

    def attend(near, j, ties_before):
        kk = key_ref[j]
        is_tie = kk == thr
        tie_rank = jnp.dot(tri, jnp.where(is_tie, 1.0, 0.0).astype(BF16),
                           preferred_element_type=F32) + ties_before
        tie_add = jnp.where(is_tie, jnp.where(tie_rank <= need, 0.0, MASKED), MASKED)
        mask_ref[...] = jnp.where(kk > thr, 0.0, tie_add)
        col_max = []
        for hd in range(A_HEADS):
            pair = hd // 2
            s = jnp.dot(k_ref[j, :, 2 * A_HEAD_DIM * pair:2 * A_HEAD_DIM * (pair + 1)], qz_ref[hd],
                        preferred_element_type=F32)
            s = s + mask_ref[...]
            if near:
                s = s + bias_ref[hd, i - j]
            s_ref[hd] = s
            col_max.append(jnp.max(s, axis=0, keepdims=True))
        for hd in range(A_HEADS):
            m_old = m_ref[hd]
            m_new = jnp.maximum(m_old, col_max[hd])
            m_ref[hd] = m_new
            alpha = jnp.exp(m_old - m_new)
            p = jnp.exp(s_ref[hd] - m_new).astype(BF16)
            pv = jnp.dot(vt_ref[j, hd * PV_ROWS:(hd + 1) * PV_ROWS, :], p, preferred_element_type=F32)
            acc_ref[hd] = alpha * acc_ref[hd] + pv
        return tie_rank[t - 1:t, :]

    first_near = jnp.maximum(i - 1, 0)
    ties = lax.fori_loop(0, first_near, functools.partial(attend, False), jnp.zeros((1, t), F32))
    lax.fori_loop(first_near, nk, functools.partial(attend, True), ties)

    outs = []
    for hd in range(A_HEADS):
        outs.append(acc_ref[hd, :A_HEAD_DIM, :] / acc_ref[hd, A_HEAD_DIM:A_HEAD_DIM + 1, :])
    o_ref[...] = jnp.concatenate(outs, axis=0).T.astype(BF16)


def _dsa_attention(qt, vt, iqt, iwt, k, ik, bias):
    b, nchunk, t, _ = k.shape
    s = nchunk * t
    return pl.pallas_call(
        _dsa_kernel,
        out_shape=jax.ShapeDtypeStruct((b, s, A_WIDTH), BF16),
        grid=(b, nchunk),
        in_specs=[
            pl.BlockSpec((None, A_WIDTH, t), lambda bi, i: (bi, 0, i)),
            pl.BlockSpec((None, IDX_HEADS * IDX_DIM, t), lambda bi, i: (bi, 0, i)),
            pl.BlockSpec((None, IDX_HEADS, t), lambda bi, i: (bi, 0, i)),
            pl.BlockSpec((None, nchunk, t, A_WIDTH), lambda bi, i: (bi, 0, 0, 0)),
            pl.BlockSpec((None, nchunk, A_HEADS * PV_ROWS, t), lambda bi, i: (bi, 0, 0, 0)),
            pl.BlockSpec((None, nchunk, t, IDX_DIM), lambda bi, i: (bi, 0, 0, 0)),
            pl.BlockSpec(bias.shape, lambda bi, i: (0, 0, 0, 0)),
        ],
        out_specs=pl.BlockSpec((None, t, A_WIDTH), lambda bi, i: (bi, i, 0)),
        scratch_shapes=[
            pltpu.VMEM((nchunk, t, t), I32),
            pltpu.VMEM((nchunk, t, t), I16),
            pltpu.VMEM((nchunk, t, t), I16),
            pltpu.VMEM((A_HEADS, PV_ROWS, t), F32),
            pltpu.VMEM((A_HEADS, 1, t), F32),
            pltpu.VMEM((A_HEADS, 2 * A_HEAD_DIM, t), BF16),
            pltpu.VMEM((t, t), F32),
            pltpu.VMEM((A_HEADS, t, t), F32),
        ],
        compiler_params=_params("parallel", "parallel"),
        name="dsa_attention",
    )(qt, iqt, iwt, k, vt, ik, bias)


def _xattn_block(x, gpre_ref, gpost_ref, wq_ref, kt_ref, v_ref, wo_ref):
    h = _rms(x, gpre_ref[...]).astype(BF16)
    q = jnp.dot(h, wq_ref[...], preferred_element_type=F32)
    heads = [slice(hd * X_HEAD_DIM, (hd + 1) * X_HEAD_DIM) for hd in range(X_HEADS)]
    logits = [jnp.dot(q[:, cols].astype(BF16), kt_ref[cols, :], preferred_element_type=F32)
              * (X_HEAD_DIM ** -0.5) for cols in heads]
    outs = []
    for cols, lg in zip(heads, logits):
        p = jnp.exp(lg - jnp.max(lg, axis=-1, keepdims=True))
        den = jnp.sum(p, axis=-1, keepdims=True)
        outs.append(jnp.dot(p.astype(BF16), v_ref[:, cols], preferred_element_type=F32) / den)
    o = jnp.concatenate(outs, axis=-1).astype(BF16)
    y = jnp.dot(o, wo_ref[...], preferred_element_type=F32)
    return x + _rms(y, gpost_ref[...])


def _xattn_operands(d, m, g_pre, g_post, wq, kt, v, wo):
    const = lambda shape: pl.BlockSpec(shape, lambda bi, i: (0, 0))
    specs = [const((1, d)), const((1, d)), const((d, X_WIDTH)),
             pl.BlockSpec((None, X_WIDTH, m), lambda bi, i: (bi, 0, 0)),
             pl.BlockSpec((None, m, X_WIDTH), lambda bi, i: (bi, 0, 0)),
             const((X_WIDTH, d))]
    return specs, (g_pre.reshape(1, d), g_post.reshape(1, d), wq.astype(BF16), kt, v, wo.astype(BF16))


def _mix_out_xattn_kernel(x_ref, a_ref, c_ref, wa_ref, wc_ref, g_ref, *rest):
    *xattn_refs, o_ref = rest
    y = jnp.dot(a_ref[...], wa_ref[...], preferred_element_type=F32)
    y = y + jnp.dot(c_ref[...], wc_ref[...], preferred_element_type=F32)
    x1 = x_ref[...] + _rms(y, g_ref[...])
    o_ref[...] = _xattn_block(x1, *xattn_refs)


def _mix_out_xattn(x, a, c, w_out, g_post, xattn_args):
    b, s, d = x.shape
    t = ROW_TILE
    wb = w_out.astype(BF16)
    row = lambda width: pl.BlockSpec((None, t, width), lambda bi, i: (bi, i, 0))
    const = lambda shape: pl.BlockSpec(shape, lambda bi, i: (0, 0))
    xspecs, xarrays = _xattn_operands(d, xattn_args[3].shape[-1], *xattn_args)
    return pl.pallas_call(
        _mix_out_xattn_kernel,
        out_shape=jax.ShapeDtypeStruct((b, s, d), F32),
        grid=(b, s // t),
        in_specs=[row(d), row(A_WIDTH), row(B_WIDTH), const((A_WIDTH, d)), const((B_WIDTH, d)),
                  const((1, d))] + xspecs,
        out_specs=row(d),
        compiler_params=_params("parallel", "parallel"),
        name="mix_out_xattn",
    )(x, a, c, wb[:A_WIDTH], wb[A_WIDTH:], g_post.reshape(1, d), *xarrays)


def _pool_xattn_kernel(x_ref, xprev_ref, gpre_ref, gpost_ref, w_ref, scale_ref, *rest):
    *xattn_refs, o_ref = rest
    i = pl.program_id(1)
    x = x_ref[...]
    t, d = x.shape
    gpre = gpre_ref[...]
    h = _rms(x, gpre)
    hp = _rms(xprev_ref[...], gpre) * jnp.where(i > 0, 1.0, 0.0)
    ext = jnp.concatenate([hp, h], axis=0)
    group = d // len(POOL_WINDOWS)
    pos = i * t + lax.broadcasted_iota(I32, (t, group), 0)
    ys = []
    for gi, win in enumerate(POOL_WINDOWS):
        sums = ext[:, gi * group:(gi + 1) * group]
        width = 1
        while width < win:
            sums = sums[width:] + sums[:-width]
            width *= 2
        wsum = sums[POOL_HALO - (win - 1):POOL_HALO - (win - 1) + t]
        cnt = jnp.minimum(pos + 1, win).astype(F32)
        pooled = wsum / cnt - h[:, gi * group:(gi + 1) * group]
        ys.append(jnp.dot(pooled.astype(BF16), w_ref[gi], preferred_element_type=F32))
    y = jnp.concatenate(ys, axis=-1) * scale_ref[...]
    x1 = x + _rms(y, gpost_ref[...])
    o_ref[...] = _xattn_block(x1, *xattn_refs)


def _pool_xattn(x, g_pre, g_post, pool_w, pool_scale, xattn_args):
    b, s, d = x.shape
    t = ROW_TILE
    const = lambda shape: pl.BlockSpec(shape, lambda bi, i: (0,) * len(shape))
    prev = t // POOL_HALO
    xspecs, xarrays = _xattn_operands(d, xattn_args[3].shape[-1], *xattn_args)
    return pl.pallas_call(
        _pool_xattn_kernel,
        out_shape=jax.ShapeDtypeStruct((b, s, d), F32),
        grid=(b, s // t),
        in_specs=[
            pl.BlockSpec((None, t, d), lambda bi, i: (bi, i, 0)),
            pl.BlockSpec((None, POOL_HALO, d), lambda bi, i: (bi, jnp.maximum(i * prev - 1, 0), 0)),
            const((1, d)), const((1, d)), const(pool_w.shape), const((1, d)),
        ] + xspecs,
        out_specs=pl.BlockSpec((None, t, d), lambda bi, i: (bi, i, 0)),
        compiler_params=_params("parallel", "parallel"),
        name="pool_xattn",
    )(x, x, g_pre.reshape(1, d), g_post.reshape(1, d), pool_w.astype(BF16), pool_scale.reshape(1, d),
      *xarrays)


def _mem_kv_kernel(mem_ref, g_ref, wkt_ref, wv_ref, kt_ref, v_ref):
    mem_n = _rms(mem_ref[...], g_ref[...]).astype(BF16)
    kt = lax.dot_general(wkt_ref[...], mem_n, (((1,), (1,)), ((), ())), preferred_element_type=F32)
    kt_ref[...] = kt.astype(BF16)
    v_ref[...] = jnp.dot(mem_n, wv_ref[...], preferred_element_type=F32).astype(BF16)


def _mem_kv(mem, g, wk, wv):
    b, m, d = mem.shape
    const = lambda shape: pl.BlockSpec(shape, lambda bi: (0, 0))
    return pl.pallas_call(
        _mem_kv_kernel,
        out_shape=(jax.ShapeDtypeStruct((b, X_WIDTH, m), BF16), jax.ShapeDtypeStruct((b, m, X_WIDTH), BF16)),
        grid=(b,),
        in_specs=[pl.BlockSpec((None, m, d), lambda bi: (bi, 0, 0)), const((1, d)), const((X_WIDTH, d)),
                  const((d, X_WIDTH))],
        out_specs=(pl.BlockSpec((None, X_WIDTH, m), lambda bi: (bi, 0, 0)),
                   pl.BlockSpec((None, m, X_WIDTH), lambda bi: (bi, 0, 0))),
        compiler_params=_params("parallel"),
        name="mem_kv",
    )(mem, g.reshape(1, d), wk.astype(BF16).T, wv.astype(BF16))


def kernel(x, mem, ffn_w_gate, ffn_w_up, ffn_w_down, ffn_norm_pre, ffn_norm_post, mix_norm_pre,
           mix_norm_post, even_w_in, even_conv_w, even_w_out, rel_bias, pool_w, pool_scale,
           xattn_norm_pre, xattn_mem_norm, xattn_norm_post, xattn_wq, xattn_wk, xattn_wv, xattn_wo):
    b, s, d = x.shape
    depth = ffn_w_gate.shape[0]
    assert s % ATT_TILE == 0 and s // 4 >= TOPK_MAX and (b * s) % FFN_ROWS == 0 and s % ROW_TILE == 0
    bias = _bias_tiles(rel_bias, s)

    def ffn(xx, layer, j):
        out = _ffn(xx.reshape(b * s, d), ffn_norm_pre[layer, j], ffn_norm_post[layer, j],
                   ffn_w_gate[layer, j], ffn_w_up[layer, j], ffn_w_down[layer, j])
        return out.reshape(b, s, d)

    for layer in range(depth):
        x = ffn(x, layer, 0)
        kt, v = _mem_kv(mem, xattn_mem_norm[layer], xattn_wk[layer], xattn_wv[layer])
        xattn_args = (xattn_norm_pre[layer], xattn_norm_post[layer], xattn_wq[layer], kt, v,
                      xattn_wo[layer])
        if layer % 2 == 0:
            e = layer // 2
            qt, vt, iqt, iwt, k, ik, c = _proj_even(x, mix_norm_pre[layer], even_w_in[e], even_conv_w[e])
            a = _dsa_attention(qt, vt, iqt, iwt, k, ik, bias)
            x = _mix_out_xattn(x, a, c, even_w_out[e], mix_norm_post[layer], xattn_args)
        else:
            o = layer // 2
            x = _pool_xattn(x, mix_norm_pre[layer], mix_norm_post[layer], pool_w[o], pool_scale[o],
                            xattn_args)
        x = ffn(x, layer, 1)
    return x
```

```python
import functools
import math

import jax
import jax.numpy as jnp
import numpy as np
from jax import lax
from jax.experimental import pallas as pl
from jax.experimental.pallas import tpu as pltpu

F32 = jnp.float32
BF16 = jnp.bfloat16
I32 = jnp.int32
I16 = jnp.int16

RMS_EPS = 1e-6

A_HEADS = 8
A_HEAD_DIM = 64
A_WIDTH = A_HEADS * A_HEAD_DIM
IDX_HEADS = 8
IDX_DIM = 64
TOPK_MAX = 256
REL_BUCKETS = 32
REL_MAX_EXACT = 16
REL_MAX_DIST = 128
B_WIDTH = 512
CONV_K = 3
POOL_WINDOWS = (2, 4, 8, 16)
POOL_HALO = 16
X_HEADS = 4
X_HEAD_DIM = 128
X_WIDTH = X_HEADS * X_HEAD_DIM

LANES = 128
SUBLANES = 8
MXU_DIM = 256
VMEM_BYTES_V7X = 64 * 1024 * 1024
VMEM_LIMIT = VMEM_BYTES_V7X - 8 * 1024 * 1024

ATT_TILE = MXU_DIM
FFN_ROWS = 512
FFN_COLS = MXU_DIM
ROW_TILE = 512

BF16_SUBLANES = 16
PV_ROWS = A_HEAD_DIM + BF16_SUBLANES
LOG2E = math.log2(math.e)
MASKED = -1e30
KEY_MIN = -(2 ** 31)
HALF_MIN_OFFSET = 2 ** 15


def _t5_bucket_table(n):
    d = np.arange(n)
    nf = np.maximum(d, 1).astype(np.float32)
    ratio = np.log(nf / np.float32(REL_MAX_EXACT)) / np.float32(math.log(REL_MAX_DIST / REL_MAX_EXACT))
    large = REL_MAX_EXACT + (ratio * np.float32(REL_BUCKETS - REL_MAX_EXACT)).astype(np.int32)
    large = np.minimum(large, REL_BUCKETS - 1)
    return np.where(d < REL_MAX_EXACT, d, large)


def _rms(x, g):
    ms = jnp.mean(x * x, axis=-1, keepdims=True)
    return x * lax.rsqrt(ms + RMS_EPS) * g


def _params(*semantics):
    return pltpu.CompilerParams(dimension_semantics=semantics, vmem_limit_bytes=VMEM_LIMIT)


def _ffn_kernel(x_ref, gpre_ref, gpost_ref, wg_ref, wu_ref, wd_ref, o_ref, h_ref, a_ref):
    x = x_ref[...]
    h_ref[...] = _rms(x, gpre_ref[...]).astype(BF16)
    for f in range(a_ref.shape[1] // FFN_COLS):
        cols = slice(f * FFN_COLS, (f + 1) * FFN_COLS)
        g = jnp.dot(h_ref[...], wg_ref[:, cols], preferred_element_type=F32)
        u = jnp.dot(h_ref[...], wu_ref[:, cols], preferred_element_type=F32)
        a_ref[:, cols] = (g * jax.nn.sigmoid(g) * u).astype(BF16)
    y = jnp.dot(a_ref[...], wd_ref[...], preferred_element_type=F32)
    o_ref[...] = x + 0.5 * _rms(y, gpost_ref[...])


def _ffn(x, g_pre, g_post, w_gate, w_up, w_down):
    n, d = x.shape
    dff = w_gate.shape[1]
    assert dff % FFN_COLS == 0
    const = lambda shape: pl.BlockSpec(shape, lambda i: (0, 0))
    return pl.pallas_call(
        _ffn_kernel,
        out_shape=jax.ShapeDtypeStruct((n, d), F32),
        grid=(n // FFN_ROWS,),
        in_specs=[pl.BlockSpec((FFN_ROWS, d), lambda i: (i, 0)), const((1, d)), const((1, d)),
                  const((d, dff)), const((d, dff)), const((dff, d))],
        out_specs=pl.BlockSpec((FFN_ROWS, d), lambda i: (i, 0)),
        scratch_shapes=[pltpu.VMEM((FFN_ROWS, d), BF16), pltpu.VMEM((FFN_ROWS, dff), BF16)],
        compiler_params=_params("parallel"),
        name="ffn",
    )(x, g_pre.reshape(1, d), g_post.reshape(1, d), w_gate.astype(BF16), w_up.astype(BF16),
      w_down.astype(BF16))


def _proj_kernel(x_ref, g_ref, wk_ref, wik_ref, wugg_ref, wqt_ref, wvt_ref, wiqt_ref, wiwt_ref,
                 convw_ref, qt_ref, vt_ref, iqt_ref, iwt_ref, k_ref, ik_ref, c_ref, carry_ref):
    i = pl.program_id(1)
    h = _rms(x_ref[...], g_ref[...]).astype(BF16)
    nt = (((1,), (1,)), ((), ()))
    qt = lax.dot_general(wqt_ref[...], h, nt, preferred_element_type=F32)
    qt_ref[...] = (qt * (A_HEAD_DIM ** -0.5 * LOG2E)).astype(BF16)
    vt = lax.dot_general(wvt_ref[...], h, nt, preferred_element_type=F32).astype(BF16)
    ones = jnp.ones((PV_ROWS - A_HEAD_DIM, vt.shape[1]), BF16)
    for hd in range(A_HEADS):
        vt_ref[hd * PV_ROWS:hd * PV_ROWS + A_HEAD_DIM, :] = vt[hd * A_HEAD_DIM:(hd + 1) * A_HEAD_DIM]
        vt_ref[hd * PV_ROWS + A_HEAD_DIM:(hd + 1) * PV_ROWS, :] = ones
    iqt_ref[...] = lax.dot_general(wiqt_ref[...], h, nt, preferred_element_type=F32).astype(BF16)
    iwt = lax.dot_general(wiwt_ref[...], h, nt, preferred_element_type=F32)
    iwt_ref[...] = iwt[:IDX_HEADS] * (IDX_HEADS ** -0.5 * IDX_DIM ** -0.5)
    k_ref[...] = jnp.dot(h, wk_ref[...], preferred_element_type=F32).astype(BF16)
    ik_ref[...] = jnp.dot(h, wik_ref[...], preferred_element_type=F32).astype(BF16)

    ugg = jnp.dot(h, wugg_ref[...], preferred_element_type=F32)
    u = ugg[:, :B_WIDTH]
    gb = ugg[:, B_WIDTH:2 * B_WIDTH]
    gc = ugg[:, 2 * B_WIDTH:]
    gu = gc * u

    @pl.when(i == 0)
    def _():
        carry_ref[...] = jnp.zeros_like(carry_ref)

    t = gu.shape[0]
    ext = jnp.concatenate([carry_ref[...], gu], axis=0)
    w = convw_ref[...]
    y = w[2:3] * gu
    for j in range(CONV_K - 1):
        lo = SUBLANES - (CONV_K - 1) + j
        y = y + w[j:j + 1] * ext[lo:lo + t]
    c_ref[...] = (gb * y).astype(BF16)
    carry_ref[...] = gu[t - SUBLANES:]


def _proj_even(x, g, w_in, conv_w):
    b, s, d = x.shape
    t = ATT_TILE
    nchunk = s // t
    q_end, k_end, v_end = A_WIDTH, 2 * A_WIDTH, 3 * A_WIDTH
    iq_end = v_end + IDX_HEADS * IDX_DIM
    ik_end = iq_end + IDX_DIM
    iw_end = ik_end + IDX_HEADS
    wb = w_in.astype(BF16)
    wqt = wb[:, :q_end].T
    wk = wb[:, q_end:k_end]
    wvt = wb[:, k_end:v_end].T
    wiqt = wb[:, v_end:iq_end].T
    wik = wb[:, iq_end:ik_end]
    wiwt = jnp.pad(wb[:, ik_end:iw_end].T, ((0, 2 * SUBLANES - IDX_HEADS), (0, 0)))
    wugg = wb[:, iw_end:]
    const = lambda shape: pl.BlockSpec(shape, lambda bi, i: (0,) * len(shape))
    return pl.pallas_call(
        _proj_kernel,
        out_shape=(
            jax.ShapeDtypeStruct((b, A_WIDTH, s), BF16),
            jax.ShapeDtypeStruct((b, nchunk, A_HEADS * PV_ROWS, t), BF16),
            jax.ShapeDtypeStruct((b, IDX_HEADS * IDX_DIM, s), BF16),
            jax.ShapeDtypeStruct((b, IDX_HEADS, s), F32),
            jax.ShapeDtypeStruct((b, nchunk, t, A_WIDTH), BF16),
            jax.ShapeDtypeStruct((b, nchunk, t, IDX_DIM), BF16),
            jax.ShapeDtypeStruct((b, s, B_WIDTH), BF16),
        ),
        grid=(b, nchunk),
        in_specs=[
            pl.BlockSpec((None, t, d), lambda bi, i: (bi, i, 0)),
            const((1, d)),
            const(wk.shape), const(wik.shape), const(wugg.shape), const(wqt.shape),
            const(wvt.shape), const(wiqt.shape), const(wiwt.shape), const(conv_w.shape),
        ],
        out_specs=(
            pl.BlockSpec((None, A_WIDTH, t), lambda bi, i: (bi, 0, i)),
            pl.BlockSpec((None, None, A_HEADS * PV_ROWS, t), lambda bi, i: (bi, i, 0, 0)),
            pl.BlockSpec((None, IDX_HEADS * IDX_DIM, t), lambda bi, i: (bi, 0, i)),
            pl.BlockSpec((None, IDX_HEADS, t), lambda bi, i: (bi, 0, i)),
            pl.BlockSpec((None, None, t, A_WIDTH), lambda bi, i: (bi, i, 0, 0)),
            pl.BlockSpec((None, None, t, IDX_DIM), lambda bi, i: (bi, i, 0, 0)),
            pl.BlockSpec((None, t, B_WIDTH), lambda bi, i: (bi, i, 0)),
        ),
        scratch_shapes=[pltpu.VMEM((SUBLANES, B_WIDTH), F32)],
        compiler_params=_params("arbitrary", "arbitrary"),
        name="proj_even",
    )(x, g.reshape(1, d), wk, wik, wugg, wqt, wvt, wiqt, wiwt, conv_w)


def _bias_kernel(rb_ref, o_ref, *, thresholds, far_bucket):
    hd = pl.program_id(0)
    t = o_ref.shape[-1]
    r = lax.broadcasted_iota(I32, (t, t), 1)
    c = lax.broadcasted_iota(I32, (t, t), 0)
    for m in range(2):
        dist = t * m + r - c
        val = jnp.full((t, t), rb_ref[0, hd], F32)
        for bucket in range(1, REL_BUCKETS):
            val = jnp.where(dist >= thresholds[bucket], rb_ref[bucket, hd], val)
        val = (val - rb_ref[far_bucket, hd]) * LOG2E
        if m == 0:
            val = jnp.where(dist < 0, MASKED, val)
        o_ref[m] = val
    o_ref[2] = jnp.zeros((t, t), F32)


def _bias_tiles(rel_bias, seq):
    t = ATT_TILE
    table = _t5_bucket_table(seq)
    assert np.all(np.diff(table) >= 0) and table[0] == 0 and table.max() == REL_BUCKETS - 1
    thresholds = tuple(int(np.argmax(table >= bkt)) for bkt in range(REL_BUCKETS))
    far_bucket = int(table[2 * t - 1])
    assert np.all(table[2 * t - 1:] == far_bucket)
    return pl.pallas_call(
        functools.partial(_bias_kernel, thresholds=thresholds, far_bucket=far_bucket),
        out_shape=jax.ShapeDtypeStruct((A_HEADS, 3, t, t), F32),
        grid=(A_HEADS,),
        in_specs=[pl.BlockSpec(memory_space=pltpu.SMEM)],
        out_specs=pl.BlockSpec((None, 3, t, t), lambda h: (h, 0, 0, 0)),
        compiler_params=_params("parallel"),
        name="rel_bias_tiles",
    )(rel_bias)


def _dsa_kernel(qt_ref, iqt_ref, iwt_ref, k_ref, vt_ref, ik_ref, bias_ref, o_ref,
                key_ref, hi_ref, lo_ref, acc_ref, m_ref, qz_ref, mask_ref, s_ref, cm_ref):
    i = pl.program_id(1)
    t = ATT_TILE
    nk = i + 1
    iw = iwt_ref[...]

    def sort_keys(j):
        ikc = ik_ref[j]
        sc = jnp.zeros((t, t), F32)
        for hd in range(IDX_HEADS):
            d = jnp.dot(ikc, iqt_ref[hd * IDX_DIM:(hd + 1) * IDX_DIM, :],
                        preferred_element_type=F32)
            sc = sc + jnp.maximum(d, 0.0) * iw[hd:hd + 1, :]
        sc = sc + 0.0
        bits = pltpu.bitcast(sc, I32)
        return bits ^ ((bits >> 31) & 0x7FFFFFFF)

    def store_keys(j, keys):
        key_ref[j] = keys
        hi_ref[j] = (keys >> 16).astype(I16)
        lo_ref[j] = ((keys & 0xFFFF) - HALF_MIN_OFFSET).astype(I16)

    def phase_a(j, carry):
        store_keys(j, sort_keys(j))
        return carry

    lax.fori_loop(0, i, phase_a, 0)
    qpos = lax.broadcasted_iota(I32, (t, t), 1)
    kpos = lax.broadcasted_iota(I32, (t, t), 0)
    store_keys(i, jnp.where(kpos <= qpos, sort_keys(i), KEY_MIN))

    def splat16(v):
        return jnp.broadcast_to(v, (BF16_SUBLANES, t)).astype(I16)

    def count16(ref, pred):
        def body(j, cnt):
            x = ref[j].reshape(t // BF16_SUBLANES, BF16_SUBLANES, t)
            ind = jnp.where(pred(x), jnp.int16(1), jnp.int16(0))
            parts = [ind[r] for r in range(ind.shape[0])]
            while len(parts) > 1:
                parts = [a + b for a, b in zip(parts[::2], parts[1::2])]
            return cnt + parts[0]
        cnt = lax.fori_loop(0, nk, body, jnp.zeros((BF16_SUBLANES, t), I16))
        return jnp.sum(cnt.astype(F32), axis=0, keepdims=True)

    def search16(ref):
        def step(it, thr):
            cand = thr + jnp.left_shift(jnp.int32(1), 15 - it)
            c16 = splat16(cand)
            return jnp.where(count16(ref, lambda x: x >= c16[None]) >= float(TOPK_MAX), cand, thr)
        return lax.fori_loop(0, 16, step, jnp.full((1, t), -HALF_MIN_OFFSET, I32))

    thr_hi = search16(hi_ref)
    h16 = splat16(thr_hi)

    def restrict(j, carry):
        hi = hi_ref[j].reshape(t // BF16_SUBLANES, BF16_SUBLANES, t)
        lo = lo_ref[j].reshape(t // BF16_SUBLANES, BF16_SUBLANES, t)
        other = jnp.where(hi > h16[None], jnp.int16(HALF_MIN_OFFSET - 1), jnp.int16(-HALF_MIN_OFFSET))
        lo_ref[j] = jnp.where(hi == h16[None], lo, other).reshape(t, t)
        return carry

    lax.fori_loop(0, nk, restrict, 0)
    thr_lo = search16(lo_ref)
    thr = thr_hi * (2 * HALF_MIN_OFFSET) + (thr_lo + HALF_MIN_OFFSET)

    def count_above(j, cnt):
        ind = jnp.where(key_ref[j] > thr, 1.0, 0.0)
        return cnt + jnp.sum(ind.reshape(t // SUBLANES, SUBLANES, t), axis=0)

    above = lax.fori_loop(0, nk, count_above, jnp.zeros((SUBLANES, t), F32))
    need = float(TOPK_MAX) - jnp.sum(above, axis=0, keepdims=True)

    m_ref[...] = jnp.full(m_ref.shape, MASKED, F32)
    acc_ref[...] = jnp.zeros(acc_ref.shape, F32)
    row = lax.broadcasted_iota(I32, (2 * A_HEAD_DIM, t), 0)
    for hd in range(A_HEADS):
        pair = hd // 2
        blk = qt_ref[2 * A_HEAD_DIM * pair:2 * A_HEAD_DIM * (pair + 1), :]
        own = (row >= A_HEAD_DIM) if hd % 2 else (row < A_HEAD_DIM)
        qz_ref[hd] = jnp.where(own, blk, jnp.zeros_like(blk))
    tri = (lax.broadcasted_iota(I32, (t, t), 0) >= lax.broadcasted_iota(I32, (t, t), 1)).astype(BF16)

    def select(j, ties_before):
        kk = key_ref[j]
        is_tie = kk == thr
        tie_rank = jnp.dot(tri, jnp.where(is_tie, 1.0, 0.0).astype(BF16),
                           preferred_element_type=F32) + ties_before
        tie_add = jnp.where(is_tie, jnp.where(tie_rank <= need, 0.0, MASKED), MASKED)
        mask_ref[...] = jnp.where(kk > thr, 0.0, tie_add)
        return tie_rank[t - 1:t, :]

    def scores(with_bias, slot, j, hd):
        pair = hd // 2
        s = jnp.dot(k_ref[j, :, 2 * A_HEAD_DIM * pair:2 * A_HEAD_DIM * (pair + 1)], qz_ref[hd],
                    preferred_element_type=F32)
        s = s + mask_ref[...]
        if with_bias:
            s = s + bias_ref[hd, jnp.minimum(i - j, 2)]
        s_ref[slot, hd] = s
        cm_ref[slot, hd] = jnp.max(s, axis=0, keepdims=True)

    def consume(slot, j, hd):
        m_old = m_ref[hd]
        m_new = jnp.maximum(m_old, cm_ref[slot, hd])
        m_ref[hd] = m_new
        alpha = jnp.exp2(m_old - m_new)
        p = jnp.exp2(s_ref[slot, hd] - m_new).astype(BF16)
        pv = jnp.dot(vt_ref[j, hd * PV_ROWS:(hd + 1) * PV_ROWS, :], p, preferred_element_type=F32)
        acc_ref[hd] = alpha * acc_ref[hd] + pv

    def step(with_bias, j, ties_before):
        def parity(slot):
            def run(ties):
                ties = select(j, ties)
                for hd in range(A_HEADS):
                    scores(with_bias, slot, j, hd)
                    consume(1 - slot, j - 1, hd)
                return ties
            return run
        return lax.cond(lax.rem(j, 2) == 1, parity(1), parity(0), ties_before)

    ties = select(0, jnp.zeros((1, t), F32))
    for hd in range(A_HEADS):
        scores(True, 0, 0, hd)
    first_near = jnp.maximum(i - 1, 1)
    ties = lax.fori_loop(1, first_near, functools.partial(step, False), ties)
    lax.fori_loop(first_near, nk, functools.partial(step, True), ties)
    for slot in range(2):
        @pl.when(lax.rem(i, 2) == slot)
        def _():
            for hd in range(A_HEADS):
                consume(slot, i, hd)

    outs = []
    for hd in range(A_HEADS):
        outs.append(acc_ref[hd, :A_HEAD_DIM, :] / acc_ref[hd, A_HEAD_DIM:A_HEAD_DIM + 1, :])
    o_ref[...] = jnp.concatenate(outs, axis=0).T.astype(BF16)


def _dsa_attention(qt, vt, iqt, iwt, k, ik, bias):
    b, nchunk, t, _ = k.shape
    s = nchunk * t
    return pl.pallas_call(
        _dsa_kernel,
        out_shape=jax.ShapeDtypeStruct((b, s, A_WIDTH), BF16),
        grid=(b, nchunk),
        in_specs=[
            pl.BlockSpec((None, A_WIDTH, t), lambda bi, i: (bi, 0, i)),
            pl.BlockSpec((None, IDX_HEADS * IDX_DIM, t), lambda bi, i: (bi, 0, i)),
            pl.BlockSpec((None, IDX_HEADS, t), lambda bi, i: (bi, 0, i)),
            pl.BlockSpec((None, nchunk, t, A_WIDTH), lambda bi, i: (bi, 0, 0, 0)),
            pl.BlockSpec((None, nchunk, A_HEADS * PV_ROWS, t), lambda bi, i: (bi, 0, 0, 0)),
            pl.BlockSpec((None, nchunk, t, IDX_DIM), lambda bi, i: (bi, 0, 0, 0)),
            pl.BlockSpec(bias.shape, lambda bi, i: (0, 0, 0, 0)),
        ],
        out_specs=pl.BlockSpec((None, t, A_WIDTH), lambda bi, i: (bi, i, 0)),
        scratch_shapes=[
            pltpu.VMEM((nchunk, t, t), I32),
            pltpu.VMEM((nchunk, t, t), I16),
            pltpu.VMEM((nchunk, t, t), I16),
            pltpu.VMEM((A_HEADS, PV_ROWS, t), F32),
            pltpu.VMEM((A_HEADS, 1, t), F32),
            pltpu.VMEM((A_HEADS, 2 * A_HEAD_DIM, t), BF16),
            pltpu.VMEM((t, t), F32),
            pltpu.VMEM((2, A_HEADS, t, t), F32),
            pltpu.VMEM((2, A_HEADS, 1, t), F32),
        ],
        compiler_params=_params("parallel", "parallel"),
        name="dsa_attention",
    )(qt, iqt, iwt, k, vt, ik, bias)


def _xattn_block(x, gpre_ref, gpost_ref, wq_ref, kt_ref, v_ref, wo_ref):
    h = _rms(x, gpre_ref[...]).astype(BF16)
    q = jnp.dot(h, wq_ref[...], preferred_element_type=F32)
    heads = [slice(hd * X_HEAD_DIM, (hd + 1) * X_HEAD_DIM) for hd in range(X_HEADS)]
    logits = [jnp.dot(q[:, cols].astype(BF16), kt_ref[cols, :], preferred_element_type=F32)
              * (X_HEAD_DIM ** -0.5) for cols in heads]
    outs = []
    for cols, lg in zip(heads, logits):
        p = jnp.exp(lg - jnp.max(lg, axis=-1, keepdims=True))
        den = jnp.sum(p, axis=-1, keepdims=True)
        outs.append(jnp.dot(p.astype(BF16), v_ref[:, cols], preferred_element_type=F32) / den)
    o = jnp.concatenate(outs, axis=-1).astype(BF16)
    y = jnp.dot(o, wo_ref[...], preferred_element_type=F32)
    return x + _rms(y, gpost_ref[...])


def _xattn_operands(d, m, g_pre, g_post, wq, kt, v, wo):
    const = lambda shape: pl.BlockSpec(shape, lambda bi, i: (0, 0))
    specs = [const((1, d)), const((1, d)), const((d, X_WIDTH)),
             pl.BlockSpec((None, X_WIDTH, m), lambda bi, i: (bi, 0, 0)),
             pl.BlockSpec((None, m, X_WIDTH), lambda bi, i: (bi, 0, 0)),
             const((X_WIDTH, d))]
    return specs, (g_pre.reshape(1, d), g_post.reshape(1, d), wq.astype(BF16), kt, v, wo.astype(BF16))


def _mix_out_xattn_kernel(x_ref, a_ref, c_ref, wa_ref, wc_ref, g_ref, *rest):
    *xattn_refs, o_ref = rest
    y = jnp.dot(a_ref[...], wa_ref[...], preferred_element_type=F32)
    y = y + jnp.dot(c_ref[...], wc_ref[...], preferred_element_type=F32)
    x1 = x_ref[...] + _rms(y, g_ref[...])
    o_ref[...] = _xattn_block(x1, *xattn_refs)


def _mix_out_xattn(x, a, c, w_out, g_post, xattn_args):
    b, s, d = x.shape
    t = ROW_TILE
    wb = w_out.astype(BF16)
    row = lambda width: pl.BlockSpec((None, t, width), lambda bi, i: (bi, i, 0))
    const = lambda shape: pl.BlockSpec(shape, lambda bi, i: (0, 0))
    xspecs, xarrays = _xattn_operands(d, xattn_args[3].shape[-1], *xattn_args)
    return pl.pallas_call(
        _mix_out_xattn_kernel,
        out_shape=jax.ShapeDtypeStruct((b, s, d), F32),
        grid=(b, s // t),
        in_specs=[row(d), row(A_WIDTH), row(B_WIDTH), const((A_WIDTH, d)), const((B_WIDTH, d)),
                  const((1, d))] + xspecs,
        out_specs=row(d),
        compiler_params=_params("parallel", "parallel"),
        name="mix_out_xattn",
    )(x, a, c, wb[:A_WIDTH], wb[A_WIDTH:], g_post.reshape(1, d), *xarrays)


def _pool_xattn_kernel(x_ref, xprev_ref, gpre_ref, gpost_ref, w_ref, scale_ref, *rest):
    *xattn_refs, o_ref = rest
    i = pl.program_id(1)
    x = x_ref[...]
    t, d = x.shape
    gpre = gpre_ref[...]
    h = _rms(x, gpre)
    hp = _rms(xprev_ref[...], gpre) * jnp.where(i > 0, 1.0, 0.0)
    ext = jnp.concatenate([hp, h], axis=0)
    group = d // len(POOL_WINDOWS)
    pos = i * t + lax.broadcasted_iota(I32, (t, group), 0)
    ys = []
    for gi, win in enumerate(POOL_WINDOWS):
        sums = ext[:, gi * group:(gi + 1) * group]
        width = 1
        while width < win:
            sums = sums[width:] + sums[:-width]
            width *= 2
        wsum = sums[POOL_HALO - (win - 1):POOL_HALO - (win - 1) + t]
        cnt = jnp.minimum(pos + 1, win).astype(F32)
        pooled = wsum / cnt - h[:, gi * group:(gi + 1) * group]
        ys.append(jnp.dot(pooled.astype(BF16), w_ref[gi], preferred_element_type=F32))
    y = jnp.concatenate(ys, axis=-1) * scale_ref[...]
    x1 = x + _rms(y, gpost_ref[...])
    o_ref[...] = _xattn_block(x1, *xattn_refs)


def _pool_xattn(x, g_pre, g_post, pool_w, pool_scale, xattn_args):
    b, s, d = x.shape
    t = ROW_TILE
    const = lambda shape: pl.BlockSpec(shape, lambda bi, i: (0,) * len(shape))
    prev = t // POOL_HALO
    xspecs, xarrays = _xattn_operands(d, xattn_args[3].shape[-1], *xattn_args)
    return pl.pallas_call(
        _pool_xattn_kernel,
        out_shape=jax.ShapeDtypeStruct((b, s, d), F32),
        grid=(b, s // t),
        in_specs=[
            pl.BlockSpec((None, t, d), lambda bi, i: (bi, i, 0)),
            pl.BlockSpec((None, POOL_HALO, d), lambda bi, i: (bi, jnp.maximum(i * prev - 1, 0), 0)),
            const((1, d)), const((1, d)), const(pool_w.shape), const((1, d)),
        ] + xspecs,
        out_specs=pl.BlockSpec((None, t, d), lambda bi, i: (bi, i, 0)),
        compiler_params=_params("parallel", "parallel"),
        name="pool_xattn",
    )(x, x, g_pre.reshape(1, d), g_post.reshape(1, d), pool_w.astype(BF16), pool_scale.reshape(1, d),
      *xarrays)


def _mem_kv_kernel(mem_ref, g_ref, wkt_ref, wv_ref, kt_ref, v_ref):
    mem_n = _rms(mem_ref[...], g_ref[...]).astype(BF16)
    kt = lax.dot_general(wkt_ref[...], mem_n, (((1,), (1,)), ((), ())), preferred_element_type=F32)
    kt_ref[...] = kt.astype(BF16)
    v_ref[...] = jnp.dot(mem_n, wv_ref[...], preferred_element_type=F32).astype(BF16)


def _mem_kv(mem, g, wk, wv):
    b, m, d = mem.shape
    const = lambda shape: pl.BlockSpec(shape, lambda bi: (0, 0))
    return pl.pallas_call(
        _mem_kv_kernel,
        out_shape=(jax.ShapeDtypeStruct((b, X_WIDTH, m), BF16), jax.ShapeDtypeStruct((b, m, X_WIDTH), BF16)),
        grid=(b,),
        in_specs=[pl.BlockSpec((None, m, d), lambda bi: (bi, 0, 0)), const((1, d)), const((X_WIDTH, d)),
                  const((d, X_WIDTH))],
        out_specs=(pl.BlockSpec((None, X_WIDTH, m), lambda bi: (bi, 0, 0)),
                   pl.BlockSpec((None, m, X_WIDTH), lambda bi: (bi, 0, 0))),
        compiler_params=_params("parallel"),
        name="mem_kv",
    )(mem, g.reshape(1, d), wk.astype(BF16).T, wv.astype(BF16))


def kernel(x, mem, ffn_w_gate, ffn_w_up, ffn_w_down, ffn_norm_pre, ffn_norm_post, mix_norm_pre,
           mix_norm_post, even_w_in, even_conv_w, even_w_out, rel_bias, pool_w, pool_scale,
           xattn_norm_pre, xattn_mem_norm, xattn_norm_post, xattn_wq, xattn_wk, xattn_wv, xattn_wo):
    b, s, d = x.shape
    depth = ffn_w_gate.shape[0]
    assert s % ATT_TILE == 0 and s // 4 >= TOPK_MAX and (b * s) % FFN_ROWS == 0 and s % ROW_TILE == 0
    bias = _bias_tiles(rel_bias, s)

    def ffn(xx, layer, j):
        out = _ffn(xx.reshape(b * s, d), ffn_norm_pre[layer, j], ffn_norm_post[layer, j],
                   ffn_w_gate[layer, j], ffn_w_up[layer, j], ffn_w_down[layer, j])
        return out.reshape(b, s, d)

    for layer in range(depth):
        x = ffn(x, layer, 0)
        kt, v = _mem_kv(mem, xattn_mem_norm[layer], xattn_wk[layer], xattn_wv[layer])
        xattn_args = (xattn_norm_pre[layer], xattn_norm_post[layer], xattn_wq[layer], kt, v,
                      xattn_wo[layer])
        if layer % 2 == 0:
            e = layer // 2
            qt, vt, iqt, iwt, k, ik, c = _proj_even(x, mix_norm_pre[layer], even_w_in[e], even_conv_w[e])
            a = _dsa_attention(qt, vt, iqt, iwt, k, ik, bias)
            x = _mix_out_xattn(x, a, c, even_w_out[e], mix_norm_post[layer], xattn_args)
        else:
            o = layer // 2
            x = _pool_xattn(x, mix_norm_pre[layer], mix_norm_post[layer], pool_w[o], pool_scale[o],
                            xattn_args)
        x = ffn(x, layer, 1)
    return x
```

```python
import functools
import math

import jax
import jax.numpy as jnp
import numpy as np
from jax import lax
from jax.experimental import pallas as pl
from jax.experimental.pallas import tpu as pltpu

F32 = jnp.float32
BF16 = jnp.bfloat16
I32 = jnp.int32

RMS_EPS = 1e-6

A_HEADS = 8
A_HEAD_DIM = 64
A_WIDTH = A_HEADS * A_HEAD_DIM
IDX_HEADS = 8
IDX_DIM = 64
TOPK_MAX = 256
REL_BUCKETS = 32
REL_MAX_EXACT = 16
REL_MAX_DIST = 128
B_WIDTH = 512
CONV_K = 3
POOL_WINDOWS = (2, 4, 8, 16)
POOL_HALO = 16
X_HEADS = 4
X_HEAD_DIM = 128
X_WIDTH = X_HEADS * X_HEAD_DIM

LANES = 128
SUBLANES = 8
MXU_DIM = 256
VMEM_BYTES_V7X = 64 * 1024 * 1024
VMEM_LIMIT = VMEM_BYTES_V7X - 8 * 1024 * 1024

ATT_TILE = MXU_DIM
FFN_ROWS = 512
FFN_COLS = MXU_DIM
ROW_TILE = 512

BF16_SUBLANES = 16
PV_ROWS = A_HEAD_DIM + BF16_SUBLANES
LOG2E = math.log2(math.e)
MASKED = -1e30
KEY_MIN = -(2 ** 31)
WORD = 32


def _t5_bucket_table(n):
    d = np.arange(n)
    nf = np.maximum(d, 1).astype(np.float32)
    ratio = np.log(nf / np.float32(REL_MAX_EXACT)) / np.float32(math.log(REL_MAX_DIST / REL_MAX_EXACT))
    large = REL_MAX_EXACT + (ratio * np.float32(REL_BUCKETS - REL_MAX_EXACT)).astype(np.int32)
    large = np.minimum(large, REL_BUCKETS - 1)
    return np.where(d < REL_MAX_EXACT, d, large)


def _rms(x, g):
    ms = jnp.mean(x * x, axis=-1, keepdims=True)
    return x * lax.rsqrt(ms + RMS_EPS) * g


def _params(*semantics):
    return pltpu.CompilerParams(dimension_semantics=semantics, vmem_limit_bytes=VMEM_LIMIT)


def _ffn_kernel(x_ref, gpre_ref, gpost_ref, wg_ref, wu_ref, wd_ref, o_ref, h_ref, a_ref):
    x = x_ref[...]
    h_ref[...] = _rms(x, gpre_ref[...]).astype(BF16)
    for f in range(a_ref.shape[1] // FFN_COLS):
        cols = slice(f * FFN_COLS, (f + 1) * FFN_COLS)
        g = jnp.dot(h_ref[...], wg_ref[:, cols], preferred_element_type=F32)
        u = jnp.dot(h_ref[...], wu_ref[:, cols], preferred_element_type=F32)
        a_ref[:, cols] = (g * jax.nn.sigmoid(g) * u).astype(BF16)
    y = jnp.dot(a_ref[...], wd_ref[...], preferred_element_type=F32)
    o_ref[...] = x + 0.5 * _rms(y, gpost_ref[...])


def _ffn(x, layer, j, g_pre, g_post, w_gate, w_up, w_down):
    n, d = x.shape
    dff = w_gate.shape[-1]
    assert dff % FFN_COLS == 0
    pick = lambda rows, cols: pl.BlockSpec((None, None, rows, cols), lambda i: (layer, j, 0, 0))
    depth = g_pre.shape[0]
    return pl.pallas_call(
        _ffn_kernel,
        out_shape=jax.ShapeDtypeStruct((n, d), F32),
        grid=(n // FFN_ROWS,),
        in_specs=[pl.BlockSpec((FFN_ROWS, d), lambda i: (i, 0)), pick(1, d), pick(1, d),
                  pick(d, dff), pick(d, dff), pick(dff, d)],
        out_specs=pl.BlockSpec((FFN_ROWS, d), lambda i: (i, 0)),
        scratch_shapes=[pltpu.VMEM((FFN_ROWS, d), BF16), pltpu.VMEM((FFN_ROWS, dff), BF16)],
        compiler_params=_params("parallel"),
        name="ffn",
    )(x, g_pre.reshape(depth, 2, 1, d), g_post.reshape(depth, 2, 1, d), w_gate, w_up, w_down)


def _proj_kernel(x_ref, g_ref, wk_ref, wik_ref, wugg_ref, wqt_ref, wvt_ref, wiqt_ref, wiwt_ref,
                 convw_ref, qt_ref, vt_ref, iqt_ref, iwt_ref, k_ref, ik_ref, c_ref, carry_ref):
    i = pl.program_id(1)
    h = _rms(x_ref[...], g_ref[...]).astype(BF16)
    nt = (((1,), (1,)), ((), ()))
    qt = lax.dot_general(wqt_ref[...], h, nt, preferred_element_type=F32)
    qt_ref[...] = (qt * (A_HEAD_DIM ** -0.5 * LOG2E)).astype(BF16)
    vt = lax.dot_general(wvt_ref[...], h, nt, preferred_element_type=F32).astype(BF16)
    ones = jnp.ones((PV_ROWS - A_HEAD_DIM, vt.shape[1]), BF16)
    for hd in range(A_HEADS):
        vt_ref[hd * PV_ROWS:hd * PV_ROWS + A_HEAD_DIM, :] = vt[hd * A_HEAD_DIM:(hd + 1) * A_HEAD_DIM]
        vt_ref[hd * PV_ROWS + A_HEAD_DIM:(hd + 1) * PV_ROWS, :] = ones
    iqt_ref[...] = lax.dot_general(wiqt_ref[...], h, nt, preferred_element_type=F32).astype(BF16)
    iwt = lax.dot_general(wiwt_ref[...], h, nt, preferred_element_type=F32)
    iwt_ref[...] = iwt[:IDX_HEADS] * (IDX_HEADS ** -0.5 * IDX_DIM ** -0.5)
    k_ref[...] = jnp.dot(h, wk_ref[...], preferred_element_type=F32).astype(BF16)
    ik_ref[...] = jnp.dot(h, wik_ref[...], preferred_element_type=F32).astype(BF16)

    ugg = jnp.dot(h, wugg_ref[...], preferred_element_type=F32)
    u = ugg[:, :B_WIDTH]
    gb = ugg[:, B_WIDTH:2 * B_WIDTH]
    gc = ugg[:, 2 * B_WIDTH:]
    gu = gc * u

    @pl.when(i == 0)
    def _():
        carry_ref[...] = jnp.zeros_like(carry_ref)

    t = gu.shape[0]
    ext = jnp.concatenate([carry_ref[...], gu], axis=0)
    w = convw_ref[...]
    y = w[2:3] * gu
    for j in range(CONV_K - 1):
        lo = SUBLANES - (CONV_K - 1) + j
        y = y + w[j:j + 1] * ext[lo:lo + t]
    c_ref[...] = (gb * y).astype(BF16)
    carry_ref[...] = gu[t - SUBLANES:]


def _proj_even(x, g, w_in, conv_w):
    b, s, d = x.shape
    t = ATT_TILE
    nchunk = s // t
    q_end, k_end, v_end = A_WIDTH, 2 * A_WIDTH, 3 * A_WIDTH
    iq_end = v_end + IDX_HEADS * IDX_DIM
    ik_end = iq_end + IDX_DIM
    iw_end = ik_end + IDX_HEADS
    wb = w_in.astype(BF16)
    wqt = wb[:, :q_end].T
    wk = wb[:, q_end:k_end]
    wvt = wb[:, k_end:v_end].T
    wiqt = wb[:, v_end:iq_end].T
    wik = wb[:, iq_end:ik_end]
    wiwt = jnp.pad(wb[:, ik_end:iw_end].T, ((0, 2 * SUBLANES - IDX_HEADS), (0, 0)))
    wugg = wb[:, iw_end:]
    const = lambda shape: pl.BlockSpec(shape, lambda bi, i: (0,) * len(shape))
    return pl.pallas_call(
        _proj_kernel,
        out_shape=(
            jax.ShapeDtypeStruct((b, A_WIDTH, s), BF16),
            jax.ShapeDtypeStruct((b, nchunk, A_HEADS * PV_ROWS, t), BF16),
            jax.ShapeDtypeStruct((b, IDX_HEADS * IDX_DIM, s), BF16),
            jax.ShapeDtypeStruct((b, IDX_HEADS, s), F32),
            jax.ShapeDtypeStruct((b, nchunk, t, A_WIDTH), BF16),
            jax.ShapeDtypeStruct((b, nchunk, t, IDX_DIM), BF16),
            jax.ShapeDtypeStruct((b, s, B_WIDTH), BF16),
        ),
        grid=(b, nchunk),
        in_specs=[
            pl.BlockSpec((None, t, d), lambda bi, i: (bi, i, 0)),
            const((1, d)),
            const(wk.shape), const(wik.shape), const(wugg.shape), const(wqt.shape),
            const(wvt.shape), const(wiqt.shape), const(wiwt.shape), const(conv_w.shape),
        ],
        out_specs=(
            pl.BlockSpec((None, A_WIDTH, t), lambda bi, i: (bi, 0, i)),
            pl.BlockSpec((None, None, A_HEADS * PV_ROWS, t), lambda bi, i: (bi, i, 0, 0)),
            pl.BlockSpec((None, IDX_HEADS * IDX_DIM, t), lambda bi, i: (bi, 0, i)),
            pl.BlockSpec((None, IDX_HEADS, t), lambda bi, i: (bi, 0, i)),
            pl.BlockSpec((None, None, t, A_WIDTH), lambda bi, i: (bi, i, 0, 0)),
            pl.BlockSpec((None, None, t, IDX_DIM), lambda bi, i: (bi, i, 0, 0)),
            pl.BlockSpec((None, t, B_WIDTH), lambda bi, i: (bi, i, 0)),
        ),
        scratch_shapes=[pltpu.VMEM((SUBLANES, B_WIDTH), F32)],
        compiler_params=_params("arbitrary", "arbitrary"),
        name="proj_even",
    )(x, g.reshape(1, d), wk, wik, wugg, wqt, wvt, wiqt, wiwt, conv_w)


def _bias_kernel(rb_ref, o_ref, *, thresholds, far_bucket):
    hd = pl.program_id(0)
    t = o_ref.shape[-1]
    r = lax.broadcasted_iota(I32, (t, t), 1)
    c = lax.broadcasted_iota(I32, (t, t), 0)
    for m in range(2):
        dist = t * m + r - c
        val = jnp.full((t, t), rb_ref[0, hd], F32)
        for bucket in range(1, REL_BUCKETS):
            val = jnp.where(dist >= thresholds[bucket], rb_ref[bucket, hd], val)
        val = (val - rb_ref[far_bucket, hd]) * LOG2E
        if m == 0:
            val = jnp.where(dist < 0, MASKED, val)
        o_ref[m] = val
    o_ref[2] = jnp.zeros((t, t), F32)


def _bias_tiles(rel_bias, seq):
    t = ATT_TILE
    table = _t5_bucket_table(seq)
    assert np.all(np.diff(table) >= 0) and table[0] == 0 and table.max() == REL_BUCKETS - 1
    thresholds = tuple(int(np.argmax(table >= bkt)) for bkt in range(REL_BUCKETS))
    far_bucket = int(table[2 * t - 1])
    assert np.all(table[2 * t - 1:] == far_bucket)
    return pl.pallas_call(
        functools.partial(_bias_kernel, thresholds=thresholds, far_bucket=far_bucket),
        out_shape=jax.ShapeDtypeStruct((A_HEADS, 3, t, t), F32),
        grid=(A_HEADS,),
        in_specs=[pl.BlockSpec(memory_space=pltpu.SMEM)],
        out_specs=pl.BlockSpec((None, 3, t, t), lambda h: (h, 0, 0, 0)),
        compiler_params=_params("parallel"),
        name="rel_bias_tiles",
    )(rel_bias)


def _bit_transpose32(words):
    a = list(words)
    shift, mask = WORD // 2, 0x0000FFFF
    while shift:
        k = 0
        while k < WORD:
            swap = (a[k] ^ lax.shift_right_logical(a[k + shift], jnp.int32(shift))) & mask
            a[k] = a[k] ^ swap
            a[k + shift] = a[k + shift] ^ (swap << shift)
            k = (k + shift + 1) & ~shift
        shift //= 2
        mask ^= (mask << shift) & 0xFFFFFFFF
    return a


def _dsa_kernel(qt_ref, iqt_ref, iwt_ref, k_ref, vt_ref, ik_ref, bias_ref, o_ref,
                key_ref, plane_ref, eq_ref, acc_ref, m_ref, qz_ref, mask_ref, s_ref, cm_ref):
    i = pl.program_id(1)
    t = ATT_TILE
    nk = i + 1
    iw = iwt_ref[...]

    def sort_keys(j):
        ikc = ik_ref[j]
        sc = jnp.zeros((t, t), F32)
        for hd in range(IDX_HEADS):
            d = jnp.dot(ikc, iqt_ref[hd * IDX_DIM:(hd + 1) * IDX_DIM, :],
                        preferred_element_type=F32)
            sc = sc + jnp.maximum(d, 0.0) * iw[hd:hd + 1, :]
        sc = sc + 0.0
        bits = pltpu.bitcast(sc, I32)
        return bits ^ ((bits >> 31) & 0x7FFFFFFF)

    def store_keys(j, keys):
        key_ref[j] = keys
        ukeys = keys ^ KEY_MIN
        for col in range(t // LANES):
            lanes = slice(col * LANES, (col + 1) * LANES)
            planes = _bit_transpose32([ukeys[v * SUBLANES:(v + 1) * SUBLANES, lanes] for v in range(WORD)])
            for r in range(WORD):
                plane_ref[r, j, :, lanes] = planes[r]

    def phase_a(j, carry):
        store_keys(j, sort_keys(j))
        return carry

    lax.fori_loop(0, i, phase_a, 0)
    qpos = lax.broadcasted_iota(I32, (t, t), 1)
    kpos = lax.broadcasted_iota(I32, (t, t), 0)
    store_keys(i, jnp.where(kpos <= qpos, sort_keys(i), KEY_MIN))

    nchunk = eq_ref.shape[0]
    live = lax.broadcasted_iota(I32, eq_ref.shape, 0) < nk
    eq_ref[...] = jnp.where(live, jnp.int32(-1), jnp.int32(0))

    def search_bit(r, carry):
        thr_u, above = carry
        cnt = jnp.zeros((SUBLANES, t), I32)
        for c in range(nchunk):
            cnt = cnt + lax.population_count(eq_ref[c] & plane_ref[r, c])
        cnt = jnp.sum(cnt.astype(F32), axis=0, keepdims=True)
        take = above + cnt >= float(TOPK_MAX)
        for c in range(nchunk):
            eq = eq_ref[c]
            with_bit = eq & plane_ref[r, c]
            eq_ref[c] = jnp.where(take, with_bit, eq ^ with_bit)
        bit = jnp.left_shift(jnp.int32(1), WORD - 1 - r)
        return jnp.where(take, thr_u | bit, thr_u), jnp.where(take, above, above + cnt)

    thr_u, above = lax.fori_loop(0, WORD, search_bit,
                                 (jnp.zeros((1, t), I32), jnp.zeros((1, t), F32)))
    thr = thr_u ^ KEY_MIN
    need = float(TOPK_MAX) - above

    m_ref[...] = jnp.full(m_ref.shape, MASKED, F32)
    acc_ref[...] = jnp.zeros(acc_ref.shape, F32)
    row = lax.broadcasted_iota(I32, (2 * A_HEAD_DIM, t), 0)
    for hd in range(A_HEADS):
        pair = hd // 2
        blk = qt_ref[2 * A_HEAD_DIM * pair:2 * A_HEAD_DIM * (pair + 1), :]
        own = (row >= A_HEAD_DIM) if hd % 2 else (row < A_HEAD_DIM)
        qz_ref[hd] = jnp.where(own, blk, jnp.zeros_like(blk))
    tri = (lax.broadcasted_iota(I32, (t, t), 0) >= lax.broadcasted_iota(I32, (t, t), 1)).astype(BF16)

    def select(j, ties_before):
        kk = key_ref[j]
        is_tie = kk == thr
        tie_rank = jnp.dot(tri, jnp.where(is_tie, 1.0, 0.0).astype(BF16),
                           preferred_element_type=F32) + ties_before
        tie_add = jnp.where(is_tie, jnp.where(tie_rank <= need, 0.0, MASKED), MASKED)
        mask_ref[...] = jnp.where(kk > thr, 0.0, tie_add)
        return tie_rank[t - 1:t, :]

    def scores(with_bias, slot, j, hd):
        pair = hd // 2
        s = jnp.dot(k_ref[j, :, 2 * A_HEAD_DIM * pair:2 * A_HEAD_DIM * (pair + 1)], qz_ref[hd],
                    preferred_element_type=F32)
        s = s + mask_ref[...]
        if with_bias:
            s = s + bias_ref[hd, jnp.minimum(i - j, 2)]
        s_ref[slot, hd] = s
        cm_ref[slot, hd] = jnp.max(s, axis=0, keepdims=True)

    def consume(slot, j, hd):
        m_old = m_ref[hd]
        m_new = jnp.maximum(m_old, cm_ref[slot, hd])
        m_ref[hd] = m_new
        alpha = jnp.exp2(m_old - m_new)
        p = jnp.exp2(s_ref[slot, hd] - m_new).astype(BF16)
        pv = jnp.dot(vt_ref[j, hd * PV_ROWS:(hd + 1) * PV_ROWS, :], p, preferred_element_type=F32)
        acc_ref[hd] = alpha * acc_ref[hd] + pv

    def step(with_bias, j, ties_before):
        def parity(slot):
            def run(ties):
                ties = select(j, ties)
                for hd in range(A_HEADS):
                    scores(with_bias, slot, j, hd)
                    consume(1 - slot, j - 1, hd)
                return ties
            return run
        return lax.cond(lax.rem(j, 2) == 1, parity(1), parity(0), ties_before)

    ties = select(0, jnp.zeros((1, t), F32))
    for hd in range(A_HEADS):
        scores(True, 0, 0, hd)
    first_near = jnp.maximum(i - 1, 1)
    ties = lax.fori_loop(1, first_near, functools.partial(step, False), ties)
    lax.fori_loop(first_near, nk, functools.partial(step, True), ties)
    for slot in range(2):
        @pl.when(lax.rem(i, 2) == slot)
        def _():
            for hd in range(A_HEADS):
                consume(slot, i, hd)

    outs = []
    for hd in range(A_HEADS):
        outs.append(acc_ref[hd, :A_HEAD_DIM, :] / acc_ref[hd, A_HEAD_DIM:A_HEAD_DIM + 1, :])
    o_ref[...] = jnp.concatenate(outs, axis=0).T.astype(BF16)


def _dsa_attention(qt, vt, iqt, iwt, k, ik, bias):
    b, nchunk, t, _ = k.shape
    s = nchunk * t
    return pl.pallas_call(
        _dsa_kernel,
        out_shape=jax.ShapeDtypeStruct((b, s, A_WIDTH), BF16),
        grid=(b, nchunk),
        in_specs=[
            pl.BlockSpec((None, A_WIDTH, t), lambda bi, i: (bi, 0, i)),
            pl.BlockSpec((None, IDX_HEADS * IDX_DIM, t), lambda bi, i: (bi, 0, i)),
            pl.BlockSpec((None, IDX_HEADS, t), lambda bi, i: (bi, 0, i)),
            pl.BlockSpec((None, nchunk, t, A_WIDTH), lambda bi, i: (bi, 0, 0, 0)),
            pl.BlockSpec((None, nchunk, A_HEADS * PV_ROWS, t), lambda bi, i: (bi, 0, 0, 0)),
            pl.BlockSpec((None, nchunk, t, IDX_DIM), lambda bi, i: (bi, 0, 0, 0)),
            pl.BlockSpec(bias.shape, lambda bi, i: (0, 0, 0, 0)),
        ],
        out_specs=pl.BlockSpec((None, t, A_WIDTH), lambda bi, i: (bi, i, 0)),
        scratch_shapes=[
            pltpu.VMEM((nchunk, t, t), I32),
            pltpu.VMEM((WORD, nchunk, t // WORD, t), I32),
            pltpu.VMEM((nchunk, t // WORD, t), I32),
            pltpu.VMEM((A_HEADS, PV_ROWS, t), F32),
            pltpu.VMEM((A_HEADS, 1, t), F32),
            pltpu.VMEM((A_HEADS, 2 * A_HEAD_DIM, t), BF16),
            pltpu.VMEM((t, t), F32),
            pltpu.VMEM((2, A_HEADS, t, t), F32),
            pltpu.VMEM((2, A_HEADS, 1, t), F32),
        ],
        compiler_params=_params("parallel", "parallel"),
        name="dsa_attention",
    )(qt, iqt, iwt, k, vt, ik, bias)


def _xattn_block(x, gpre_ref, gpost_ref, wq_ref, kt_ref, v_ref, wo_ref):
    h = _rms(x, gpre_ref[...]).astype(BF16)
    q = jnp.dot(h, wq_ref[...], preferred_element_type=F32)
    heads = [slice(hd * X_HEAD_DIM, (hd + 1) * X_HEAD_DIM) for hd in range(X_HEADS)]
    logits = [jnp.dot(q[:, cols].astype(BF16), kt_ref[cols, :], preferred_element_type=F32)
              * (X_HEAD_DIM ** -0.5) for cols in heads]
    outs = []
    for cols, lg in zip(heads, logits):
        p = jnp.exp(lg - jnp.max(lg, axis=-1, keepdims=True))
        den = jnp.sum(p, axis=-1, keepdims=True)
        outs.append(jnp.dot(p.astype(BF16), v_ref[:, cols], preferred_element_type=F32) / den)
    o = jnp.concatenate(outs, axis=-1).astype(BF16)
    y = jnp.dot(o, wo_ref[...], preferred_element_type=F32)
    return x + _rms(y, gpost_ref[...])


def _xattn_operands(d, m, g_pre, g_post, wq, kt, v, wo):
    const = lambda shape: pl.BlockSpec(shape, lambda bi, i: (0, 0))
    specs = [const((1, d)), const((1, d)), const((d, X_WIDTH)),
             pl.BlockSpec((None, X_WIDTH, m), lambda bi, i: (bi, 0, 0)),
             pl.BlockSpec((None, m, X_WIDTH), lambda bi, i: (bi, 0, 0)),
             const((X_WIDTH, d))]
    return specs, (g_pre.reshape(1, d), g_post.reshape(1, d), wq.astype(BF16), kt, v, wo.astype(BF16))


def _mix_out_xattn_kernel(x_ref, a_ref, c_ref, wa_ref, wc_ref, g_ref, *rest):
    *xattn_refs, o_ref = rest
    y = jnp.dot(a_ref[...], wa_ref[...], preferred_element_type=F32)
    y = y + jnp.dot(c_ref[...], wc_ref[...], preferred_element_type=F32)
    x1 = x_ref[...] + _rms(y, g_ref[...])
    o_ref[...] = _xattn_block(x1, *xattn_refs)


def _mix_out_xattn(x, a, c, w_out, g_post, xattn_args):
    b, s, d = x.shape
    t = ROW_TILE
    wb = w_out.astype(BF16)
    row = lambda width: pl.BlockSpec((None, t, width), lambda bi, i: (bi, i, 0))
    const = lambda shape: pl.BlockSpec(shape, lambda bi, i: (0, 0))
    xspecs, xarrays = _xattn_operands(d, xattn_args[3].shape[-1], *xattn_args)
    return pl.pallas_call(
        _mix_out_xattn_kernel,
        out_shape=jax.ShapeDtypeStruct((b, s, d), F32),
        grid=(b, s // t),
        in_specs=[row(d), row(A_WIDTH), row(B_WIDTH), const((A_WIDTH, d)), const((B_WIDTH, d)),
                  const((1, d))] + xspecs,
        out_specs=row(d),
        compiler_params=_params("parallel", "parallel"),
        name="mix_out_xattn",
    )(x, a, c, wb[:A_WIDTH], wb[A_WIDTH:], g_post.reshape(1, d), *xarrays)


def _pool_xattn_kernel(x_ref, xprev_ref, gpre_ref, gpost_ref, w_ref, scale_ref, *rest):
    *xattn_refs, o_ref = rest
    i = pl.program_id(1)
    x = x_ref[...]
    t, d = x.shape
    gpre = gpre_ref[...]
    h = _rms(x, gpre)
    hp = _rms(xprev_ref[...], gpre) * jnp.where(i > 0, 1.0, 0.0)
    ext = jnp.concatenate([hp, h], axis=0)
    group = d // len(POOL_WINDOWS)
    pos = i * t + lax.broadcasted_iota(I32, (t, group), 0)
    ys = []
    for gi, win in enumerate(POOL_WINDOWS):
        sums = ext[:, gi * group:(gi + 1) * group]
        width = 1
        while width < win:
            sums = sums[width:] + sums[:-width]
            width *= 2
        wsum = sums[POOL_HALO - (win - 1):POOL_HALO - (win - 1) + t]
        cnt = jnp.minimum(pos + 1, win).astype(F32)
        pooled = wsum / cnt - h[:, gi * group:(gi + 1) * group]
        ys.append(jnp.dot(pooled.astype(BF16), w_ref[gi], preferred_element_type=F32))
    y = jnp.concatenate(ys, axis=-1) * scale_ref[...]
    x1 = x + _rms(y, gpost_ref[...])
    o_ref[...] = _xattn_block(x1, *xattn_refs)


def _pool_xattn(x, g_pre, g_post, pool_w, pool_scale, xattn_args):
    b, s, d = x.shape
    t = ROW_TILE
    const = lambda shape: pl.BlockSpec(shape, lambda bi, i: (0,) * len(shape))
    prev = t // POOL_HALO
    xspecs, xarrays = _xattn_operands(d, xattn_args[3].shape[-1], *xattn_args)
    return pl.pallas_call(
        _pool_xattn_kernel,
        out_shape=jax.ShapeDtypeStruct((b, s, d), F32),
        grid=(b, s // t),
        in_specs=[
            pl.BlockSpec((None, t, d), lambda bi, i: (bi, i, 0)),
            pl.BlockSpec((None, POOL_HALO, d), lambda bi, i: (bi, jnp.maximum(i * prev - 1, 0), 0)),
            const((1, d)), const((1, d)), const(pool_w.shape), const((1, d)),
        ] + xspecs,
        out_specs=pl.BlockSpec((None, t, d), lambda bi, i: (bi, i, 0)),
        compiler_params=_params("parallel", "parallel"),
        name="pool_xattn",
    )(x, x, g_pre.reshape(1, d), g_post.reshape(1, d), pool_w.astype(BF16), pool_scale.reshape(1, d),
      *xarrays)


def _mem_kv_kernel(mem_ref, g_ref, wkt_ref, wv_ref, kt_ref, v_ref):
    mem_n = _rms(mem_ref[...], g_ref[...]).astype(BF16)
    kt = lax.dot_general(wkt_ref[...], mem_n, (((1,), (1,)), ((), ())), preferred_element_type=F32)
    kt_ref[...] = kt.astype(BF16)
    v_ref[...] = jnp.dot(mem_n, wv_ref[...], preferred_element_type=F32).astype(BF16)


def _mem_kv(mem, g, wk, wv):
    b, m, d = mem.shape
    const = lambda shape: pl.BlockSpec(shape, lambda bi: (0, 0))
    return pl.pallas_call(
        _mem_kv_kernel,
        out_shape=(jax.ShapeDtypeStruct((b, X_WIDTH, m), BF16), jax.ShapeDtypeStruct((b, m, X_WIDTH), BF16)),
        grid=(b,),
        in_specs=[pl.BlockSpec((None, m, d), lambda bi: (bi, 0, 0)), const((1, d)), const((X_WIDTH, d)),
                  const((d, X_WIDTH))],
        out_specs=(pl.BlockSpec((None, X_WIDTH, m), lambda bi: (bi, 0, 0)),
                   pl.BlockSpec((None, m, X_WIDTH), lambda bi: (bi, 0, 0))),
        compiler_params=_params("parallel"),
        name="mem_kv",
    )(mem, g.reshape(1, d), wk.astype(BF16).T, wv.astype(BF16))


def kernel(x, mem, ffn_w_gate, ffn_w_up, ffn_w_down, ffn_norm_pre, ffn_norm_post, mix_norm_pre,
           mix_norm_post, even_w_in, even_conv_w, even_w_out, rel_bias, pool_w, pool_scale,
           xattn_norm_pre, xattn_mem_norm, xattn_norm_post, xattn_wq, xattn_wk, xattn_wv, xattn_wo):
    b, s, d = x.shape
    depth = ffn_w_gate.shape[0]
    assert s % ATT_TILE == 0 and s // 4 >= TOPK_MAX and (b * s) % FFN_ROWS == 0 and s % ROW_TILE == 0
    bias = _bias_tiles(rel_bias, s)

    ffn_weights = (ffn_w_gate.astype(BF16), ffn_w_up.astype(BF16), ffn_w_down.astype(BF16))

    def ffn(xx, layer, j):
        out = _ffn(xx.reshape(b * s, d), layer, j, ffn_norm_pre, ffn_norm_post, *ffn_weights)
        return out.reshape(b, s, d)

    for layer in range(depth):
        x = ffn(x, layer, 0)
        kt, v = _mem_kv(mem, xattn_mem_norm[layer], xattn_wk[layer], xattn_wv[layer])
        xattn_args = (xattn_norm_pre[layer], xattn_norm_post[layer], xattn_wq[layer], kt, v,
                      xattn_wo[layer])
        if layer % 2 == 0:
            e = layer // 2
            qt, vt, iqt, iwt, k, ik, c = _proj_even(x, mix_norm_pre[layer], even_w_in[e], even_conv_w[e])
            a = _dsa_attention(qt, vt, iqt, iwt, k, ik, bias)
            x = _mix_out_xattn(x, a, c, even_w_out[e], mix_norm_post[layer], xattn_args)
        else:
            o = layer // 2
            x = _pool_xattn(x, mix_norm_pre[layer], mix_norm_post[layer], pool_w[o], pool_scale[o],
                            xattn_args)
        x = ffn(x, layer, 1)
    return x
```

```python
import functools
import math

import jax
import jax.numpy as jnp
import numpy as np
from jax import lax
from jax.experimental import pallas as pl
from jax.experimental.pallas import tpu as pltpu

F32 = jnp.float32
BF16 = jnp.bfloat16
I32 = jnp.int32

RMS_EPS = 1e-6

A_HEADS = 8
A_HEAD_DIM = 64
A_WIDTH = A_HEADS * A_HEAD_DIM
IDX_HEADS = 8
IDX_DIM = 64
TOPK_MAX = 256
REL_BUCKETS = 32
REL_MAX_EXACT = 16
REL_MAX_DIST = 128
B_WIDTH = 512
CONV_K = 3
POOL_WINDOWS = (2, 4, 8, 16)
POOL_HALO = 16
X_HEADS = 4
X_HEAD_DIM = 128
X_WIDTH = X_HEADS * X_HEAD_DIM

LANES = 128
SUBLANES = 8
MXU_DIM = 256
VMEM_BYTES_V7X = 64 * 1024 * 1024
VMEM_LIMIT = VMEM_BYTES_V7X - 8 * 1024 * 1024

ATT_TILE = MXU_DIM
FFN_ROWS = 512
FFN_COLS = MXU_DIM
ROW_TILE = 512

BF16_SUBLANES = 16
PV_ROWS = A_HEAD_DIM + BF16_SUBLANES
LOG2E = math.log2(math.e)
MASKED = -1e30
KEY_MIN = -(2 ** 31)
WORD = 32


def _t5_bucket_table(n):
    d = np.arange(n)
    nf = np.maximum(d, 1).astype(np.float32)
    ratio = np.log(nf / np.float32(REL_MAX_EXACT)) / np.float32(math.log(REL_MAX_DIST / REL_MAX_EXACT))
    large = REL_MAX_EXACT + (ratio * np.float32(REL_BUCKETS - REL_MAX_EXACT)).astype(np.int32)
    large = np.minimum(large, REL_BUCKETS - 1)
    return np.where(d < REL_MAX_EXACT, d, large)


def _rms(x, g):
    ms = jnp.mean(x * x, axis=-1, keepdims=True)
    return x * lax.rsqrt(ms + RMS_EPS) * g


def _params(*semantics):
    return pltpu.CompilerParams(dimension_semantics=semantics, vmem_limit_bytes=VMEM_LIMIT)


def _ffn_kernel(x_ref, gpre_ref, gpost_ref, wg_ref, wu_ref, wd_ref, o_ref, h_ref, a_ref):
    x = x_ref[...]
    h_ref[...] = _rms(x, gpre_ref[...]).astype(BF16)
    for f in range(a_ref.shape[1] // FFN_COLS):
        cols = slice(f * FFN_COLS, (f + 1) * FFN_COLS)
        g = jnp.dot(h_ref[...], wg_ref[:, cols], preferred_element_type=F32)
        u = jnp.dot(h_ref[...], wu_ref[:, cols], preferred_element_type=F32)
        a_ref[:, cols] = (g * jax.nn.sigmoid(g) * u).astype(BF16)
    y = jnp.dot(a_ref[...], wd_ref[...], preferred_element_type=F32)
    o_ref[...] = x + 0.5 * _rms(y, gpost_ref[...])


def _ffn(x, layer, j, g_pre, g_post, w_gate, w_up, w_down):
    n, d = x.shape
    dff = w_gate.shape[-1]
    assert dff % FFN_COLS == 0
    pick = lambda rows, cols: pl.BlockSpec((None, None, rows, cols), lambda i: (layer, j, 0, 0))
    depth = g_pre.shape[0]
    return pl.pallas_call(
        _ffn_kernel,
        out_shape=jax.ShapeDtypeStruct((n, d), F32),
        grid=(n // FFN_ROWS,),
        in_specs=[pl.BlockSpec((FFN_ROWS, d), lambda i: (i, 0)), pick(1, d), pick(1, d),
                  pick(d, dff), pick(d, dff), pick(dff, d)],
        out_specs=pl.BlockSpec((FFN_ROWS, d), lambda i: (i, 0)),
        scratch_shapes=[pltpu.VMEM((FFN_ROWS, d), BF16), pltpu.VMEM((FFN_ROWS, dff), BF16)],
        compiler_params=_params("parallel"),
        name="ffn",
    )(x, g_pre.reshape(depth, 2, 1, d), g_post.reshape(depth, 2, 1, d), w_gate, w_up, w_down)


def _proj_kernel(x_ref, g_ref, wk_ref, wik_ref, wugg_ref, wqt_ref, wvt_ref, wiqt_ref, wiwt_ref,
                 convw_ref, qt_ref, vt_ref, iqt_ref, iwt_ref, k_ref, ik_ref, c_ref, carry_ref):
    @pl.when(pl.program_id(1) == 0)
    def _():
        carry_ref[...] = jnp.zeros_like(carry_ref)

    h = _rms(x_ref[...], g_ref[...]).astype(BF16)
    ugg = jnp.dot(h, wugg_ref[...], preferred_element_type=F32)
    u = ugg[:, :B_WIDTH]
    gb = ugg[:, B_WIDTH:2 * B_WIDTH]
    gc = ugg[:, 2 * B_WIDTH:]
    gu = gc * u
    t = gu.shape[0]
    ext = jnp.concatenate([carry_ref[...], gu], axis=0)
    w = convw_ref[...]
    y = w[2:3] * gu
    for j in range(CONV_K - 1):
        lo = SUBLANES - (CONV_K - 1) + j
        y = y + w[j:j + 1] * ext[lo:lo + t]
    c_ref[...] = (gb * y).astype(BF16)
    carry_ref[...] = gu[t - SUBLANES:]

    nt = (((1,), (1,)), ((), ()))
    qt = lax.dot_general(wqt_ref[...], h, nt, preferred_element_type=F32)
    qt_ref[...] = (qt * (A_HEAD_DIM ** -0.5 * LOG2E)).astype(BF16)
    vt = lax.dot_general(wvt_ref[...], h, nt, preferred_element_type=F32).astype(BF16)
    ones = jnp.ones((PV_ROWS - A_HEAD_DIM, vt.shape[1]), BF16)
    for hd in range(A_HEADS):
        vt_ref[hd * PV_ROWS:hd * PV_ROWS + A_HEAD_DIM, :] = vt[hd * A_HEAD_DIM:(hd + 1) * A_HEAD_DIM]
        vt_ref[hd * PV_ROWS + A_HEAD_DIM:(hd + 1) * PV_ROWS, :] = ones
    iqt_ref[...] = lax.dot_general(wiqt_ref[...], h, nt, preferred_element_type=F32).astype(BF16)
    iwt = lax.dot_general(wiwt_ref[...], h, nt, preferred_element_type=F32)
    iwt_ref[...] = iwt[:IDX_HEADS] * (IDX_HEADS ** -0.5 * IDX_DIM ** -0.5)
    k_ref[...] = jnp.dot(h, wk_ref[...], preferred_element_type=F32).astype(BF16)
    ik_ref[...] = jnp.dot(h, wik_ref[...], preferred_element_type=F32).astype(BF16)


def _proj_even(x, g, w_in, conv_w):
    b, s, d = x.shape
    t = ATT_TILE
    nchunk = s // t
    q_end, k_end, v_end = A_WIDTH, 2 * A_WIDTH, 3 * A_WIDTH
    iq_end = v_end + IDX_HEADS * IDX_DIM
    ik_end = iq_end + IDX_DIM
    iw_end = ik_end + IDX_HEADS
    wb = w_in.astype(BF16)
    wqt = wb[:, :q_end].T
    wk = wb[:, q_end:k_end]
    wvt = wb[:, k_end:v_end].T
    wiqt = wb[:, v_end:iq_end].T
    wik = wb[:, iq_end:ik_end]
    wiwt = jnp.pad(wb[:, ik_end:iw_end].T, ((0, 2 * SUBLANES - IDX_HEADS), (0, 0)))
    wugg = wb[:, iw_end:]
    const = lambda shape: pl.BlockSpec(shape, lambda bi, i: (0,) * len(shape))
    return pl.pallas_call(
        _proj_kernel,
        out_shape=(
            jax.ShapeDtypeStruct((b, A_WIDTH, s), BF16),
            jax.ShapeDtypeStruct((b, nchunk, A_HEADS * PV_ROWS, t), BF16),
            jax.ShapeDtypeStruct((b, IDX_HEADS * IDX_DIM, s), BF16),
            jax.ShapeDtypeStruct((b, IDX_HEADS, s), F32),
            jax.ShapeDtypeStruct((b, nchunk, t, A_WIDTH), BF16),
            jax.ShapeDtypeStruct((b, nchunk, t, IDX_DIM), BF16),
            jax.ShapeDtypeStruct((b, s, B_WIDTH), BF16),
        ),
        grid=(b, nchunk),
        in_specs=[
            pl.BlockSpec((None, t, d), lambda bi, i: (bi, i, 0)),
            const((1, d)),
            const(wk.shape), const(wik.shape), const(wugg.shape), const(wqt.shape),
            const(wvt.shape), const(wiqt.shape), const(wiwt.shape), const(conv_w.shape),
        ],
        out_specs=(
            pl.BlockSpec((None, A_WIDTH, t), lambda bi, i: (bi, 0, i)),
            pl.BlockSpec((None, None, A_HEADS * PV_ROWS, t), lambda bi, i: (bi, i, 0, 0)),
            pl.BlockSpec((None, IDX_HEADS * IDX_DIM, t), lambda bi, i: (bi, 0, i)),
            pl.BlockSpec((None, IDX_HEADS, t), lambda bi, i: (bi, 0, i)),
            pl.BlockSpec((None, None, t, A_WIDTH), lambda bi, i: (bi, i, 0, 0)),
            pl.BlockSpec((None, None, t, IDX_DIM), lambda bi, i: (bi, i, 0, 0)),
            pl.BlockSpec((None, t, B_WIDTH), lambda bi, i: (bi, i, 0)),
        ),
        scratch_shapes=[pltpu.VMEM((SUBLANES, B_WIDTH), F32)],
        compiler_params=_params("arbitrary", "arbitrary"),
        name="proj_even",
    )(x, g.reshape(1, d), wk, wik, wugg, wqt, wvt, wiqt, wiwt, conv_w)


def _bias_kernel(rb_ref, o_ref, *, thresholds, far_bucket):
    hd = pl.program_id(0)
    t = o_ref.shape[-1]
    r = lax.broadcasted_iota(I32, (t, t), 1)
    c = lax.broadcasted_iota(I32, (t, t), 0)
    for m in range(2):
        dist = t * m + r - c
        val = jnp.full((t, t), rb_ref[0, hd], F32)
        for bucket in range(1, REL_BUCKETS):
            val = jnp.where(dist >= thresholds[bucket], rb_ref[bucket, hd], val)
        val = (val - rb_ref[far_bucket, hd]) * LOG2E
        if m == 0:
            val = jnp.where(dist < 0, MASKED, val)
        o_ref[m] = val
    o_ref[2] = jnp.zeros((t, t), F32)


def _bias_tiles(rel_bias, seq):
    t = ATT_TILE
    table = _t5_bucket_table(seq)
    assert np.all(np.diff(table) >= 0) and table[0] == 0 and table.max() == REL_BUCKETS - 1
    thresholds = tuple(int(np.argmax(table >= bkt)) for bkt in range(REL_BUCKETS))
    far_bucket = int(table[2 * t - 1])
    assert np.all(table[2 * t - 1:] == far_bucket)
    return pl.pallas_call(
        functools.partial(_bias_kernel, thresholds=thresholds, far_bucket=far_bucket),
        out_shape=jax.ShapeDtypeStruct((A_HEADS, 3, t, t), F32),
        grid=(A_HEADS,),
        in_specs=[pl.BlockSpec(memory_space=pltpu.SMEM)],
        out_specs=pl.BlockSpec((None, 3, t, t), lambda h: (h, 0, 0, 0)),
        compiler_params=_params("parallel"),
        name="rel_bias_tiles",
    )(rel_bias)


def _bit_transpose32(words):
    a = list(words)
    shift, mask = WORD // 2, 0x0000FFFF
    while shift:
        k = 0
        while k < WORD:
            swap = (a[k] ^ lax.shift_right_logical(a[k + shift], jnp.int32(shift))) & mask
            a[k] = a[k] ^ swap
            a[k + shift] = a[k + shift] ^ (swap << shift)
            k = (k + shift + 1) & ~shift
        shift //= 2
        mask ^= (mask << shift) & 0xFFFFFFFF
    return a


def _dsa_kernel(qt_ref, iqt_ref, iwt_ref, k_ref, vt_ref, ik_ref, bias_ref, o_ref,
                key_ref, plane_ref, eq_ref, acc_ref, m_ref, qz_ref, mask_ref, s_ref, cm_ref):
    i = pl.program_id(1)
    t = ATT_TILE
    nk = i + 1
    iw = iwt_ref[...]

    def sort_keys(j):
        ikc = ik_ref[j]
        sc = jnp.zeros((t, t), F32)
        for hd in range(IDX_HEADS):
            d = jnp.dot(ikc, iqt_ref[hd * IDX_DIM:(hd + 1) * IDX_DIM, :],
                        preferred_element_type=F32)
            sc = sc + jnp.maximum(d, 0.0) * iw[hd:hd + 1, :]
        sc = sc + 0.0
        bits = pltpu.bitcast(sc, I32)
        return bits ^ ((bits >> 31) & 0x7FFFFFFF)

    def store_keys(j, keys):
        key_ref[j] = keys
        ukeys = keys ^ KEY_MIN
        for col in range(t // LANES):
            lanes = slice(col * LANES, (col + 1) * LANES)
            planes = _bit_transpose32([ukeys[v * SUBLANES:(v + 1) * SUBLANES, lanes] for v in range(WORD)])
            for r in range(WORD):
                plane_ref[r, j, :, lanes] = planes[r]

    @pl.when(i == 0)
    def _():
        plane_ref[...] = jnp.zeros(plane_ref.shape, I32)

    def phase_a(pair, carry):
        store_keys(2 * pair, sort_keys(2 * pair))
        store_keys(2 * pair + 1, sort_keys(2 * pair + 1))
        return carry

    lax.fori_loop(0, lax.shift_right_logical(i, 1), phase_a, 0)

    @pl.when((i & 1) == 1)
    def _():
        store_keys(i - 1, sort_keys(i - 1))

    qpos = lax.broadcasted_iota(I32, (t, t), 1)
    kpos = lax.broadcasted_iota(I32, (t, t), 0)
    store_keys(i, jnp.where(kpos <= qpos, sort_keys(i), KEY_MIN))

    nchunk = eq_ref.shape[0]
    live = lax.broadcasted_iota(I32, eq_ref.shape, 0) < nk
    eq_ref[...] = jnp.where(live, jnp.int32(-1), jnp.int32(0))

    def search_bit(r, carry):
        thr_u, above = carry
        cnt = jnp.zeros((SUBLANES, t), I32)
        for c in range(nchunk):
            cnt = cnt + lax.population_count(eq_ref[c] & plane_ref[r, c])
        cnt = jnp.sum(cnt.astype(F32), axis=0, keepdims=True)
        take = above + cnt >= float(TOPK_MAX)
        for c in range(nchunk):
            eq = eq_ref[c]
            with_bit = eq & plane_ref[r, c]
            eq_ref[c] = jnp.where(take, with_bit, eq ^ with_bit)
        bit = jnp.left_shift(jnp.int32(1), WORD - 1 - r)
        return jnp.where(take, thr_u | bit, thr_u), jnp.where(take, above, above + cnt)

    thr_u, above = lax.fori_loop(0, WORD, search_bit,
                                 (jnp.zeros((1, t), I32), jnp.zeros((1, t), F32)))
    thr = thr_u ^ KEY_MIN
    need = float(TOPK_MAX) - above

    m_ref[...] = jnp.full(m_ref.shape, MASKED, F32)
    acc_ref[...] = jnp.zeros(acc_ref.shape, F32)
    row = lax.broadcasted_iota(I32, (2 * A_HEAD_DIM, t), 0)
    for hd in range(A_HEADS):
        pair = hd // 2
        blk = qt_ref[2 * A_HEAD_DIM * pair:2 * A_HEAD_DIM * (pair + 1), :]
        own = (row >= A_HEAD_DIM) if hd % 2 else (row < A_HEAD_DIM)
        qz_ref[hd] = jnp.where(own, blk, jnp.zeros_like(blk))
    tri = (lax.broadcasted_iota(I32, (t, t), 0) >= lax.broadcasted_iota(I32, (t, t), 1)).astype(BF16)

    def select(j, ties_before):
        kk = key_ref[j]
        is_tie = kk == thr
        tie_rank = jnp.dot(tri, jnp.where(is_tie, 1.0, 0.0).astype(BF16),
                           preferred_element_type=F32) + ties_before
        tie_add = jnp.where(is_tie, jnp.where(tie_rank <= need, 0.0, MASKED), MASKED)
        mask_ref[...] = jnp.where(kk > thr, 0.0, tie_add)
        return tie_rank[t - 1:t, :]

    def scores(with_bias, slot, j, hd):
        pair = hd // 2
        s = jnp.dot(k_ref[j, :, 2 * A_HEAD_DIM * pair:2 * A_HEAD_DIM * (pair + 1)], qz_ref[hd],
                    preferred_element_type=F32)
        s = s + mask_ref[...]
        if with_bias:
            s = s + bias_ref[hd, jnp.minimum(i - j, 2)]
        s_ref[slot, hd] = s
        cm_ref[slot, hd] = jnp.max(s, axis=0, keepdims=True)

    def consume(slot, j, hd):
        m_old = m_ref[hd]
        m_new = jnp.maximum(m_old, cm_ref[slot, hd])
        m_ref[hd] = m_new
        alpha = jnp.exp2(m_old - m_new)
        p = jnp.exp2(s_ref[slot, hd] - m_new).astype(BF16)
        pv = jnp.dot(vt_ref[j, hd * PV_ROWS:(hd + 1) * PV_ROWS, :], p, preferred_element_type=F32)
        acc_ref[hd] = alpha * acc_ref[hd] + pv

    def step(with_bias, j, ties_before):
        def parity(slot):
            def run(ties):
                ties = select(j, ties)
                for hd in range(A_HEADS):
                    scores(with_bias, slot, j, hd)
                    consume(1 - slot, j - 1, hd)
                return ties
            return run
        return lax.cond(lax.rem(j, 2) == 1, parity(1), parity(0), ties_before)

    ties = select(0, jnp.zeros((1, t), F32))
    for hd in range(A_HEADS):
        scores(True, 0, 0, hd)
    first_near = jnp.maximum(i - 1, 1)
    ties = lax.fori_loop(1, first_near, functools.partial(step, False), ties)
    lax.fori_loop(first_near, nk, functools.partial(step, True), ties)
    for slot in range(2):
        @pl.when(lax.rem(i, 2) == slot)
        def _():
            for hd in range(A_HEADS):
                consume(slot, i, hd)

    outs = []
    for hd in range(A_HEADS):
        outs.append(acc_ref[hd, :A_HEAD_DIM, :] / acc_ref[hd, A_HEAD_DIM:A_HEAD_DIM + 1, :])
    o_ref[...] = jnp.concatenate(outs, axis=0).T.astype(BF16)


def _dsa_attention(qt, vt, iqt, iwt, k, ik, bias):
    b, nchunk, t, _ = k.shape
    s = nchunk * t
    return pl.pallas_call(
        _dsa_kernel,
        out_shape=jax.ShapeDtypeStruct((b, s, A_WIDTH), BF16),
        grid=(b, nchunk),
        in_specs=[
            pl.BlockSpec((None, A_WIDTH, t), lambda bi, i: (bi, 0, i)),
            pl.BlockSpec((None, IDX_HEADS * IDX_DIM, t), lambda bi, i: (bi, 0, i)),
            pl.BlockSpec((None, IDX_HEADS, t), lambda bi, i: (bi, 0, i)),
            pl.BlockSpec((None, nchunk, t, A_WIDTH), lambda bi, i: (bi, 0, 0, 0)),
            pl.BlockSpec((None, nchunk, A_HEADS * PV_ROWS, t), lambda bi, i: (bi, 0, 0, 0)),
            pl.BlockSpec((None, nchunk, t, IDX_DIM), lambda bi, i: (bi, 0, 0, 0)),
            pl.BlockSpec(bias.shape, lambda bi, i: (0, 0, 0, 0)),
        ],
        out_specs=pl.BlockSpec((None, t, A_WIDTH), lambda bi, i: (bi, i, 0)),
        scratch_shapes=[
            pltpu.VMEM((nchunk, t, t), I32),
            pltpu.VMEM((WORD, nchunk, t // WORD, t), I32),
            pltpu.VMEM((nchunk, t // WORD, t), I32),
            pltpu.VMEM((A_HEADS, PV_ROWS, t), F32),
            pltpu.VMEM((A_HEADS, 1, t), F32),
            pltpu.VMEM((A_HEADS, 2 * A_HEAD_DIM, t), BF16),
            pltpu.VMEM((t, t), F32),
            pltpu.VMEM((2, A_HEADS, t, t), F32),
            pltpu.VMEM((2, A_HEADS, 1, t), F32),
        ],
        compiler_params=_params("parallel", "arbitrary"),
        name="dsa_attention",
    )(qt, iqt, iwt, k, vt, ik, bias)


def _halves(n):
    return (slice(0, n // 2), slice(n // 2, n))


def _xattn_block(parts, gpre_ref, gpost_ref, wq_ref, kt_ref, v_ref, wo_ref):
    heads = [slice(hd * X_HEAD_DIM, (hd + 1) * X_HEAD_DIM) for hd in range(X_HEADS)]
    hs = [_rms(x, gpre_ref[...]).astype(BF16) for x in parts]
    qs = [jnp.dot(h, wq_ref[...], preferred_element_type=F32) for h in hs]
    logits = [[jnp.dot(q[:, cols].astype(BF16), kt_ref[cols, :], preferred_element_type=F32)
               * (X_HEAD_DIM ** -0.5) for cols in heads] for q in qs]
    os = []
    for part_logits in logits:
        outs = []
        for cols, lg in zip(heads, part_logits):
            p = jnp.exp(lg - jnp.max(lg, axis=-1, keepdims=True))
            den = jnp.sum(p, axis=-1, keepdims=True)
            outs.append(jnp.dot(p.astype(BF16), v_ref[:, cols], preferred_element_type=F32) / den)
        os.append(jnp.concatenate(outs, axis=-1).astype(BF16))
    ys = [jnp.dot(o, wo_ref[...], preferred_element_type=F32) for o in os]
    return [x + _rms(y, gpost_ref[...]) for x, y in zip(parts, ys)]


def _xattn_operands(d, m, g_pre, g_post, wq, kt, v, wo):
    const = lambda shape: pl.BlockSpec(shape, lambda bi, i: (0, 0))
    specs = [const((1, d)), const((1, d)), const((d, X_WIDTH)),
             pl.BlockSpec((None, X_WIDTH, m), lambda bi, i: (bi, 0, 0)),
             pl.BlockSpec((None, m, X_WIDTH), lambda bi, i: (bi, 0, 0)),
             const((X_WIDTH, d))]
    return specs, (g_pre.reshape(1, d), g_post.reshape(1, d), wq.astype(BF16), kt, v, wo.astype(BF16))


def _mix_out_xattn_kernel(x_ref, a_ref, c_ref, wa_ref, wc_ref, g_ref, *rest):
    *xattn_refs, o_ref = rest
    halves = _halves(x_ref.shape[0])
    ys = [jnp.dot(a_ref[rows, :], wa_ref[...], preferred_element_type=F32)
          + jnp.dot(c_ref[rows, :], wc_ref[...], preferred_element_type=F32) for rows in halves]
    x1 = [x_ref[rows, :] + _rms(y, g_ref[...]) for rows, y in zip(halves, ys)]
    for rows, out in zip(halves, _xattn_block(x1, *xattn_refs)):
        o_ref[rows, :] = out


def _mix_out_xattn(x, a, c, w_out, g_post, xattn_args):
    b, s, d = x.shape
    t = ROW_TILE
    wb = w_out.astype(BF16)
    row = lambda width: pl.BlockSpec((None, t, width), lambda bi, i: (bi, i, 0))
    const = lambda shape: pl.BlockSpec(shape, lambda bi, i: (0, 0))
    xspecs, xarrays = _xattn_operands(d, xattn_args[3].shape[-1], *xattn_args)
    return pl.pallas_call(
        _mix_out_xattn_kernel,
        out_shape=jax.ShapeDtypeStruct((b, s, d), F32),
        grid=(b, s // t),
        in_specs=[row(d), row(A_WIDTH), row(B_WIDTH), const((A_WIDTH, d)), const((B_WIDTH, d)),
                  const((1, d))] + xspecs,
        out_specs=row(d),
        compiler_params=_params("parallel", "parallel"),
        name="mix_out_xattn",
    )(x, a, c, wb[:A_WIDTH], wb[A_WIDTH:], g_post.reshape(1, d), *xarrays)


def _pool_xattn_kernel(x_ref, xprev_ref, gpre_ref, gpost_ref, w_ref, scale_ref, *rest):
    *xattn_refs, o_ref = rest
    i = pl.program_id(1)
    x = x_ref[...]
    t, d = x.shape
    gpre = gpre_ref[...]
    h = _rms(x, gpre)
    hp = _rms(xprev_ref[...], gpre) * jnp.where(i > 0, 1.0, 0.0)
    ext = jnp.concatenate([hp, h], axis=0)
    group = d // len(POOL_WINDOWS)
    pos = i * t + lax.broadcasted_iota(I32, (t, group), 0)
    pooled = []
    for gi, win in enumerate(POOL_WINDOWS):
        sums = ext[:, gi * group:(gi + 1) * group]
        width = 1
        while width < win:
            sums = sums[width:] + sums[:-width]
            width *= 2
        wsum = sums[POOL_HALO - (win - 1):POOL_HALO - (win - 1) + t]
        cnt = jnp.minimum(pos + 1, win).astype(F32)
        pooled.append((wsum / cnt - h[:, gi * group:(gi + 1) * group]).astype(BF16))
    halves = _halves(t)
    x1 = []
    for rows in halves:
        y = jnp.concatenate([jnp.dot(pg[rows], w_ref[gi], preferred_element_type=F32)
                             for gi, pg in enumerate(pooled)], axis=-1) * scale_ref[...]
        x1.append(x[rows] + _rms(y, gpost_ref[...]))
    for rows, out in zip(halves, _xattn_block(x1, *xattn_refs)):
        o_ref[rows, :] = out


def _pool_xattn(x, g_pre, g_post, pool_w, pool_scale, xattn_args):
    b, s, d = x.shape
    t = ROW_TILE
    const = lambda shape: pl.BlockSpec(shape, lambda bi, i: (0,) * len(shape))
    prev = t // POOL_HALO
    xspecs, xarrays = _xattn_operands(d, xattn_args[3].shape[-1], *xattn_args)
    return pl.pallas_call(
        _pool_xattn_kernel,
        out_shape=jax.ShapeDtypeStruct((b, s, d), F32),
        grid=(b, s // t),
        in_specs=[
            pl.BlockSpec((None, t, d), lambda bi, i: (bi, i, 0)),
            pl.BlockSpec((None, POOL_HALO, d), lambda bi, i: (bi, jnp.maximum(i * prev - 1, 0), 0)),
            const((1, d)), const((1, d)), const(pool_w.shape), const((1, d)),
        ] + xspecs,
        out_specs=pl.BlockSpec((None, t, d), lambda bi, i: (bi, i, 0)),
        compiler_params=_params("parallel", "parallel"),
        name="pool_xattn",
    )(x, x, g_pre.reshape(1, d), g_post.reshape(1, d), pool_w.astype(BF16), pool_scale.reshape(1, d),
      *xarrays)


def _mem_kv_kernel(mem_ref, g_ref, wkt_ref, wv_ref, kt_ref, v_ref):
    mem_n = _rms(mem_ref[...], g_ref[...]).astype(BF16)
    kt = lax.dot_general(wkt_ref[...], mem_n, (((1,), (1,)), ((), ())), preferred_element_type=F32)
    kt_ref[...] = kt.astype(BF16)
    v_ref[...] = jnp.dot(mem_n, wv_ref[...], preferred_element_type=F32).astype(BF16)


def _mem_kv(mem, g, wk, wv):
    b, m, d = mem.shape
    const = lambda shape: pl.BlockSpec(shape, lambda bi: (0, 0))
    return pl.pallas_call(
        _mem_kv_kernel,
        out_shape=(jax.ShapeDtypeStruct((b, X_WIDTH, m), BF16), jax.ShapeDtypeStruct((b, m, X_WIDTH), BF16)),
        grid=(b,),
        in_specs=[pl.BlockSpec((None, m, d), lambda bi: (bi, 0, 0)), const((1, d)), const((X_WIDTH, d)),
                  const((d, X_WIDTH))],
        out_specs=(pl.BlockSpec((None, X_WIDTH, m), lambda bi: (bi, 0, 0)),
                   pl.BlockSpec((None, m, X_WIDTH), lambda bi: (bi, 0, 0))),
        compiler_params=_params("parallel"),
        name="mem_kv",
    )(mem, g.reshape(1, d), wk.astype(BF16).T, wv.astype(BF16))


def kernel(x, mem, ffn_w_gate, ffn_w_up, ffn_w_down, ffn_norm_pre, ffn_norm_post, mix_norm_pre,
           mix_norm_post, even_w_in, even_conv_w, even_w_out, rel_bias, pool_w, pool_scale,
           xattn_norm_pre, xattn_mem_norm, xattn_norm_post, xattn_wq, xattn_wk, xattn_wv, xattn_wo):
    b, s, d = x.shape
    depth = ffn_w_gate.shape[0]
    assert s % ATT_TILE == 0 and s // 4 >= TOPK_MAX and (b * s) % FFN_ROWS == 0 and s % ROW_TILE == 0
    bias = _bias_tiles(rel_bias, s)

    ffn_weights = (ffn_w_gate.astype(BF16), ffn_w_up.astype(BF16), ffn_w_down.astype(BF16))

    def ffn(xx, layer, j):
        out = _ffn(xx.reshape(b * s, d), layer, j, ffn_norm_pre, ffn_norm_post, *ffn_weights)
        return out.reshape(b, s, d)

    for layer in range(depth):
        x = ffn(x, layer, 0)
        kt, v = _mem_kv(mem, xattn_mem_norm[layer], xattn_wk[layer], xattn_wv[layer])
        xattn_args = (xattn_norm_pre[layer], xattn_norm_post[layer], xattn_wq[layer], kt, v,
                      xattn_wo[layer])
        if layer % 2 == 0:
            e = layer // 2
            qt, vt, iqt, iwt, k, ik, c = _proj_even(x, mix_norm_pre[layer], even_w_in[e], even_conv_w[e])
            a = _dsa_attention(qt, vt, iqt, iwt, k, ik, bias)
            x = _mix_out_xattn(x, a, c, even_w_out[e], mix_norm_post[layer], xattn_args)
        else:
            o = layer // 2
            x = _pool_xattn(x, mix_norm_pre[layer], mix_norm_post[layer], pool_w[o], pool_scale[o],
                            xattn_args)
        x = ffn(x, layer, 1)
    return x
```

```python
import functools
import math

import jax
import jax.numpy as jnp
import numpy as np
from jax import lax
from jax.experimental import pallas as pl
from jax.experimental.pallas import tpu as pltpu

F32 = jnp.float32
BF16 = jnp.bfloat16
I32 = jnp.int32

RMS_EPS = 1e-6

A_HEADS = 8
A_HEAD_DIM = 64
A_WIDTH = A_HEADS * A_HEAD_DIM
IDX_HEADS = 8
IDX_DIM = 64
TOPK_MAX = 256
REL_BUCKETS = 32
REL_MAX_EXACT = 16
REL_MAX_DIST = 128
B_WIDTH = 512
CONV_K = 3
POOL_WINDOWS = (2, 4, 8, 16)
POOL_HALO = 16
X_HEADS = 4
X_HEAD_DIM = 128
X_WIDTH = X_HEADS * X_HEAD_DIM

LANES = 128
SUBLANES = 8
MXU_DIM = 256
VMEM_BYTES_V7X = 64 * 1024 * 1024
VMEM_LIMIT = VMEM_BYTES_V7X - 8 * 1024 * 1024

ATT_TILE = MXU_DIM
FFN_ROWS = 1024
FFN_COLS = MXU_DIM
ROW_TILE = 512

BF16_SUBLANES = 16
PV_ROWS = A_HEAD_DIM + BF16_SUBLANES
LOG2E = math.log2(math.e)
MASKED = -1e30
KEY_MIN = -(2 ** 31)
WORD = 32


def _t5_bucket_table(n):
    d = np.arange(n)
    nf = np.maximum(d, 1).astype(np.float32)
    ratio = np.log(nf / np.float32(REL_MAX_EXACT)) / np.float32(math.log(REL_MAX_DIST / REL_MAX_EXACT))
    large = REL_MAX_EXACT + (ratio * np.float32(REL_BUCKETS - REL_MAX_EXACT)).astype(np.int32)
    large = np.minimum(large, REL_BUCKETS - 1)
    return np.where(d < REL_MAX_EXACT, d, large)


def _rms(x, g):
    ms = jnp.mean(x * x, axis=-1, keepdims=True)
    return x * lax.rsqrt(ms + RMS_EPS) * g


def _params(*semantics):
    return pltpu.CompilerParams(dimension_semantics=semantics, vmem_limit_bytes=VMEM_LIMIT)


def _ffn_kernel(x_ref, gpre_ref, gpost_ref, wg_ref, wu_ref, wd_ref, o_ref, h_ref, a_ref):
    for rows in _halves(x_ref.shape[0]):
        x = x_ref[rows, :]
        h_ref[rows, :] = _rms(x, gpre_ref[...]).astype(BF16)
        for f in range(a_ref.shape[1] // FFN_COLS):
            cols = slice(f * FFN_COLS, (f + 1) * FFN_COLS)
            g = jnp.dot(h_ref[rows, :], wg_ref[:, cols], preferred_element_type=F32)
            u = jnp.dot(h_ref[rows, :], wu_ref[:, cols], preferred_element_type=F32)
            a_ref[rows, cols] = (g * jax.nn.sigmoid(g) * u).astype(BF16)
        y = jnp.dot(a_ref[rows, :], wd_ref[...], preferred_element_type=F32)
        o_ref[rows, :] = x + 0.5 * _rms(y, gpost_ref[...])


def _ffn(x, layer, j, g_pre, g_post, w_gate, w_up, w_down):
    n, d = x.shape
    dff = w_gate.shape[-1]
    assert dff % FFN_COLS == 0
    pick = lambda rows, cols: pl.BlockSpec((None, None, rows, cols), lambda i: (layer, j, 0, 0))
    depth = g_pre.shape[0]
    return pl.pallas_call(
        _ffn_kernel,
        out_shape=jax.ShapeDtypeStruct((n, d), F32),
        grid=(n // FFN_ROWS,),
        in_specs=[pl.BlockSpec((FFN_ROWS, d), lambda i: (i, 0)), pick(1, d), pick(1, d),
                  pick(d, dff), pick(d, dff), pick(dff, d)],
        out_specs=pl.BlockSpec((FFN_ROWS, d), lambda i: (i, 0)),
        scratch_shapes=[pltpu.VMEM((FFN_ROWS, d), BF16), pltpu.VMEM((FFN_ROWS, dff), BF16)],
        compiler_params=_params("parallel"),
        name="ffn",
    )(x, g_pre.reshape(depth, 2, 1, d), g_post.reshape(depth, 2, 1, d), w_gate, w_up, w_down)


def _proj_kernel(x_ref, g_ref, wk_ref, wik_ref, wugg_ref, wqt_ref, wvt_ref, wiqt_ref, wiwt_ref,
                 convw_ref, qt_ref, vt_ref, iqt_ref, iwt_ref, k_ref, ik_ref, c_ref, carry_ref):
    @pl.when(pl.program_id(1) == 0)
    def _():
        carry_ref[...] = jnp.zeros_like(carry_ref)

    h = _rms(x_ref[...], g_ref[...]).astype(BF16)
    ugg = jnp.dot(h, wugg_ref[...], preferred_element_type=F32)
    u = ugg[:, :B_WIDTH]
    gb = ugg[:, B_WIDTH:2 * B_WIDTH]
    gc = ugg[:, 2 * B_WIDTH:]
    gu = gc * u
    t = gu.shape[0]
    ext = jnp.concatenate([carry_ref[...], gu], axis=0)
    w = convw_ref[...]
    y = w[2:3] * gu
    for j in range(CONV_K - 1):
        lo = SUBLANES - (CONV_K - 1) + j
        y = y + w[j:j + 1] * ext[lo:lo + t]
    c_ref[...] = (gb * y).astype(BF16)
    carry_ref[...] = gu[t - SUBLANES:]

    nt = (((1,), (1,)), ((), ()))
    qt = lax.dot_general(wqt_ref[...], h, nt, preferred_element_type=F32)
    qt_ref[...] = (qt * (A_HEAD_DIM ** -0.5 * LOG2E)).astype(BF16)
    vt = lax.dot_general(wvt_ref[...], h, nt, preferred_element_type=F32).astype(BF16)
    ones = jnp.ones((PV_ROWS - A_HEAD_DIM, vt.shape[1]), BF16)
    for hd in range(A_HEADS):
        vt_ref[hd * PV_ROWS:hd * PV_ROWS + A_HEAD_DIM, :] = vt[hd * A_HEAD_DIM:(hd + 1) * A_HEAD_DIM]
        vt_ref[hd * PV_ROWS + A_HEAD_DIM:(hd + 1) * PV_ROWS, :] = ones
    iqt_ref[...] = lax.dot_general(wiqt_ref[...], h, nt, preferred_element_type=F32).astype(BF16)
    iwt = lax.dot_general(wiwt_ref[...], h, nt, preferred_element_type=F32)
    iwt_ref[...] = iwt[:IDX_HEADS] * (IDX_HEADS ** -0.5 * IDX_DIM ** -0.5)
    k_ref[...] = jnp.dot(h, wk_ref[...], preferred_element_type=F32).astype(BF16)
    ik_ref[...] = jnp.dot(h, wik_ref[...], preferred_element_type=F32).astype(BF16)


def _proj_even(x, g, w_in, conv_w):
    b, s, d = x.shape
    t = ATT_TILE
    nchunk = s // t
    q_end, k_end, v_end = A_WIDTH, 2 * A_WIDTH, 3 * A_WIDTH
    iq_end = v_end + IDX_HEADS * IDX_DIM
    ik_end = iq_end + IDX_DIM
    iw_end = ik_end + IDX_HEADS
    wb = w_in.astype(BF16)
    wqt = wb[:, :q_end].T
    wk = wb[:, q_end:k_end]
    wvt = wb[:, k_end:v_end].T
    wiqt = wb[:, v_end:iq_end].T
    wik = wb[:, iq_end:ik_end]
    wiwt = jnp.pad(wb[:, ik_end:iw_end].T, ((0, 2 * SUBLANES - IDX_HEADS), (0, 0)))
    wugg = wb[:, iw_end:]
    const = lambda shape: pl.BlockSpec(shape, lambda bi, i: (0,) * len(shape))
    return pl.pallas_call(
        _proj_kernel,
        out_shape=(
            jax.ShapeDtypeStruct((b, nchunk, A_WIDTH, t), BF16),
            jax.ShapeDtypeStruct((b, nchunk, A_HEADS * PV_ROWS, t), BF16),
            jax.ShapeDtypeStruct((b, nchunk, IDX_HEADS * IDX_DIM, t), BF16),
            jax.ShapeDtypeStruct((b, nchunk, IDX_HEADS, t), F32),
            jax.ShapeDtypeStruct((b, nchunk, t, A_WIDTH), BF16),
            jax.ShapeDtypeStruct((b, nchunk, t, IDX_DIM), BF16),
            jax.ShapeDtypeStruct((b, s, B_WIDTH), BF16),
        ),
        grid=(b, nchunk),
        in_specs=[
            pl.BlockSpec((None, t, d), lambda bi, i: (bi, i, 0)),
            const((1, d)),
            const(wk.shape), const(wik.shape), const(wugg.shape), const(wqt.shape),
            const(wvt.shape), const(wiqt.shape), const(wiwt.shape), const(conv_w.shape),
        ],
        out_specs=(
            pl.BlockSpec((None, None, A_WIDTH, t), lambda bi, i: (bi, i, 0, 0)),
            pl.BlockSpec((None, None, A_HEADS * PV_ROWS, t), lambda bi, i: (bi, i, 0, 0)),
            pl.BlockSpec((None, None, IDX_HEADS * IDX_DIM, t), lambda bi, i: (bi, i, 0, 0)),
            pl.BlockSpec((None, None, IDX_HEADS, t), lambda bi, i: (bi, i, 0, 0)),
            pl.BlockSpec((None, None, t, A_WIDTH), lambda bi, i: (bi, i, 0, 0)),
            pl.BlockSpec((None, None, t, IDX_DIM), lambda bi, i: (bi, i, 0, 0)),
            pl.BlockSpec((None, t, B_WIDTH), lambda bi, i: (bi, i, 0)),
        ),
        scratch_shapes=[pltpu.VMEM((SUBLANES, B_WIDTH), F32)],
        compiler_params=_params("arbitrary", "arbitrary"),
        name="proj_even",
    )(x, g.reshape(1, d), wk, wik, wugg, wqt, wvt, wiqt, wiwt, conv_w)


def _bias_kernel(rb_ref, o_ref, *, thresholds, far_bucket):
    hd = pl.program_id(0)
    t = o_ref.shape[-1]
    r = lax.broadcasted_iota(I32, (t, t), 1)
    c = lax.broadcasted_iota(I32, (t, t), 0)
    for m in range(2):
        dist = t * m + r - c
        val = jnp.full((t, t), rb_ref[0, hd], F32)
        for bucket in range(1, REL_BUCKETS):
            val = jnp.where(dist >= thresholds[bucket], rb_ref[bucket, hd], val)
        val = (val - rb_ref[far_bucket, hd]) * LOG2E
        if m == 0:
            val = jnp.where(dist < 0, MASKED, val)
        o_ref[m] = val
    o_ref[2] = jnp.zeros((t, t), F32)


def _bias_tiles(rel_bias, seq):
    t = ATT_TILE
    table = _t5_bucket_table(seq)
    assert np.all(np.diff(table) >= 0) and table[0] == 0 and table.max() == REL_BUCKETS - 1
    thresholds = tuple(int(np.argmax(table >= bkt)) for bkt in range(REL_BUCKETS))
    far_bucket = int(table[2 * t - 1])
    assert np.all(table[2 * t - 1:] == far_bucket)
    return pl.pallas_call(
        functools.partial(_bias_kernel, thresholds=thresholds, far_bucket=far_bucket),
        out_shape=jax.ShapeDtypeStruct((A_HEADS, 3, t, t), F32),
        grid=(A_HEADS,),
        in_specs=[pl.BlockSpec(memory_space=pltpu.SMEM)],
        out_specs=pl.BlockSpec((None, 3, t, t), lambda h: (h, 0, 0, 0)),
        compiler_params=_params("parallel"),
        name="rel_bias_tiles",
    )(rel_bias)


def _bit_transpose32(words):
    a = list(words)
    shift, mask = WORD // 2, 0x0000FFFF
    while shift:
        k = 0
        while k < WORD:
            swap = (a[k] ^ lax.shift_right_logical(a[k + shift], jnp.int32(shift))) & mask
            a[k] = a[k] ^ swap
            a[k + shift] = a[k + shift] ^ (swap << shift)
            k = (k + shift + 1) & ~shift
        shift //= 2
        mask ^= (mask << shift) & 0xFFFFFFFF
    return a


def _dsa_kernel(qt_ref, iqt_ref, iwt_ref, k_ref, vt_ref, ik_ref, bias_ref, o_ref,
                sc_ref, plane_ref, eq_ref, acc_ref, m_ref, qz_ref, mask_ref, s_ref, cm_ref):
    i = pl.program_id(1)
    t = ATT_TILE
    nk = i + 1
    iw = iwt_ref[...]

    def index_scores(j):
        ikc = ik_ref[j]
        sc = jnp.zeros((t, t), F32)
        for hd in range(IDX_HEADS):
            d = jnp.dot(ikc, iqt_ref[hd * IDX_DIM:(hd + 1) * IDX_DIM, :],
                        preferred_element_type=F32)
            sc = sc + jnp.maximum(d, 0.0) * iw[hd:hd + 1, :]
        return sc

    def store_scores(j, sc):
        sc_ref[j] = sc
        bits = pltpu.bitcast(sc + 0.0, I32)
        ukeys = bits ^ ((bits >> 31) | KEY_MIN)
        for col in range(t // LANES):
            lanes = slice(col * LANES, (col + 1) * LANES)
            planes = _bit_transpose32([ukeys[v * SUBLANES:(v + 1) * SUBLANES, lanes] for v in range(WORD)])
            for r in range(WORD):
                plane_ref[r, j, :, lanes] = planes[r]

    @pl.when(i == 0)
    def _():
        plane_ref[...] = jnp.zeros(plane_ref.shape, I32)

    def phase_a(pair, carry):
        store_scores(2 * pair, index_scores(2 * pair))
        store_scores(2 * pair + 1, index_scores(2 * pair + 1))
        return carry

    lax.fori_loop(0, lax.shift_right_logical(i, 1), phase_a, 0)

    @pl.when((i & 1) == 1)
    def _():
        store_scores(i - 1, index_scores(i - 1))

    qpos = lax.broadcasted_iota(I32, (t, t), 1)
    kpos = lax.broadcasted_iota(I32, (t, t), 0)
    store_scores(i, jnp.where(kpos <= qpos, index_scores(i), -jnp.inf))

    nchunk = eq_ref.shape[0]
    live = lax.broadcasted_iota(I32, eq_ref.shape, 0) < nk
    eq_ref[...] = jnp.where(live, jnp.int32(-1), jnp.int32(0))

    def search(span):
        def search_bit(r, carry):
            thr_u, above = carry
            cnt = jnp.zeros((SUBLANES, t), I32)
            for c in range(span):
                cnt = cnt + lax.population_count(eq_ref[c] & plane_ref[r, c])
            cnt = jnp.sum(cnt.astype(F32), axis=0, keepdims=True)
            take = above + cnt >= float(TOPK_MAX)
            for c in range(span):
                eq = eq_ref[c]
                with_bit = eq & plane_ref[r, c]
                eq_ref[c] = jnp.where(take, with_bit, eq ^ with_bit)
            bit = jnp.left_shift(jnp.int32(1), WORD - 1 - r)
            return jnp.where(take, thr_u | bit, thr_u), jnp.where(take, above, above + cnt)

        return lambda: lax.fori_loop(0, WORD, search_bit,
                                     (jnp.zeros((1, t), I32), jnp.zeros((1, t), F32)))

    quarter = nchunk // 4
    thr_u, _ = lax.cond(
        nk <= 2 * quarter,
        lambda: lax.cond(nk <= quarter, search(quarter), search(2 * quarter)),
        lambda: lax.cond(nk <= 3 * quarter, search(3 * quarter), search(nchunk)))

    def key_to_score(ukey):
        return pltpu.bitcast(ukey ^ (~(ukey >> 31) | KEY_MIN), F32)

    def count_scores(*preds):
        def body(j, cnts):
            sc = sc_ref[j]
            return tuple(cnt + jnp.sum(jnp.where(pred(sc), 1.0, 0.0).reshape(t // SUBLANES, SUBLANES, t), axis=0)
                         for cnt, pred in zip(cnts, preds))
        cnts = lax.fori_loop(0, nk, body, tuple(jnp.zeros((SUBLANES, t), F32) for _ in preds))
        return [jnp.sum(cnt, axis=0, keepdims=True) for cnt in cnts]

    thr_fast = key_to_score(thr_u)
    reach, exceed = count_scores(lambda sc: sc >= thr_fast, lambda sc: sc > thr_fast)
    good = jnp.logical_and(exceed < float(TOPK_MAX), reach >= float(TOPK_MAX))

    def search_by_value():
        def step(r, ukey):
            cand = ukey | jnp.left_shift(jnp.int32(1), WORD - 1 - r)
            cand_score = key_to_score(cand)
            reach, = count_scores(lambda sc: sc >= cand_score)
            return jnp.where(reach >= float(TOPK_MAX), cand, ukey)
        ukey = lax.fori_loop(0, WORD, step, jnp.zeros((1, t), I32))
        thr_slow = jnp.where(ukey == 0, -jnp.inf, key_to_score(ukey))
        exceed, = count_scores(lambda sc: sc > thr_slow)
        return thr_slow, exceed

    thr, exceed = lax.cond(jnp.min(jnp.where(good, 1.0, 0.0)) > 0.5,
                           lambda: (thr_fast, exceed), search_by_value)
    need = float(TOPK_MAX) - exceed

    m_ref[...] = jnp.full(m_ref.shape, MASKED, F32)
    acc_ref[...] = jnp.zeros(acc_ref.shape, F32)
    row = lax.broadcasted_iota(I32, (2 * A_HEAD_DIM, t), 0)
    for hd in range(A_HEADS):
        pair = hd // 2
        blk = qt_ref[2 * A_HEAD_DIM * pair:2 * A_HEAD_DIM * (pair + 1), :]
        own = (row >= A_HEAD_DIM) if hd % 2 else (row < A_HEAD_DIM)
        qz_ref[hd] = jnp.where(own, blk, jnp.zeros_like(blk))
    tri = (lax.broadcasted_iota(I32, (t, t), 0) >= lax.broadcasted_iota(I32, (t, t), 1)).astype(BF16)

    def select(j, ties_before):
        kk = sc_ref[j]
        is_tie = kk == thr
        tie_rank = jnp.dot(tri, jnp.where(is_tie, 1.0, 0.0).astype(BF16),
                           preferred_element_type=F32) + ties_before
        tie_add = jnp.where(is_tie, jnp.where(tie_rank <= need, 0.0, MASKED), MASKED)
        mask_ref[...] = jnp.where(kk > thr, 0.0, tie_add)
        return tie_rank[t - 1:t, :]

    def scores(with_bias, slot, j, hd):
        pair = hd // 2
        s = jnp.dot(k_ref[j, :, 2 * A_HEAD_DIM * pair:2 * A_HEAD_DIM * (pair + 1)], qz_ref[hd],
                    preferred_element_type=F32)
        s = s + mask_ref[...]
        if with_bias:
            s = s + bias_ref[hd, jnp.minimum(i - j, 2)]
        s_ref[slot, hd] = s
        cm_ref[slot, hd] = jnp.max(s, axis=0, keepdims=True)

    def consume(slot, j, hd):
        m_old = m_ref[hd]
        m_new = jnp.maximum(m_old, cm_ref[slot, hd])
        m_ref[hd] = m_new
        alpha = jnp.exp2(m_old - m_new)
        p = jnp.exp2(s_ref[slot, hd] - m_new).astype(BF16)
        pv = jnp.dot(vt_ref[j, hd * PV_ROWS:(hd + 1) * PV_ROWS, :], p, preferred_element_type=F32)
        acc_ref[hd] = alpha * acc_ref[hd] + pv

    def step(with_bias, j, ties_before):
        def parity(slot):
            def run(ties):
                ties = select(j, ties)
                for hd in range(A_HEADS):
                    scores(with_bias, slot, j, hd)
                    consume(1 - slot, j - 1, hd)
                return ties
            return run
        return lax.cond(lax.rem(j, 2) == 1, parity(1), parity(0), ties_before)

    ties = select(0, jnp.zeros((1, t), F32))
    for hd in range(A_HEADS):
        scores(True, 0, 0, hd)
    first_near = jnp.maximum(i - 1, 1)
    ties = lax.fori_loop(1, first_near, functools.partial(step, False), ties)
    lax.fori_loop(first_near, nk, functools.partial(step, True), ties)
    for slot in range(2):
        @pl.when(lax.rem(i, 2) == slot)
        def _():
            for hd in range(A_HEADS):
                consume(slot, i, hd)

    outs = []
    for hd in range(A_HEADS):
        outs.append(acc_ref[hd, :A_HEAD_DIM, :] / acc_ref[hd, A_HEAD_DIM:A_HEAD_DIM + 1, :])
    o_ref[...] = jnp.concatenate(outs, axis=0).T.astype(BF16)


def _dsa_attention(qt, vt, iqt, iwt, k, ik, bias):
    b, nchunk, t, _ = k.shape
    s = nchunk * t
    return pl.pallas_call(
        _dsa_kernel,
        out_shape=jax.ShapeDtypeStruct((b, s, A_WIDTH), BF16),
        grid=(b, nchunk),
        in_specs=[
            pl.BlockSpec((None, None, A_WIDTH, t), lambda bi, i: (bi, i, 0, 0)),
            pl.BlockSpec((None, None, IDX_HEADS * IDX_DIM, t), lambda bi, i: (bi, i, 0, 0)),
            pl.BlockSpec((None, None, IDX_HEADS, t), lambda bi, i: (bi, i, 0, 0)),
            pl.BlockSpec((None, nchunk, t, A_WIDTH), lambda bi, i: (bi, 0, 0, 0)),
            pl.BlockSpec((None, nchunk, A_HEADS * PV_ROWS, t), lambda bi, i: (bi, 0, 0, 0)),
            pl.BlockSpec((None, nchunk, t, IDX_DIM), lambda bi, i: (bi, 0, 0, 0)),
            pl.BlockSpec(bias.shape, lambda bi, i: (0, 0, 0, 0)),
        ],
        out_specs=pl.BlockSpec((None, t, A_WIDTH), lambda bi, i: (bi, i, 0)),
        scratch_shapes=[
            pltpu.VMEM((nchunk, t, t), F32),
            pltpu.VMEM((WORD, nchunk, t // WORD, t), I32),
            pltpu.VMEM((nchunk, t // WORD, t), I32),
            pltpu.VMEM((A_HEADS, PV_ROWS, t), F32),
            pltpu.VMEM((A_HEADS, 1, t), F32),
            pltpu.VMEM((A_HEADS, 2 * A_HEAD_DIM, t), BF16),
            pltpu.VMEM((t, t), F32),
            pltpu.VMEM((2, A_HEADS, t, t), F32),
            pltpu.VMEM((2, A_HEADS, 1, t), F32),
        ],
        compiler_params=_params("parallel", "arbitrary"),
        name="dsa_attention",
    )(qt, iqt, iwt, k, vt, ik, bias)


def _halves(n):
    return (slice(0, n // 2), slice(n // 2, n))


def _xattn_block(parts, gpre_ref, gpost_ref, wq_ref, kt_ref, v_ref, wo_ref):
    heads = [slice(hd * X_HEAD_DIM, (hd + 1) * X_HEAD_DIM) for hd in range(X_HEADS)]
    hs = [_rms(x, gpre_ref[...]).astype(BF16) for x in parts]
    qs = [jnp.dot(h, wq_ref[...], preferred_element_type=F32) for h in hs]
    logits = [[jnp.dot(q[:, cols].astype(BF16), kt_ref[cols, :], preferred_element_type=F32)
               * (X_HEAD_DIM ** -0.5) for cols in heads] for q in qs]
    os = []
    for part_logits in logits:
        outs = []
        for cols, lg in zip(heads, part_logits):
            p = jnp.exp(lg - jnp.max(lg, axis=-1, keepdims=True))
            den = jnp.sum(p, axis=-1, keepdims=True)
            outs.append(jnp.dot(p.astype(BF16), v_ref[:, cols], preferred_element_type=F32) / den)
        os.append(jnp.concatenate(outs, axis=-1).astype(BF16))
    ys = [jnp.dot(o, wo_ref[...], preferred_element_type=F32) for o in os]
    return [x + _rms(y, gpost_ref[...]) for x, y in zip(parts, ys)]


def _xattn_operands(d, m, g_pre, g_post, wq, kt, v, wo):
    const = lambda shape: pl.BlockSpec(shape, lambda bi, i: (0, 0))
    specs = [const((1, d)), const((1, d)), const((d, X_WIDTH)),
             pl.BlockSpec((None, X_WIDTH, m), lambda bi, i: (bi, 0, 0)),
             pl.BlockSpec((None, m, X_WIDTH), lambda bi, i: (bi, 0, 0)),
             const((X_WIDTH, d))]
    return specs, (g_pre.reshape(1, d), g_post.reshape(1, d), wq.astype(BF16), kt, v, wo.astype(BF16))


def _mix_out_xattn_kernel(x_ref, a_ref, c_ref, wa_ref, wc_ref, g_ref, *rest):
    *xattn_refs, o_ref = rest
    halves = _halves(x_ref.shape[0])
    ys = [jnp.dot(a_ref[rows, :], wa_ref[...], preferred_element_type=F32)
          + jnp.dot(c_ref[rows, :], wc_ref[...], preferred_element_type=F32) for rows in halves]
    x1 = [x_ref[rows, :] + _rms(y, g_ref[...]) for rows, y in zip(halves, ys)]
    for rows, out in zip(halves, _xattn_block(x1, *xattn_refs)):
        o_ref[rows, :] = out


def _mix_out_xattn(x, a, c, w_out, g_post, xattn_args):
    b, s, d = x.shape
    t = ROW_TILE
    wb = w_out.astype(BF16)
    row = lambda width: pl.BlockSpec((None, t, width), lambda bi, i: (bi, i, 0))
    const = lambda shape: pl.BlockSpec(shape, lambda bi, i: (0, 0))
    xspecs, xarrays = _xattn_operands(d, xattn_args[3].shape[-1], *xattn_args)
    return pl.pallas_call(
        _mix_out_xattn_kernel,
        out_shape=jax.ShapeDtypeStruct((b, s, d), F32),
        grid=(b, s // t),
        in_specs=[row(d), row(A_WIDTH), row(B_WIDTH), const((A_WIDTH, d)), const((B_WIDTH, d)),
                  const((1, d))] + xspecs,
        out_specs=row(d),
        compiler_params=_params("parallel", "parallel"),
        name="mix_out_xattn",
    )(x, a, c, wb[:A_WIDTH], wb[A_WIDTH:], g_post.reshape(1, d), *xarrays)


def _pool_xattn_kernel(x_ref, xprev_ref, gpre_ref, gpost_ref, w_ref, scale_ref, *rest):
    *xattn_refs, o_ref = rest
    i = pl.program_id(1)
    x = x_ref[...]
    t, d = x.shape
    gpre = gpre_ref[...]
    h = _rms(x, gpre)
    hp = _rms(xprev_ref[...], gpre) * jnp.where(i > 0, 1.0, 0.0)
    ext = jnp.concatenate([hp, h], axis=0)
    group = d // len(POOL_WINDOWS)
    pos = i * t + lax.broadcasted_iota(I32, (t, group), 0)
    pooled = []
    for gi, win in enumerate(POOL_WINDOWS):
        sums = ext[:, gi * group:(gi + 1) * group]
        width = 1
        while width < win:
            sums = sums[width:] + sums[:-width]
            width *= 2
        wsum = sums[POOL_HALO - (win - 1):POOL_HALO - (win - 1) + t]
        cnt = jnp.minimum(pos + 1, win).astype(F32)
        pooled.append((wsum / cnt - h[:, gi * group:(gi + 1) * group]).astype(BF16))
    halves = _halves(t)
    x1 = []
    for rows in halves:
        y = jnp.concatenate([jnp.dot(pg[rows], w_ref[gi], preferred_element_type=F32)
                             for gi, pg in enumerate(pooled)], axis=-1) * scale_ref[...]
        x1.append(x[rows] + _rms(y, gpost_ref[...]))
    for rows, out in zip(halves, _xattn_block(x1, *xattn_refs)):
        o_ref[rows, :] = out


def _pool_xattn(x, g_pre, g_post, pool_w, pool_scale, xattn_args):
    b, s, d = x.shape
    t = ROW_TILE
    const = lambda shape: pl.BlockSpec(shape, lambda bi, i: (0,) * len(shape))
    prev = t // POOL_HALO
    xspecs, xarrays = _xattn_operands(d, xattn_args[3].shape[-1], *xattn_args)
    return pl.pallas_call(
        _pool_xattn_kernel,
        out_shape=jax.ShapeDtypeStruct((b, s, d), F32),
        grid=(b, s // t),
        in_specs=[
            pl.BlockSpec((None, t, d), lambda bi, i: (bi, i, 0)),
            pl.BlockSpec((None, POOL_HALO, d), lambda bi, i: (bi, jnp.maximum(i * prev - 1, 0), 0)),
            const((1, d)), const((1, d)), const(pool_w.shape), const((1, d)),
        ] + xspecs,
        out_specs=pl.BlockSpec((None, t, d), lambda bi, i: (bi, i, 0)),
        compiler_params=_params("parallel", "parallel"),
        name="pool_xattn",
    )(x, x, g_pre.reshape(1, d), g_post.reshape(1, d), pool_w.astype(BF16), pool_scale.reshape(1, d),
      *xarrays)


def _mem_kv_kernel(mem_ref, g_ref, wkt_ref, wv_ref, kt_ref, v_ref):
    mem_n = _rms(mem_ref[...], g_ref[...]).astype(BF16)
    kt = lax.dot_general(wkt_ref[...], mem_n, (((1,), (1,)), ((), ())), preferred_element_type=F32)
    kt_ref[...] = kt.astype(BF16)
    v_ref[...] = jnp.dot(mem_n, wv_ref[...], preferred_element_type=F32).astype(BF16)


def _mem_kv(mem, g, wk, wv):
    b, m, d = mem.shape
    const = lambda shape: pl.BlockSpec(shape, lambda bi: (0, 0))
    return pl.pallas_call(
        _mem_kv_kernel,
        out_shape=(jax.ShapeDtypeStruct((b, X_WIDTH, m), BF16), jax.ShapeDtypeStruct((b, m, X_WIDTH), BF16)),
        grid=(b,),
        in_specs=[pl.BlockSpec((None, m, d), lambda bi: (bi, 0, 0)), const((1, d)), const((X_WIDTH, d)),
                  const((d, X_WIDTH))],
        out_specs=(pl.BlockSpec((None, X_WIDTH, m), lambda bi: (bi, 0, 0)),
                   pl.BlockSpec((None, m, X_WIDTH), lambda bi: (bi, 0, 0))),
        compiler_params=_params("parallel"),
        name="mem_kv",
    )(mem, g.reshape(1, d), wk.astype(BF16).T, wv.astype(BF16))


def kernel(x, mem, ffn_w_gate, ffn_w_up, ffn_w_down, ffn_norm_pre, ffn_norm_post, mix_norm_pre,
           mix_norm_post, even_w_in, even_conv_w, even_w_out, rel_bias, pool_w, pool_scale,
           xattn_norm_pre, xattn_mem_norm, xattn_norm_post, xattn_wq, xattn_wk, xattn_wv, xattn_wo):
    b, s, d = x.shape
    depth = ffn_w_gate.shape[0]
    assert s % ATT_TILE == 0 and s // 4 >= TOPK_MAX and (b * s) % FFN_ROWS == 0 and s % ROW_TILE == 0
    bias = _bias_tiles(rel_bias, s)

    ffn_weights = (ffn_w_gate.astype(BF16), ffn_w_up.astype(BF16), ffn_w_down.astype(BF16))

    def ffn(xx, layer, j):
        out = _ffn(xx.reshape(b * s, d), layer, j, ffn_norm_pre, ffn_norm_post, *ffn_weights)
        return out.reshape(b, s, d)

    for layer in range(depth):
        x = ffn(x, layer, 0)
        kt, v = _mem_kv(mem, xattn_mem_norm[layer], xattn_wk[layer], xattn_wv[layer])
        xattn_args = (xattn_norm_pre[layer], xattn_norm_post[layer], xattn_wq[layer], kt, v,
                      xattn_wo[layer])
        if layer % 2 == 0:
            e = layer // 2
            qt, vt, iqt, iwt, k, ik, c = _proj_even(x, mix_norm_pre[layer], even_w_in[e], even_conv_w[e])
            a = _dsa_attention(qt, vt, iqt, iwt, k, ik, bias)
            x = _mix_out_xattn(x, a, c, even_w_out[e], mix_norm_post[layer], xattn_args)
        else:
            o = layer // 2
            x = _pool_xattn(x, mix_norm_pre[layer], mix_norm_post[layer], pool_w[o], pool_scale[o],
                            xattn_args)
        x = ffn(x, layer, 1)
    return x
```

```python
import functools
import math

import jax
import jax.numpy as jnp
import numpy as np
from jax import lax
from jax.experimental import pallas as pl
from jax.experimental.pallas import tpu as pltpu

F32 = jnp.float32
BF16 = jnp.bfloat16
I32 = jnp.int32

RMS_EPS = 1e-6

A_HEADS = 8
A_HEAD_DIM = 64
A_WIDTH = A_HEADS * A_HEAD_DIM
IDX_HEADS = 8
IDX_DIM = 64
TOPK_MAX = 256
REL_BUCKETS = 32
REL_MAX_EXACT = 16
REL_MAX_DIST = 128
B_WIDTH = 512
CONV_K = 3
POOL_WINDOWS = (2, 4, 8, 16)
POOL_HALO = 16
X_HEADS = 4
X_HEAD_DIM = 128
X_WIDTH = X_HEADS * X_HEAD_DIM

LANES = 128
SUBLANES = 8
MXU_DIM = 256
VMEM_BYTES_V7X = 64 * 1024 * 1024
VMEM_LIMIT = VMEM_BYTES_V7X - 8 * 1024 * 1024

ATT_TILE = MXU_DIM
FFN_ROWS = 1024
FFN_COLS = MXU_DIM
ROW_TILE = 512

BF16_SUBLANES = 16
PV_ROWS = A_HEAD_DIM + BF16_SUBLANES
LOG2E = math.log2(math.e)
MASKED = -1e30
KEY_MIN = -(2 ** 31)
WORD = 32


def _t5_bucket_table(n):
    d = np.arange(n)
    nf = np.maximum(d, 1).astype(np.float32)
    ratio = np.log(nf / np.float32(REL_MAX_EXACT)) / np.float32(math.log(REL_MAX_DIST / REL_MAX_EXACT))
    large = REL_MAX_EXACT + (ratio * np.float32(REL_BUCKETS - REL_MAX_EXACT)).astype(np.int32)
    large = np.minimum(large, REL_BUCKETS - 1)
    return np.where(d < REL_MAX_EXACT, d, large)


def _rms(x, g):
    ms = jnp.mean(x * x, axis=-1, keepdims=True)
    return x * lax.rsqrt(ms + RMS_EPS) * g


def _params(*semantics):
    return pltpu.CompilerParams(dimension_semantics=semantics, vmem_limit_bytes=VMEM_LIMIT)


def _ffn_kernel(x_ref, gpre_ref, gpost_ref, wg_ref, wu_ref, wd_ref, o_ref, h_ref, a_ref):
    for rows in _halves(x_ref.shape[0]):
        x = x_ref[rows, :]
        h_ref[rows, :] = _rms(x, gpre_ref[...]).astype(BF16)
        for f in range(a_ref.shape[1] // FFN_COLS):
            cols = slice(f * FFN_COLS, (f + 1) * FFN_COLS)
            g = jnp.dot(h_ref[rows, :], wg_ref[:, cols], preferred_element_type=F32)
            u = jnp.dot(h_ref[rows, :], wu_ref[:, cols], preferred_element_type=F32)
            a_ref[rows, cols] = (g * jax.nn.sigmoid(g) * u).astype(BF16)
        y = jnp.dot(a_ref[rows, :], wd_ref[...], preferred_element_type=F32)
        o_ref[rows, :] = x + 0.5 * _rms(y, gpost_ref[...])


def _ffn(x, layer, j, g_pre, g_post, w_gate, w_up, w_down):
    n, d = x.shape
    dff = w_gate.shape[-1]
    assert dff % FFN_COLS == 0
    pick = lambda rows, cols: pl.BlockSpec((None, None, rows, cols), lambda i: (layer, j, 0, 0))
    depth = g_pre.shape[0]
    return pl.pallas_call(
        _ffn_kernel,
        out_shape=jax.ShapeDtypeStruct((n, d), F32),
        grid=(n // FFN_ROWS,),
        in_specs=[pl.BlockSpec((FFN_ROWS, d), lambda i: (i, 0)), pick(1, d), pick(1, d),
                  pick(d, dff), pick(d, dff), pick(dff, d)],
        out_specs=pl.BlockSpec((FFN_ROWS, d), lambda i: (i, 0)),
        scratch_shapes=[pltpu.VMEM((FFN_ROWS, d), BF16), pltpu.VMEM((FFN_ROWS, dff), BF16)],
        compiler_params=_params("parallel"),
        name="ffn",
    )(x, g_pre.reshape(depth, 2, 1, d), g_post.reshape(depth, 2, 1, d), w_gate, w_up, w_down)


def _proj_kernel(x_ref, g_ref, wk_ref, wik_ref, wugg_ref, wqt_ref, wvt_ref, wiqt_ref, wiwt_ref,
                 convw_ref, qt_ref, vt_ref, iqt_ref, iwt_ref, k_ref, ik_ref, c_ref, carry_ref):
    @pl.when(pl.program_id(1) == 0)
    def _():
        carry_ref[...] = jnp.zeros_like(carry_ref)

    h = _rms(x_ref[...], g_ref[...]).astype(BF16)
    ugg = jnp.dot(h, wugg_ref[...], preferred_element_type=F32)
    u = ugg[:, :B_WIDTH]
    gb = ugg[:, B_WIDTH:2 * B_WIDTH]
    gc = ugg[:, 2 * B_WIDTH:]
    gu = gc * u
    t = gu.shape[0]
    ext = jnp.concatenate([carry_ref[...], gu], axis=0)
    w = convw_ref[...]
    y = w[2:3] * gu
    for j in range(CONV_K - 1):
        lo = SUBLANES - (CONV_K - 1) + j
        y = y + w[j:j + 1] * ext[lo:lo + t]
    c_ref[...] = (gb * y).astype(BF16)
    carry_ref[...] = gu[t - SUBLANES:]

    nt = (((1,), (1,)), ((), ()))
    qt = lax.dot_general(wqt_ref[...], h, nt, preferred_element_type=F32)
    qt_ref[...] = (qt * (A_HEAD_DIM ** -0.5 * LOG2E)).astype(BF16)
    vt = lax.dot_general(wvt_ref[...], h, nt, preferred_element_type=F32).astype(BF16)
    ones = jnp.ones((PV_ROWS - A_HEAD_DIM, vt.shape[1]), BF16)
    for hd in range(A_HEADS):
        vt_ref[hd * PV_ROWS:hd * PV_ROWS + A_HEAD_DIM, :] = vt[hd * A_HEAD_DIM:(hd + 1) * A_HEAD_DIM]
        vt_ref[hd * PV_ROWS + A_HEAD_DIM:(hd + 1) * PV_ROWS, :] = ones
    iqt_ref[...] = lax.dot_general(wiqt_ref[...], h, nt, preferred_element_type=F32).astype(BF16)
    iwt = lax.dot_general(wiwt_ref[...], h, nt, preferred_element_type=F32)
    iwt_ref[...] = iwt[:IDX_HEADS] * (IDX_HEADS ** -0.5 * IDX_DIM ** -0.5)
    k_ref[...] = jnp.dot(h, wk_ref[...], preferred_element_type=F32).astype(BF16)
    ik_ref[...] = jnp.dot(h, wik_ref[...], preferred_element_type=F32).astype(BF16)


def _proj_even(x, g, w_in, conv_w):
    b, s, d = x.shape
    t = ATT_TILE
    nchunk = s // t
    q_end, k_end, v_end = A_WIDTH, 2 * A_WIDTH, 3 * A_WIDTH
    iq_end = v_end + IDX_HEADS * IDX_DIM
    ik_end = iq_end + IDX_DIM
    iw_end = ik_end + IDX_HEADS
    wb = w_in.astype(BF16)
    wqt = wb[:, :q_end].T
    wk = wb[:, q_end:k_end]
    wvt = wb[:, k_end:v_end].T
    wiqt = wb[:, v_end:iq_end].T
    wik = wb[:, iq_end:ik_end]
    wiwt = jnp.pad(wb[:, ik_end:iw_end].T, ((0, 2 * SUBLANES - IDX_HEADS), (0, 0)))
    wugg = wb[:, iw_end:]
    const = lambda shape: pl.BlockSpec(shape, lambda bi, i: (0,) * len(shape))
    return pl.pallas_call(
        _proj_kernel,
        out_shape=(
            jax.ShapeDtypeStruct((b, nchunk, A_WIDTH, t), BF16),
            jax.ShapeDtypeStruct((b, nchunk, A_HEADS * PV_ROWS, t), BF16),
            jax.ShapeDtypeStruct((b, nchunk, IDX_HEADS * IDX_DIM, t), BF16),
            jax.ShapeDtypeStruct((b, nchunk, IDX_HEADS, t), F32),
            jax.ShapeDtypeStruct((b, nchunk, t, A_WIDTH), BF16),
            jax.ShapeDtypeStruct((b, nchunk, t, IDX_DIM), BF16),
            jax.ShapeDtypeStruct((b, s, B_WIDTH), BF16),
        ),
        grid=(b, nchunk),
        in_specs=[
            pl.BlockSpec((None, t, d), lambda bi, i: (bi, i, 0)),
            const((1, d)),
            const(wk.shape), const(wik.shape), const(wugg.shape), const(wqt.shape),
            const(wvt.shape), const(wiqt.shape), const(wiwt.shape), const(conv_w.shape),
        ],
        out_specs=(
            pl.BlockSpec((None, None, A_WIDTH, t), lambda bi, i: (bi, i, 0, 0)),
            pl.BlockSpec((None, None, A_HEADS * PV_ROWS, t), lambda bi, i: (bi, i, 0, 0)),
            pl.BlockSpec((None, None, IDX_HEADS * IDX_DIM, t), lambda bi, i: (bi, i, 0, 0)),
            pl.BlockSpec((None, None, IDX_HEADS, t), lambda bi, i: (bi, i, 0, 0)),
            pl.BlockSpec((None, None, t, A_WIDTH), lambda bi, i: (bi, i, 0, 0)),
            pl.BlockSpec((None, None, t, IDX_DIM), lambda bi, i: (bi, i, 0, 0)),
            pl.BlockSpec((None, t, B_WIDTH), lambda bi, i: (bi, i, 0)),
        ),
        scratch_shapes=[pltpu.VMEM((SUBLANES, B_WIDTH), F32)],
        compiler_params=_params("arbitrary", "arbitrary"),
        name="proj_even",
    )(x, g.reshape(1, d), wk, wik, wugg, wqt, wvt, wiqt, wiwt, conv_w)


def _bias_kernel(rb_ref, o_ref, *, thresholds, far_bucket):
    hd = pl.program_id(0)
    t = o_ref.shape[-1]
    r = lax.broadcasted_iota(I32, (t, t), 1)
    c = lax.broadcasted_iota(I32, (t, t), 0)
    for m in range(2):
        dist = t * m + r - c
        val = jnp.full((t, t), rb_ref[0, hd], F32)
        for bucket in range(1, REL_BUCKETS):
            val = jnp.where(dist >= thresholds[bucket], rb_ref[bucket, hd], val)
        val = (val - rb_ref[far_bucket, hd]) * LOG2E
        if m == 0:
            val = jnp.where(dist < 0, MASKED, val)
        o_ref[m] = val
    o_ref[2] = jnp.zeros((t, t), F32)


def _bias_tiles(rel_bias, seq):
    t = ATT_TILE
    table = _t5_bucket_table(seq)
    assert np.all(np.diff(table) >= 0) and table[0] == 0 and table.max() == REL_BUCKETS - 1
    thresholds = tuple(int(np.argmax(table >= bkt)) for bkt in range(REL_BUCKETS))
    far_bucket = int(table[2 * t - 1])
    assert np.all(table[2 * t - 1:] == far_bucket)
    return pl.pallas_call(
        functools.partial(_bias_kernel, thresholds=thresholds, far_bucket=far_bucket),
        out_shape=jax.ShapeDtypeStruct((A_HEADS, 3, t, t), F32),
        grid=(A_HEADS,),
        in_specs=[pl.BlockSpec(memory_space=pltpu.SMEM)],
        out_specs=pl.BlockSpec((None, 3, t, t), lambda h: (h, 0, 0, 0)),
        compiler_params=_params("parallel"),
        name="rel_bias_tiles",
    )(rel_bias)


def _bit_transpose32(words):
    a = list(words)
    shift, mask = WORD // 2, 0x0000FFFF
    while shift:
        k = 0
        while k < WORD:
            swap = (a[k] ^ lax.shift_right_logical(a[k + shift], jnp.int32(shift))) & mask
            a[k] = a[k] ^ swap
            a[k + shift] = a[k + shift] ^ (swap << shift)
            k = (k + shift + 1) & ~shift
        shift //= 2
        mask ^= (mask << shift) & 0xFFFFFFFF
    return a


def _dsa_kernel(qt_ref, iqt_ref, iwt_ref, k_ref, vt_ref, ik_ref, bias_ref, o_ref,
                sc_ref, plane_ref, eq_ref, acc_ref, m_ref, qz_ref, mask_ref, s_ref, cm_ref):
    i = pl.program_id(1)
    t = ATT_TILE
    nk = i + 1
    iw = iwt_ref[...]

    def index_scores(j):
        ikc = ik_ref[j]
        sc = jnp.zeros((t, t), F32)
        for hd in range(IDX_HEADS):
            d = jnp.dot(ikc, iqt_ref[hd * IDX_DIM:(hd + 1) * IDX_DIM, :],
                        preferred_element_type=F32)
            sc = sc + jnp.maximum(d, 0.0) * iw[hd:hd + 1, :]
        return sc

    def store_scores(j, sc):
        sc_ref[j] = sc
        bits = pltpu.bitcast(sc + 0.0, I32)
        ukeys = bits ^ ((bits >> 31) | KEY_MIN)
        for col in range(t // LANES):
            lanes = slice(col * LANES, (col + 1) * LANES)
            planes = _bit_transpose32([ukeys[v * SUBLANES:(v + 1) * SUBLANES, lanes] for v in range(WORD)])
            for r in range(WORD):
                plane_ref[r, j, :, lanes] = planes[r]

    @pl.when(i == 0)
    def _():
        plane_ref[...] = jnp.zeros(plane_ref.shape, I32)

    def phase_a(pair, carry):
        store_scores(2 * pair, index_scores(2 * pair))
        store_scores(2 * pair + 1, index_scores(2 * pair + 1))
        return carry

    lax.fori_loop(0, lax.shift_right_logical(i, 1), phase_a, 0)

    @pl.when((i & 1) == 1)
    def _():
        store_scores(i - 1, index_scores(i - 1))

    qpos = lax.broadcasted_iota(I32, (t, t), 1)
    kpos = lax.broadcasted_iota(I32, (t, t), 0)
    store_scores(i, jnp.where(kpos <= qpos, index_scores(i), -jnp.inf))

    nchunk = eq_ref.shape[0]
    live = lax.broadcasted_iota(I32, eq_ref.shape, 0) < nk
    eq_ref[...] = jnp.where(live, jnp.int32(-1), jnp.int32(0))

    def search(span):
        def search_bit(r, carry):
            thr_u, above = carry
            cnt = jnp.zeros((SUBLANES, t), I32)
            for c in range(span):
                cnt = cnt + lax.population_count(eq_ref[c] & plane_ref[r, c])
            cnt = jnp.sum(cnt.astype(F32), axis=0, keepdims=True)
            take = above + cnt >= float(TOPK_MAX)
            for c in range(span):
                eq = eq_ref[c]
                with_bit = eq & plane_ref[r, c]
                eq_ref[c] = jnp.where(take, with_bit, eq ^ with_bit)
            bit = jnp.left_shift(jnp.int32(1), WORD - 1 - r)
            return jnp.where(take, thr_u | bit, thr_u), jnp.where(take, above, above + cnt)

        return lambda: lax.fori_loop(0, WORD, search_bit,
                                     (jnp.zeros((1, t), I32), jnp.zeros((1, t), F32)))

    quarter = nchunk // 4
    thr_u, _ = lax.cond(
        nk <= 2 * quarter,
        lambda: lax.cond(nk <= quarter, search(quarter), search(2 * quarter)),
        lambda: lax.cond(nk <= 3 * quarter, search(3 * quarter), search(nchunk)))

    def key_to_score(ukey):
        return pltpu.bitcast(ukey ^ (~(ukey >> 31) | KEY_MIN), F32)

    def count_scores(*preds):
        def body(j, cnts):
            sc = sc_ref[j]
            return tuple(cnt + jnp.sum(jnp.where(pred(sc), 1.0, 0.0).reshape(t // SUBLANES, SUBLANES, t), axis=0)
                         for cnt, pred in zip(cnts, preds))
        cnts = lax.fori_loop(0, nk, body, tuple(jnp.zeros((SUBLANES, t), F32) for _ in preds))
        return [jnp.sum(cnt, axis=0, keepdims=True) for cnt in cnts]

    thr_fast = key_to_score(thr_u)
    reach, exceed = count_scores(lambda sc: sc >= thr_fast, lambda sc: sc > thr_fast)
    good = jnp.logical_and(exceed < float(TOPK_MAX), reach >= float(TOPK_MAX))

    def search_by_value():
        def step(r, ukey):
            cand = ukey | jnp.left_shift(jnp.int32(1), WORD - 1 - r)
            cand_score = key_to_score(cand)
            reach, = count_scores(lambda sc: sc >= cand_score)
            return jnp.where(reach >= float(TOPK_MAX), cand, ukey)
        ukey = lax.fori_loop(0, WORD, step, jnp.zeros((1, t), I32))
        thr_slow = jnp.where(ukey == 0, -jnp.inf, key_to_score(ukey))
        exceed, = count_scores(lambda sc: sc > thr_slow)
        return thr_slow, exceed

    thr, exceed = lax.cond(jnp.min(jnp.where(good, 1.0, 0.0)) > 0.5,
                           lambda: (thr_fast, exceed), search_by_value)
    need = float(TOPK_MAX) - exceed

    m_ref[...] = jnp.full(m_ref.shape, MASKED, F32)
    acc_ref[...] = jnp.zeros(acc_ref.shape, F32)
    row = lax.broadcasted_iota(I32, (2 * A_HEAD_DIM, t), 0)
    for hd in range(A_HEADS):
        pair = hd // 2
        blk = qt_ref[2 * A_HEAD_DIM * pair:2 * A_HEAD_DIM * (pair + 1), :]
        own = (row >= A_HEAD_DIM) if hd % 2 else (row < A_HEAD_DIM)
        qz_ref[hd] = jnp.where(own, blk, jnp.zeros_like(blk))
    tri = (lax.broadcasted_iota(I32, (t, t), 0) >= lax.broadcasted_iota(I32, (t, t), 1)).astype(BF16)

    def select(j, ties_before):
        kk = sc_ref[j]
        is_tie = kk == thr
        tie_rank = jnp.dot(tri, jnp.where(is_tie, 1.0, 0.0).astype(BF16),
                           preferred_element_type=F32) + ties_before
        tie_add = jnp.where(is_tie, jnp.where(tie_rank <= need, 0.0, MASKED), MASKED)
        mask_ref[...] = jnp.where(kk > thr, 0.0, tie_add).astype(BF16)
        return tie_rank[t - 1:t, :]

    slabs = (t // BF16_SUBLANES, BF16_SUBLANES, t)

    def scores(with_bias, slot, j, hd):
        pair = hd // 2
        s = jnp.dot(k_ref[j, :, 2 * A_HEAD_DIM * pair:2 * A_HEAD_DIM * (pair + 1)], qz_ref[hd],
                    preferred_element_type=F32)
        if with_bias:
            s = s + bias_ref[hd, jnp.minimum(i - j, 2)]
        s = s.astype(BF16) + mask_ref[...]
        s_ref[slot, hd] = s
        parts = [s.reshape(slabs)[r] for r in range(slabs[0])]
        while len(parts) > 1:
            parts = [jnp.maximum(a, b) for a, b in zip(parts[::2], parts[1::2])]
        cm_ref[slot, hd] = jnp.max(parts[0].astype(F32), axis=0, keepdims=True)

    def consume(slot, j, hd):
        m_old = m_ref[hd]
        m_new = jnp.maximum(m_old, cm_ref[slot, hd])
        m_ref[hd] = m_new
        alpha = jnp.exp2(m_old - m_new)
        m_tile = jnp.broadcast_to(m_new, (BF16_SUBLANES, t)).astype(BF16)
        p = jnp.exp2(s_ref[slot, hd].reshape(slabs) - m_tile[None]).reshape(t, t)
        pv = jnp.dot(vt_ref[j, hd * PV_ROWS:(hd + 1) * PV_ROWS, :], p, preferred_element_type=F32)
        acc_ref[hd] = alpha * acc_ref[hd] + pv

    def half_step(with_bias, slot, j, ties):
        ties = select(j, ties)
        for hd in range(A_HEADS):
            scores(with_bias, slot, j, hd)
            consume(1 - slot, j - 1, hd)
        return ties

    def step(with_bias, j, ties_before):
        return lax.cond(lax.rem(j, 2) == 1, functools.partial(half_step, with_bias, 1, j),
                        functools.partial(half_step, with_bias, 0, j), ties_before)

    def far_pair(pair, ties):
        ties = half_step(False, 1, 2 * pair + 1, ties)
        return half_step(False, 0, 2 * pair + 2, ties)

    ties = select(0, jnp.zeros((1, t), F32))
    for hd in range(A_HEADS):
        scores(True, 0, 0, hd)
    first_near = jnp.maximum(i - 1, 1)
    far_pairs = lax.shift_right_logical(first_near - 1, 1)
    ties = lax.fori_loop(0, far_pairs, far_pair, ties)
    ties = lax.fori_loop(2 * far_pairs + 1, first_near, functools.partial(step, False), ties)
    lax.fori_loop(first_near, nk, functools.partial(step, True), ties)
    for slot in range(2):
        @pl.when(lax.rem(i, 2) == slot)
        def _():
            for hd in range(A_HEADS):
                consume(slot, i, hd)

    outs = []
    for hd in range(A_HEADS):
        outs.append(acc_ref[hd, :A_HEAD_DIM, :] / acc_ref[hd, A_HEAD_DIM:A_HEAD_DIM + 1, :])
    o_ref[...] = jnp.concatenate(outs, axis=0).T.astype(BF16)


def _dsa_attention(qt, vt, iqt, iwt, k, ik, bias):
    b, nchunk, t, _ = k.shape
    s = nchunk * t
    return pl.pallas_call(
        _dsa_kernel,
        out_shape=jax.ShapeDtypeStruct((b, s, A_WIDTH), BF16),
        grid=(b, nchunk),
        in_specs=[
            pl.BlockSpec((None, None, A_WIDTH, t), lambda bi, i: (bi, i, 0, 0)),
            pl.BlockSpec((None, None, IDX_HEADS * IDX_DIM, t), lambda bi, i: (bi, i, 0, 0)),
            pl.BlockSpec((None, None, IDX_HEADS, t), lambda bi, i: (bi, i, 0, 0)),
            pl.BlockSpec((None, nchunk, t, A_WIDTH), lambda bi, i: (bi, 0, 0, 0)),
            pl.BlockSpec((None, nchunk, A_HEADS * PV_ROWS, t), lambda bi, i: (bi, 0, 0, 0)),
            pl.BlockSpec((None, nchunk, t, IDX_DIM), lambda bi, i: (bi, 0, 0, 0)),
            pl.BlockSpec(bias.shape, lambda bi, i: (0, 0, 0, 0)),
        ],
        out_specs=pl.BlockSpec((None, t, A_WIDTH), lambda bi, i: (bi, i, 0)),
        scratch_shapes=[
            pltpu.VMEM((nchunk, t, t), F32),
            pltpu.VMEM((WORD, nchunk, t // WORD, t), I32),
            pltpu.VMEM((nchunk, t // WORD, t), I32),
            pltpu.VMEM((A_HEADS, PV_ROWS, t), F32),
            pltpu.VMEM((A_HEADS, 1, t), F32),
            pltpu.VMEM((A_HEADS, 2 * A_HEAD_DIM, t), BF16),
            pltpu.VMEM((t, t), BF16),
            pltpu.VMEM((2, A_HEADS, t, t), BF16),
            pltpu.VMEM((2, A_HEADS, 1, t), F32),
        ],
        compiler_params=_params("parallel", "arbitrary"),
        name="dsa_attention",
    )(qt, iqt, iwt, k, vt, ik, bias)


def _halves(n):
    return (slice(0, n // 2), slice(n // 2, n))


def _xattn_block(parts, gpre_ref, gpost_ref, wq_ref, kt_ref, v_ref, wo_ref):
    heads = [slice(hd * X_HEAD_DIM, (hd + 1) * X_HEAD_DIM) for hd in range(X_HEADS)]
    hs = [_rms(x, gpre_ref[...]).astype(BF16) for x in parts]
    qs = [jnp.dot(h, wq_ref[...], preferred_element_type=F32) for h in hs]
    logits = [[jnp.dot(q[:, cols].astype(BF16), kt_ref[cols, :], preferred_element_type=F32)
               * (X_HEAD_DIM ** -0.5) for cols in heads] for q in qs]
    os = []
    for part_logits in logits:
        outs = []
        for cols, lg in zip(heads, part_logits):
            p = jnp.exp(lg - jnp.max(lg, axis=-1, keepdims=True))
            den = jnp.sum(p, axis=-1, keepdims=True)
            outs.append(jnp.dot(p.astype(BF16), v_ref[:, cols], preferred_element_type=F32) / den)
        os.append(jnp.concatenate(outs, axis=-1).astype(BF16))
    ys = [jnp.dot(o, wo_ref[...], preferred_element_type=F32) for o in os]
    return [x + _rms(y, gpost_ref[...]) for x, y in zip(parts, ys)]


def _xattn_operands(d, m, g_pre, g_post, wq, kt, v, wo):
    const = lambda shape: pl.BlockSpec(shape, lambda bi, i: (0, 0))
    specs = [const((1, d)), const((1, d)), const((d, X_WIDTH)),
             pl.BlockSpec((None, X_WIDTH, m), lambda bi, i: (bi, 0, 0)),
             pl.BlockSpec((None, m, X_WIDTH), lambda bi, i: (bi, 0, 0)),
             const((X_WIDTH, d))]
    return specs, (g_pre.reshape(1, d), g_post.reshape(1, d), wq.astype(BF16), kt, v, wo.astype(BF16))


def _mix_out_xattn_kernel(x_ref, a_ref, c_ref, wa_ref, wc_ref, g_ref, *rest):
    *xattn_refs, o_ref = rest
    halves = _halves(x_ref.shape[0])
    ys = [jnp.dot(a_ref[rows, :], wa_ref[...], preferred_element_type=F32)
          + jnp.dot(c_ref[rows, :], wc_ref[...], preferred_element_type=F32) for rows in halves]
    x1 = [x_ref[rows, :] + _rms(y, g_ref[...]) for rows, y in zip(halves, ys)]
    for rows, out in zip(halves, _xattn_block(x1, *xattn_refs)):
        o_ref[rows, :] = out


def _mix_out_xattn(x, a, c, w_out, g_post, xattn_args):
    b, s, d = x.shape
    t = ROW_TILE
    wb = w_out.astype(BF16)
    row = lambda width: pl.BlockSpec((None, t, width), lambda bi, i: (bi, i, 0))
    const = lambda shape: pl.BlockSpec(shape, lambda bi, i: (0, 0))
    xspecs, xarrays = _xattn_operands(d, xattn_args[3].shape[-1], *xattn_args)
    return pl.pallas_call(
        _mix_out_xattn_kernel,
        out_shape=jax.ShapeDtypeStruct((b, s, d), F32),
        grid=(b, s // t),
        in_specs=[row(d), row(A_WIDTH), row(B_WIDTH), const((A_WIDTH, d)), const((B_WIDTH, d)),
                  const((1, d))] + xspecs,
        out_specs=row(d),
        compiler_params=_params("parallel", "parallel"),
        name="mix_out_xattn",
    )(x, a, c, wb[:A_WIDTH], wb[A_WIDTH:], g_post.reshape(1, d), *xarrays)


def _pool_xattn_kernel(x_ref, xprev_ref, gpre_ref, gpost_ref, w_ref, scale_ref, *rest):
    *xattn_refs, o_ref = rest
    i = pl.program_id(1)
    x = x_ref[...]
    t, d = x.shape
    gpre = gpre_ref[...]
    h = _rms(x, gpre)
    hp = _rms(xprev_ref[...], gpre) * jnp.where(i > 0, 1.0, 0.0)
    ext = jnp.concatenate([hp, h], axis=0)
    group = d // len(POOL_WINDOWS)
    pos = i * t + lax.broadcasted_iota(I32, (t, group), 0)
    pooled = []
    for gi, win in enumerate(POOL_WINDOWS):
        sums = ext[:, gi * group:(gi + 1) * group]
        width = 1
        while width < win:
            sums = sums[width:] + sums[:-width]
            width *= 2
        wsum = sums[POOL_HALO - (win - 1):POOL_HALO - (win - 1) + t]
        cnt = jnp.minimum(pos + 1, win).astype(F32)
        pooled.append((wsum / cnt - h[:, gi * group:(gi + 1) * group]).astype(BF16))
    halves = _halves(t)
    x1 = []
    for rows in halves:
        y = jnp.concatenate([jnp.dot(pg[rows], w_ref[gi], preferred_element_type=F32)
                             for gi, pg in enumerate(pooled)], axis=-1) * scale_ref[...]
        x1.append(x[rows] + _rms(y, gpost_ref[...]))
    for rows, out in zip(halves, _xattn_block(x1, *xattn_refs)):
        o_ref[rows, :] = out


def _pool_xattn(x, g_pre, g_post, pool_w, pool_scale, xattn_args):
    b, s, d = x.shape
    t = ROW_TILE
    const = lambda shape: pl.BlockSpec(shape, lambda bi, i: (0,) * len(shape))
    prev = t // POOL_HALO
    xspecs, xarrays = _xattn_operands(d, xattn_args[3].shape[-1], *xattn_args)
    return pl.pallas_call(
        _pool_xattn_kernel,
        out_shape=jax.ShapeDtypeStruct((b, s, d), F32),
        grid=(b, s // t),
        in_specs=[
            pl.BlockSpec((None, t, d), lambda bi, i: (bi, i, 0)),
            pl.BlockSpec((None, POOL_HALO, d), lambda bi, i: (bi, jnp.maximum(i * prev - 1, 0), 0)),
            const((1, d)), const((1, d)), const(pool_w.shape), const((1, d)),
        ] + xspecs,
        out_specs=pl.BlockSpec((None, t, d), lambda bi, i: (bi, i, 0)),
        compiler_params=_params("parallel", "parallel"),
        name="pool_xattn",
    )(x, x, g_pre.reshape(1, d), g_post.reshape(1, d), pool_w.astype(BF16), pool_scale.reshape(1, d),
      *xarrays)


def _mem_kv_kernel(mem_ref, g_ref, wkt_ref, wv_ref, kt_ref, v_ref):
    mem_n = _rms(mem_ref[...], g_ref[...]).astype(BF16)
    kt = lax.dot_general(wkt_ref[...], mem_n, (((1,), (1,)), ((), ())), preferred_element_type=F32)
    kt_ref[...] = kt.astype(BF16)
    v_ref[...] = jnp.dot(mem_n, wv_ref[...], preferred_element_type=F32).astype(BF16)


def _mem_kv(mem, g, wk, wv):
    b, m, d = mem.shape
    const = lambda shape: pl.BlockSpec(shape, lambda bi: (0, 0))
    return pl.pallas_call(
        _mem_kv_kernel,
        out_shape=(jax.ShapeDtypeStruct((b, X_WIDTH, m), BF16), jax.ShapeDtypeStruct((b, m, X_WIDTH), BF16)),
        grid=(b,),
        in_specs=[pl.BlockSpec((None, m, d), lambda bi: (bi, 0, 0)), const((1, d)), const((X_WIDTH, d)),
                  const((d, X_WIDTH))],
        out_specs=(pl.BlockSpec((None, X_WIDTH, m), lambda bi: (bi, 0, 0)),
                   pl.BlockSpec((None, m, X_WIDTH), lambda bi: (bi, 0, 0))),
        compiler_params=_params("parallel"),
        name="mem_kv",
    )(mem, g.reshape(1, d), wk.astype(BF16).T, wv.astype(BF16))


def kernel(x, mem, ffn_w_gate, ffn_w_up, ffn_w_down, ffn_norm_pre, ffn_norm_post, mix_norm_pre,
           mix_norm_post, even_w_in, even_conv_w, even_w_out, rel_bias, pool_w, pool_scale,
           xattn_norm_pre, xattn_mem_norm, xattn_norm_post, xattn_wq, xattn_wk, xattn_wv, xattn_wo):
    b, s, d = x.shape
    depth = ffn_w_gate.shape[0]
    assert s % ATT_TILE == 0 and s // 4 >= TOPK_MAX and (b * s) % FFN_ROWS == 0 and s % ROW_TILE == 0
    bias = _bias_tiles(rel_bias, s)

    ffn_weights = (ffn_w_gate.astype(BF16), ffn_w_up.astype(BF16), ffn_w_down.astype(BF16))

    def ffn(xx, layer, j):
        out = _ffn(xx.reshape(b * s, d), layer, j, ffn_norm_pre, ffn_norm_post, *ffn_weights)
        return out.reshape(b, s, d)

    for layer in range(depth):
        x = ffn(x, layer, 0)
        kt, v = _mem_kv(mem, xattn_mem_norm[layer], xattn_wk[layer], xattn_wv[layer])
        xattn_args = (xattn_norm_pre[layer], xattn_norm_post[layer], xattn_wq[layer], kt, v,
                      xattn_wo[layer])
        if layer % 2 == 0:
            e = layer // 2
            qt, vt, iqt, iwt, k, ik, c = _proj_even(x, mix_norm_pre[layer], even_w_in[e], even_conv_w[e])
            a = _dsa_attention(qt, vt, iqt, iwt, k, ik, bias)
            x = _mix_out_xattn(x, a, c, even_w_out[e], mix_norm_post[layer], xattn_args)
        else:
            o = layer // 2
            x = _pool_xattn(x, mix_norm_pre[layer], mix_norm_post[layer], pool_w[o], pool_scale[o],
                            xattn_args)
        x = ffn(x, layer, 1)
    return x
```

```python
import functools
import math

import jax
import jax.numpy as jnp
import numpy as np
from jax import lax
from jax.experimental import pallas as pl
from jax.experimental.pallas import tpu as pltpu

F32 = jnp.float32
BF16 = jnp.bfloat16
I32 = jnp.int32

RMS_EPS = 1e-6

A_HEADS = 8
A_HEAD_DIM = 64
A_WIDTH = A_HEADS * A_HEAD_DIM
IDX_HEADS = 8
IDX_DIM = 64
TOPK_MAX = 256
REL_BUCKETS = 32
REL_MAX_EXACT = 16
REL_MAX_DIST = 128
B_WIDTH = 512
CONV_K = 3
POOL_WINDOWS = (2, 4, 8, 16)
POOL_HALO = 16
X_HEADS = 4
X_HEAD_DIM = 128
X_WIDTH = X_HEADS * X_HEAD_DIM

LANES = 128
SUBLANES = 8
MXU_DIM = 256
VMEM_BYTES_V7X = 64 * 1024 * 1024
VMEM_LIMIT = VMEM_BYTES_V7X - 8 * 1024 * 1024

ATT_TILE = MXU_DIM
FFN_ROWS = 1024
FFN_COLS = MXU_DIM
ROW_TILE = 512

BF16_SUBLANES = 16
PV_ROWS = A_HEAD_DIM + BF16_SUBLANES
LOG2E = math.log2(math.e)
MASKED = -1e30
KEY_MIN = -(2 ** 31)
WORD = 32


def _t5_bucket_table(n):
    d = np.arange(n)
    nf = np.maximum(d, 1).astype(np.float32)
    ratio = np.log(nf / np.float32(REL_MAX_EXACT)) / np.float32(math.log(REL_MAX_DIST / REL_MAX_EXACT))
    large = REL_MAX_EXACT + (ratio * np.float32(REL_BUCKETS - REL_MAX_EXACT)).astype(np.int32)
    large = np.minimum(large, REL_BUCKETS - 1)
    return np.where(d < REL_MAX_EXACT, d, large)


def _rms(x, g):
    ms = jnp.mean(x * x, axis=-1, keepdims=True)
    return x * lax.rsqrt(ms + RMS_EPS) * g


def _params(*semantics):
    return pltpu.CompilerParams(dimension_semantics=semantics, vmem_limit_bytes=VMEM_LIMIT)


def _ffn_kernel(x_ref, gpre_ref, gpost_ref, wg_ref, wu_ref, wd_ref, o_ref, h_ref, a_ref):
    for rows in _halves(x_ref.shape[0]):
        x = x_ref[rows, :]
        h_ref[rows, :] = _rms(x, gpre_ref[...]).astype(BF16)
        for f in range(a_ref.shape[1] // FFN_COLS):
            cols = slice(f * FFN_COLS, (f + 1) * FFN_COLS)
            g = jnp.dot(h_ref[rows, :], wg_ref[:, cols], preferred_element_type=F32)
            u = jnp.dot(h_ref[rows, :], wu_ref[:, cols], preferred_element_type=F32)
            a_ref[rows, cols] = (g * jax.nn.sigmoid(g) * u).astype(BF16)
        y = jnp.dot(a_ref[rows, :], wd_ref[...], preferred_element_type=F32)
        o_ref[rows, :] = x + 0.5 * _rms(y, gpost_ref[...])


def _ffn(x, layer, j, g_pre, g_post, w_gate, w_up, w_down):
    n, d = x.shape
    dff = w_gate.shape[-1]
    assert dff % FFN_COLS == 0
    pick = lambda rows, cols: pl.BlockSpec((None, None, rows, cols), lambda i: (layer, j, 0, 0))
    depth = g_pre.shape[0]
    return pl.pallas_call(
        _ffn_kernel,
        out_shape=jax.ShapeDtypeStruct((n, d), F32),
        grid=(n // FFN_ROWS,),
        in_specs=[pl.BlockSpec((FFN_ROWS, d), lambda i: (i, 0)), pick(1, d), pick(1, d),
                  pick(d, dff), pick(d, dff), pick(dff, d)],
        out_specs=pl.BlockSpec((FFN_ROWS, d), lambda i: (i, 0)),
        scratch_shapes=[pltpu.VMEM((FFN_ROWS, d), BF16), pltpu.VMEM((FFN_ROWS, dff), BF16)],
        compiler_params=_params("parallel"),
        name="ffn",
    )(x, g_pre.reshape(depth, 2, 1, d), g_post.reshape(depth, 2, 1, d), w_gate, w_up, w_down)


def _proj_kernel(x_ref, g_ref, wk_ref, wik_ref, wugg_ref, wqt_ref, wvt_ref, wiqt_ref, wiwt_ref,
                 convw_ref, qt_ref, vt_ref, iqt_ref, iwt_ref, k_ref, ik_ref, c_ref, carry_ref):
    @pl.when(pl.program_id(1) == 0)
    def _():
        carry_ref[...] = jnp.zeros_like(carry_ref)

    h = _rms(x_ref[...], g_ref[...]).astype(BF16)
    ugg = jnp.dot(h, wugg_ref[...], preferred_element_type=F32)
    u = ugg[:, :B_WIDTH]
    gb = ugg[:, B_WIDTH:2 * B_WIDTH]
    gc = ugg[:, 2 * B_WIDTH:]
    gu = gc * u
    t = gu.shape[0]
    ext = jnp.concatenate([carry_ref[...], gu], axis=0)
    w = convw_ref[...]
    y = w[2:3] * gu
    for j in range(CONV_K - 1):
        lo = SUBLANES - (CONV_K - 1) + j
        y = y + w[j:j + 1] * ext[lo:lo + t]
    c_ref[...] = (gb * y).astype(BF16)
    carry_ref[...] = gu[t - SUBLANES:]

    nt = (((1,), (1,)), ((), ()))
    qt = lax.dot_general(wqt_ref[...], h, nt, preferred_element_type=F32)
    qt = (qt * (A_HEAD_DIM ** -0.5 * LOG2E)).astype(BF16)
    vt = lax.dot_general(wvt_ref[...], h, nt, preferred_element_type=F32).astype(BF16)
    iqt = lax.dot_general(wiqt_ref[...], h, nt, preferred_element_type=F32).astype(BF16)
    iwt = lax.dot_general(wiwt_ref[...], h, nt, preferred_element_type=F32)
    iwt = iwt[:IDX_HEADS] * (IDX_HEADS ** -0.5 * IDX_DIM ** -0.5)
    k = jnp.dot(h, wk_ref[...], preferred_element_type=F32).astype(BF16)
    ik = jnp.dot(h, wik_ref[...], preferred_element_type=F32).astype(BF16)
    tile = qt_ref.shape[-1]
    ones = jnp.ones((PV_ROWS - A_HEAD_DIM, tile), BF16)
    for c in range(t // tile):
        tok = slice(c * tile, (c + 1) * tile)
        qt_ref[c] = qt[:, tok]
        iqt_ref[c] = iqt[:, tok]
        iwt_ref[c] = iwt[:, tok]
        k_ref[c] = k[tok]
        ik_ref[c] = ik[tok]
        for hd in range(A_HEADS):
            vt_ref[c, hd * PV_ROWS:hd * PV_ROWS + A_HEAD_DIM, :] = vt[hd * A_HEAD_DIM:(hd + 1) * A_HEAD_DIM, tok]
            vt_ref[c, hd * PV_ROWS + A_HEAD_DIM:(hd + 1) * PV_ROWS, :] = ones


def _proj_even(x, g, w_in, conv_w):
    b, s, d = x.shape
    t = ATT_TILE
    nchunk = s // t
    per = ROW_TILE // t
    q_end, k_end, v_end = A_WIDTH, 2 * A_WIDTH, 3 * A_WIDTH
    iq_end = v_end + IDX_HEADS * IDX_DIM
    ik_end = iq_end + IDX_DIM
    iw_end = ik_end + IDX_HEADS
    wb = w_in.astype(BF16)
    wqt = wb[:, :q_end].T
    wk = wb[:, q_end:k_end]
    wvt = wb[:, k_end:v_end].T
    wiqt = wb[:, v_end:iq_end].T
    wik = wb[:, iq_end:ik_end]
    wiwt = jnp.pad(wb[:, ik_end:iw_end].T, ((0, 2 * SUBLANES - IDX_HEADS), (0, 0)))
    wugg = wb[:, iw_end:]
    const = lambda shape: pl.BlockSpec(shape, lambda bi, i: (0,) * len(shape))
    return pl.pallas_call(
        _proj_kernel,
        out_shape=(
            jax.ShapeDtypeStruct((b, nchunk, A_WIDTH, t), BF16),
            jax.ShapeDtypeStruct((b, nchunk, A_HEADS * PV_ROWS, t), BF16),
            jax.ShapeDtypeStruct((b, nchunk, IDX_HEADS * IDX_DIM, t), BF16),
            jax.ShapeDtypeStruct((b, nchunk, IDX_HEADS, t), F32),
            jax.ShapeDtypeStruct((b, nchunk, t, A_WIDTH), BF16),
            jax.ShapeDtypeStruct((b, nchunk, t, IDX_DIM), BF16),
            jax.ShapeDtypeStruct((b, s, B_WIDTH), BF16),
        ),
        grid=(b, nchunk // per),
        in_specs=[
            pl.BlockSpec((None, per * t, d), lambda bi, i: (bi, i, 0)),
            const((1, d)),
            const(wk.shape), const(wik.shape), const(wugg.shape), const(wqt.shape),
            const(wvt.shape), const(wiqt.shape), const(wiwt.shape), const(conv_w.shape),
        ],
        out_specs=(
            pl.BlockSpec((None, per, A_WIDTH, t), lambda bi, i: (bi, i, 0, 0)),
            pl.BlockSpec((None, per, A_HEADS * PV_ROWS, t), lambda bi, i: (bi, i, 0, 0)),
            pl.BlockSpec((None, per, IDX_HEADS * IDX_DIM, t), lambda bi, i: (bi, i, 0, 0)),
            pl.BlockSpec((None, per, IDX_HEADS, t), lambda bi, i: (bi, i, 0, 0)),
            pl.BlockSpec((None, per, t, A_WIDTH), lambda bi, i: (bi, i, 0, 0)),
            pl.BlockSpec((None, per, t, IDX_DIM), lambda bi, i: (bi, i, 0, 0)),
            pl.BlockSpec((None, per * t, B_WIDTH), lambda bi, i: (bi, i, 0)),
        ),
        scratch_shapes=[pltpu.VMEM((SUBLANES, B_WIDTH), F32)],
        compiler_params=_params("arbitrary", "arbitrary"),
        name="proj_even",
    )(x, g.reshape(1, d), wk, wik, wugg, wqt, wvt, wiqt, wiwt, conv_w)


def _bias_kernel(rb_ref, o_ref, *, thresholds, far_bucket):
    hd = pl.program_id(0)
    t = o_ref.shape[-1]
    r = lax.broadcasted_iota(I32, (t, t), 1)
    c = lax.broadcasted_iota(I32, (t, t), 0)
    for m in range(2):
        dist = t * m + r - c
        val = jnp.full((t, t), rb_ref[0, hd], F32)
        for bucket in range(1, REL_BUCKETS):
            val = jnp.where(dist >= thresholds[bucket], rb_ref[bucket, hd], val)
        val = (val - rb_ref[far_bucket, hd]) * LOG2E
        if m == 0:
            val = jnp.where(dist < 0, MASKED, val)
        o_ref[m] = val
    o_ref[2] = jnp.zeros((t, t), F32)


def _bias_tiles(rel_bias, seq):
    t = ATT_TILE
    table = _t5_bucket_table(seq)
    assert np.all(np.diff(table) >= 0) and table[0] == 0 and table.max() == REL_BUCKETS - 1
    thresholds = tuple(int(np.argmax(table >= bkt)) for bkt in range(REL_BUCKETS))
    far_bucket = int(table[2 * t - 1])
    assert np.all(table[2 * t - 1:] == far_bucket)
    return pl.pallas_call(
        functools.partial(_bias_kernel, thresholds=thresholds, far_bucket=far_bucket),
        out_shape=jax.ShapeDtypeStruct((A_HEADS, 3, t, t), F32),
        grid=(A_HEADS,),
        in_specs=[pl.BlockSpec(memory_space=pltpu.SMEM)],
        out_specs=pl.BlockSpec((None, 3, t, t), lambda h: (h, 0, 0, 0)),
        compiler_params=_params("parallel"),
        name="rel_bias_tiles",
    )(rel_bias)


def _bit_transpose32(words):
    a = list(words)
    shift, mask = WORD // 2, 0x0000FFFF
    while shift:
        k = 0
        while k < WORD:
            swap = (a[k] ^ lax.shift_right_logical(a[k + shift], jnp.int32(shift))) & mask
            a[k] = a[k] ^ swap
            a[k + shift] = a[k + shift] ^ (swap << shift)
            k = (k + shift + 1) & ~shift
        shift //= 2
        mask ^= (mask << shift) & 0xFFFFFFFF
    return a


def _dsa_kernel(qt_ref, iqt_ref, iwt_ref, k_ref, vt_ref, ik_ref, bias_ref, o_ref,
                sc_ref, plane_ref, eq_ref, acc_ref, m_ref, qz_ref, mask_ref, s_ref, cm_ref):
    i = pl.program_id(1)
    t = ATT_TILE
    nk = i + 1
    iw = iwt_ref[...]

    def index_scores(j):
        ikc = ik_ref[j]
        sc = jnp.zeros((t, t), F32)
        for hd in range(IDX_HEADS):
            d = jnp.dot(ikc, iqt_ref[hd * IDX_DIM:(hd + 1) * IDX_DIM, :],
                        preferred_element_type=F32)
            sc = sc + jnp.maximum(d, 0.0) * iw[hd:hd + 1, :]
        return sc

    def store_scores(j, sc):
        sc_ref[j] = sc
        bits = pltpu.bitcast(sc + 0.0, I32)
        ukeys = bits ^ ((bits >> 31) | KEY_MIN)
        for col in range(t // LANES):
            lanes = slice(col * LANES, (col + 1) * LANES)
            planes = _bit_transpose32([ukeys[v * SUBLANES:(v + 1) * SUBLANES, lanes] for v in range(WORD)])
            for r in range(WORD):
                plane_ref[r, j, :, lanes] = planes[r]

    @pl.when(i == 0)
    def _():
        plane_ref[...] = jnp.zeros(plane_ref.shape, I32)

    def phase_a(pair, carry):
        store_scores(2 * pair, index_scores(2 * pair))
        store_scores(2 * pair + 1, index_scores(2 * pair + 1))
        return carry

    lax.fori_loop(0, lax.shift_right_logical(i, 1), phase_a, 0)

    @pl.when((i & 1) == 1)
    def _():
        store_scores(i - 1, index_scores(i - 1))

    qpos = lax.broadcasted_iota(I32, (t, t), 1)
    kpos = lax.broadcasted_iota(I32, (t, t), 0)
    store_scores(i, jnp.where(kpos <= qpos, index_scores(i), -jnp.inf))

    nchunk = eq_ref.shape[0]
    live = lax.broadcasted_iota(I32, eq_ref.shape, 0) < nk
    eq_ref[...] = jnp.where(live, jnp.int32(-1), jnp.int32(0))

    def search(span):
        def search_bit(r, carry):
            thr_u, above = carry
            cnt = jnp.zeros((SUBLANES, t), I32)
            for c in range(span):
                cnt = cnt + lax.population_count(eq_ref[c] & plane_ref[r, c])
            cnt = jnp.sum(cnt.astype(F32), axis=0, keepdims=True)
            take = above + cnt >= float(TOPK_MAX)
            for c in range(span):
                eq = eq_ref[c]
                with_bit = eq & plane_ref[r, c]
                eq_ref[c] = jnp.where(take, with_bit, eq ^ with_bit)
            bit = jnp.left_shift(jnp.int32(1), WORD - 1 - r)
            return jnp.where(take, thr_u | bit, thr_u), jnp.where(take, above, above + cnt)

        return lambda: lax.fori_loop(0, WORD, search_bit,
                                     (jnp.zeros((1, t), I32), jnp.zeros((1, t), F32)))

    quarter = nchunk // 4
    thr_u, _ = lax.cond(
        nk <= 2 * quarter,
        lambda: lax.cond(nk <= quarter, search(quarter), search(2 * quarter)),
        lambda: lax.cond(nk <= 3 * quarter, search(3 * quarter), search(nchunk)))

    def key_to_score(ukey):
        return pltpu.bitcast(ukey ^ (~(ukey >> 31) | KEY_MIN), F32)

    def count_scores(*preds):
        def body(j, cnts):
            sc = sc_ref[j]
            return tuple(cnt + jnp.sum(jnp.where(pred(sc), 1.0, 0.0).reshape(t // SUBLANES, SUBLANES, t), axis=0)
                         for cnt, pred in zip(cnts, preds))
        cnts = lax.fori_loop(0, nk, body, tuple(jnp.zeros((SUBLANES, t), F32) for _ in preds))
        return [jnp.sum(cnt, axis=0, keepdims=True) for cnt in cnts]

    thr_fast = key_to_score(thr_u)
    reach, exceed = count_scores(lambda sc: sc >= thr_fast, lambda sc: sc > thr_fast)
    good = jnp.logical_and(exceed < float(TOPK_MAX), reach >= float(TOPK_MAX))

    def search_by_value():
        def step(r, ukey):
            cand = ukey | jnp.left_shift(jnp.int32(1), WORD - 1 - r)
            cand_score = key_to_score(cand)
            reach, = count_scores(lambda sc: sc >= cand_score)
            return jnp.where(reach >= float(TOPK_MAX), cand, ukey)
        ukey = lax.fori_loop(0, WORD, step, jnp.zeros((1, t), I32))
        thr_slow = jnp.where(ukey == 0, -jnp.inf, key_to_score(ukey))
        exceed, = count_scores(lambda sc: sc > thr_slow)
        return thr_slow, exceed

    thr, exceed = lax.cond(jnp.min(jnp.where(good, 1.0, 0.0)) > 0.5,
                           lambda: (thr_fast, exceed), search_by_value)
    need = float(TOPK_MAX) - exceed

    m_ref[...] = jnp.full(m_ref.shape, MASKED, F32)
    acc_ref[...] = jnp.zeros(acc_ref.shape, F32)
    row = lax.broadcasted_iota(I32, (2 * A_HEAD_DIM, t), 0)
    for hd in range(A_HEADS):
        pair = hd // 2
        blk = qt_ref[2 * A_HEAD_DIM * pair:2 * A_HEAD_DIM * (pair + 1), :]
        own = (row >= A_HEAD_DIM) if hd % 2 else (row < A_HEAD_DIM)
        qz_ref[hd] = jnp.where(own, blk, jnp.zeros_like(blk))
    tri = (lax.broadcasted_iota(I32, (t, t), 0) >= lax.broadcasted_iota(I32, (t, t), 1)).astype(BF16)

    def select(j, ties_before):
        kk = sc_ref[j]
        is_tie = kk == thr
        tie_rank = jnp.dot(tri, jnp.where(is_tie, 1.0, 0.0).astype(BF16),
                           preferred_element_type=F32) + ties_before
        tie_add = jnp.where(is_tie, jnp.where(tie_rank <= need, 0.0, MASKED), MASKED)
        mask_ref[...] = jnp.where(kk > thr, 0.0, tie_add).astype(BF16)
        return tie_rank[t - 1:t, :]

    slabs = (t // BF16_SUBLANES, BF16_SUBLANES, t)

    def scores(with_bias, slot, j, hd):
        pair = hd // 2
        s = jnp.dot(k_ref[j, :, 2 * A_HEAD_DIM * pair:2 * A_HEAD_DIM * (pair + 1)], qz_ref[hd],
                    preferred_element_type=F32)
        if with_bias:
            s = s + bias_ref[hd, jnp.minimum(i - j, 2)]
        s = s.astype(BF16) + mask_ref[...]
        s_ref[slot, hd] = s
        parts = [s.reshape(slabs)[r] for r in range(slabs[0])]
        while len(parts) > 1:
            parts = [jnp.maximum(a, b) for a, b in zip(parts[::2], parts[1::2])]
        cm_ref[slot, hd] = jnp.max(parts[0].astype(F32), axis=0, keepdims=True)

    def consume(slot, j, hd):
        m_old = m_ref[hd]
        m_new = jnp.maximum(m_old, cm_ref[slot, hd])
        m_ref[hd] = m_new
        alpha = jnp.exp2(m_old - m_new)
        m_tile = jnp.broadcast_to(m_new, (BF16_SUBLANES, t)).astype(BF16)
        p = jnp.exp2(s_ref[slot, hd].reshape(slabs) - m_tile[None]).reshape(t, t)
        pv = jnp.dot(vt_ref[j, hd * PV_ROWS:(hd + 1) * PV_ROWS, :], p, preferred_element_type=F32)
        acc_ref[hd] = alpha * acc_ref[hd] + pv

    def half_step(with_bias, slot, j, ties):
        ties = select(j, ties)
        for hd in range(A_HEADS):
            scores(with_bias, slot, j, hd)
            consume(1 - slot, j - 1, hd)
        return ties

    def step(with_bias, j, ties_before):
        return lax.cond(lax.rem(j, 2) == 1, functools.partial(half_step, with_bias, 1, j),
                        functools.partial(half_step, with_bias, 0, j), ties_before)

    def far_pair(pair, ties):
        ties = half_step(False, 1, 2 * pair + 1, ties)
        return half_step(False, 0, 2 * pair + 2, ties)

    ties = select(0, jnp.zeros((1, t), F32))
    for hd in range(A_HEADS):
        scores(True, 0, 0, hd)
    first_near = jnp.maximum(i - 1, 1)
    far_pairs = lax.shift_right_logical(first_near - 1, 1)
    ties = lax.fori_loop(0, far_pairs, far_pair, ties)
    ties = lax.fori_loop(2 * far_pairs + 1, first_near, functools.partial(step, False), ties)
    lax.fori_loop(first_near, nk, functools.partial(step, True), ties)
    for slot in range(2):
        @pl.when(lax.rem(i, 2) == slot)
        def _():
            for hd in range(A_HEADS):
                consume(slot, i, hd)

    outs = []
    for hd in range(A_HEADS):
        outs.append(acc_ref[hd, :A_HEAD_DIM, :] / acc_ref[hd, A_HEAD_DIM:A_HEAD_DIM + 1, :])
    o_ref[...] = jnp.concatenate(outs, axis=0).T.astype(BF16)


def _dsa_attention(qt, vt, iqt, iwt, k, ik, bias):
    b, nchunk, t, _ = k.shape
    s = nchunk * t
    return pl.pallas_call(
        _dsa_kernel,
        out_shape=jax.ShapeDtypeStruct((b, s, A_WIDTH), BF16),
        grid=(b, nchunk),
        in_specs=[
            pl.BlockSpec((None, None, A_WIDTH, t), lambda bi, i: (bi, i, 0, 0)),
            pl.BlockSpec((None, None, IDX_HEADS * IDX_DIM, t), lambda bi, i: (bi, i, 0, 0)),
            pl.BlockSpec((None, None, IDX_HEADS, t), lambda bi, i: (bi, i, 0, 0)),
            pl.BlockSpec((None, nchunk, t, A_WIDTH), lambda bi, i: (bi, 0, 0, 0)),
            pl.BlockSpec((None, nchunk, A_HEADS * PV_ROWS, t), lambda bi, i: (bi, 0, 0, 0)),
            pl.BlockSpec((None, nchunk, t, IDX_DIM), lambda bi, i: (bi, 0, 0, 0)),
            pl.BlockSpec(bias.shape, lambda bi, i: (0, 0, 0, 0)),
        ],
        out_specs=pl.BlockSpec((None, t, A_WIDTH), lambda bi, i: (bi, i, 0)),
        scratch_shapes=[
            pltpu.VMEM((nchunk, t, t), F32),
            pltpu.VMEM((WORD, nchunk, t // WORD, t), I32),
            pltpu.VMEM((nchunk, t // WORD, t), I32),
            pltpu.VMEM((A_HEADS, PV_ROWS, t), F32),
            pltpu.VMEM((A_HEADS, 1, t), F32),
            pltpu.VMEM((A_HEADS, 2 * A_HEAD_DIM, t), BF16),
            pltpu.VMEM((t, t), BF16),
            pltpu.VMEM((2, A_HEADS, t, t), BF16),
            pltpu.VMEM((2, A_HEADS, 1, t), F32),
        ],
        compiler_params=_params("parallel", "arbitrary"),
        name="dsa_attention",
    )(qt, iqt, iwt, k, vt, ik, bias)


def _halves(n):
    return (slice(0, n // 2), slice(n // 2, n))


def _xattn_block(parts, gpre_ref, gpost_ref, wq_ref, kt_ref, v_ref, wo_ref):
    heads = [slice(hd * X_HEAD_DIM, (hd + 1) * X_HEAD_DIM) for hd in range(X_HEADS)]
    hs = [_rms(x, gpre_ref[...]).astype(BF16) for x in parts]
    qs = [jnp.dot(h, wq_ref[...], preferred_element_type=F32) for h in hs]
    logits = [[jnp.dot(q[:, cols].astype(BF16), kt_ref[cols, :], preferred_element_type=F32)
               * (X_HEAD_DIM ** -0.5) for cols in heads] for q in qs]
    os = []
    for part_logits in logits:
        outs = []
        for cols, lg in zip(heads, part_logits):
            p = jnp.exp(lg - jnp.max(lg, axis=-1, keepdims=True))
            den = jnp.sum(p, axis=-1, keepdims=True)
            outs.append(jnp.dot(p.astype(BF16), v_ref[:, cols], preferred_element_type=F32) / den)
        os.append(jnp.concatenate(outs, axis=-1).astype(BF16))
    ys = [jnp.dot(o, wo_ref[...], preferred_element_type=F32) for o in os]
    return [x + _rms(y, gpost_ref[...]) for x, y in zip(parts, ys)]


def _xattn_operands(d, m, g_pre, g_post, wq, kt, v, wo):
    const = lambda shape: pl.BlockSpec(shape, lambda bi, i: (0, 0))
    specs = [const((1, d)), const((1, d)), const((d, X_WIDTH)),
             pl.BlockSpec((None, X_WIDTH, m), lambda bi, i: (bi, 0, 0)),
             pl.BlockSpec((None, m, X_WIDTH), lambda bi, i: (bi, 0, 0)),
             const((X_WIDTH, d))]
    return specs, (g_pre.reshape(1, d), g_post.reshape(1, d), wq.astype(BF16), kt, v, wo.astype(BF16))


def _mix_out_xattn_kernel(x_ref, a_ref, c_ref, wa_ref, wc_ref, g_ref, *rest):
    *xattn_refs, o_ref = rest
    halves = _halves(x_ref.shape[0])
    ys = [jnp.dot(a_ref[rows, :], wa_ref[...], preferred_element_type=F32)
          + jnp.dot(c_ref[rows, :], wc_ref[...], preferred_element_type=F32) for rows in halves]
    x1 = [x_ref[rows, :] + _rms(y, g_ref[...]) for rows, y in zip(halves, ys)]
    for rows, out in zip(halves, _xattn_block(x1, *xattn_refs)):
        o_ref[rows, :] = out


def _mix_out_xattn(x, a, c, w_out, g_post, xattn_args):
    b, s, d = x.shape
    t = ROW_TILE
    wb = w_out.astype(BF16)
    row = lambda width: pl.BlockSpec((None, t, width), lambda bi, i: (bi, i, 0))
    const = lambda shape: pl.BlockSpec(shape, lambda bi, i: (0, 0))
    xspecs, xarrays = _xattn_operands(d, xattn_args[3].shape[-1], *xattn_args)
    return pl.pallas_call(
        _mix_out_xattn_kernel,
        out_shape=jax.ShapeDtypeStruct((b, s, d), F32),
        grid=(b, s // t),
        in_specs=[row(d), row(A_WIDTH), row(B_WIDTH), const((A_WIDTH, d)), const((B_WIDTH, d)),
                  const((1, d))] + xspecs,
        out_specs=row(d),
        compiler_params=_params("parallel", "parallel"),
        name="mix_out_xattn",
    )(x, a, c, wb[:A_WIDTH], wb[A_WIDTH:], g_post.reshape(1, d), *xarrays)


def _pool_xattn_kernel(x_ref, xprev_ref, gpre_ref, gpost_ref, w_ref, scale_ref, *rest):
    *xattn_refs, o_ref = rest
    i = pl.program_id(1)
    x = x_ref[...]
    t, d = x.shape
    gpre = gpre_ref[...]
    h = _rms(x, gpre)
    hp = _rms(xprev_ref[...], gpre) * jnp.where(i > 0, 1.0, 0.0)
    ext = jnp.concatenate([hp, h], axis=0)
    group = d // len(POOL_WINDOWS)
    pos = i * t + lax.broadcasted_iota(I32, (t, group), 0)
    pooled = []
    for gi, win in enumerate(POOL_WINDOWS):
        sums = ext[:, gi * group:(gi + 1) * group]
        width = 1
        while width < win:
            sums = sums[width:] + sums[:-width]
            width *= 2
        wsum = sums[POOL_HALO - (win - 1):POOL_HALO - (win - 1) + t]
        cnt = jnp.minimum(pos + 1, win).astype(F32)
        pooled.append((wsum / cnt - h[:, gi * group:(gi + 1) * group]).astype(BF16))
    halves = _halves(t)
    x1 = []
    for rows in halves:
        y = jnp.concatenate([jnp.dot(pg[rows], w_ref[gi], preferred_element_type=F32)
                             for gi, pg in enumerate(pooled)], axis=-1) * scale_ref[...]
        x1.append(x[rows] + _rms(y, gpost_ref[...]))
    for rows, out in zip(halves, _xattn_block(x1, *xattn_refs)):
        o_ref[rows, :] = out


def _pool_xattn(x, g_pre, g_post, pool_w, pool_scale, xattn_args):
    b, s, d = x.shape
    t = ROW_TILE
    const = lambda shape: pl.BlockSpec(shape, lambda bi, i: (0,) * len(shape))
    prev = t // POOL_HALO
    xspecs, xarrays = _xattn_operands(d, xattn_args[3].shape[-1], *xattn_args)
    return pl.pallas_call(
        _pool_xattn_kernel,
        out_shape=jax.ShapeDtypeStruct((b, s, d), F32),
        grid=(b, s // t),
        in_specs=[
            pl.BlockSpec((None, t, d), lambda bi, i: (bi, i, 0)),
            pl.BlockSpec((None, POOL_HALO, d), lambda bi, i: (bi, jnp.maximum(i * prev - 1, 0), 0)),
            const((1, d)), const((1, d)), const(pool_w.shape), const((1, d)),
        ] + xspecs,
        out_specs=pl.BlockSpec((None, t, d), lambda bi, i: (bi, i, 0)),
        compiler_params=_params("parallel", "parallel"),
        name="pool_xattn",
    )(x, x, g_pre.reshape(1, d), g_post.reshape(1, d), pool_w.astype(BF16), pool_scale.reshape(1, d),
      *xarrays)


def _mem_kv_kernel(mem_ref, g_ref, wkt_ref, wv_ref, kt_ref, v_ref):
    mem_n = _rms(mem_ref[...], g_ref[...]).astype(BF16)
    kt = lax.dot_general(wkt_ref[...], mem_n, (((1,), (1,)), ((), ())), preferred_element_type=F32)
    kt_ref[...] = kt.astype(BF16)
    v_ref[...] = jnp.dot(mem_n, wv_ref[...], preferred_element_type=F32).astype(BF16)


def _mem_kv(mem, g, wk, wv):
    b, m, d = mem.shape
    const = lambda shape: pl.BlockSpec(shape, lambda bi: (0, 0))
    return pl.pallas_call(
        _mem_kv_kernel,
        out_shape=(jax.ShapeDtypeStruct((b, X_WIDTH, m), BF16), jax.ShapeDtypeStruct((b, m, X_WIDTH), BF16)),
        grid=(b,),
        in_specs=[pl.BlockSpec((None, m, d), lambda bi: (bi, 0, 0)), const((1, d)), const((X_WIDTH, d)),
                  const((d, X_WIDTH))],
        out_specs=(pl.BlockSpec((None, X_WIDTH, m), lambda bi: (bi, 0, 0)),
                   pl.BlockSpec((None, m, X_WIDTH), lambda bi: (bi, 0, 0))),
        compiler_params=_params("parallel"),
        name="mem_kv",
    )(mem, g.reshape(1, d), wk.astype(BF16).T, wv.astype(BF16))


def kernel(x, mem, ffn_w_gate, ffn_w_up, ffn_w_down, ffn_norm_pre, ffn_norm_post, mix_norm_pre,
           mix_norm_post, even_w_in, even_conv_w, even_w_out, rel_bias, pool_w, pool_scale,
           xattn_norm_pre, xattn_mem_norm, xattn_norm_post, xattn_wq, xattn_wk, xattn_wv, xattn_wo):
    b, s, d = x.shape
    depth = ffn_w_gate.shape[0]
    assert s % ATT_TILE == 0 and s // 4 >= TOPK_MAX and (b * s) % FFN_ROWS == 0 and s % ROW_TILE == 0
    bias = _bias_tiles(rel_bias, s)

    ffn_weights = (ffn_w_gate.astype(BF16), ffn_w_up.astype(BF16), ffn_w_down.astype(BF16))

    def ffn(xx, layer, j):
        out = _ffn(xx.reshape(b * s, d), layer, j, ffn_norm_pre, ffn_norm_post, *ffn_weights)
        return out.reshape(b, s, d)

    for layer in range(depth):
        x = ffn(x, layer, 0)
        kt, v = _mem_kv(mem, xattn_mem_norm[layer], xattn_wk[layer], xattn_wv[layer])
        xattn_args = (xattn_norm_pre[layer], xattn_norm_post[layer], xattn_wq[layer], kt, v,
                      xattn_wo[layer])
        if layer % 2 == 0:
            e = layer // 2
            qt, vt, iqt, iwt, k, ik, c = _proj_even(x, mix_norm_pre[layer], even_w_in[e], even_conv_w[e])
            a = _dsa_attention(qt, vt, iqt, iwt, k, ik, bias)
            x = _mix_out_xattn(x, a, c, even_w_out[e], mix_norm_post[layer], xattn_args)
        else:
            o = layer // 2
            x = _pool_xattn(x, mix_norm_pre[layer], mix_norm_post[layer], pool_w[o], pool_scale[o],
                            xattn_args)
        x = ffn(x, layer, 1)
    return x
```

```python
import functools
import math

import jax
import jax.numpy as jnp
import numpy as np
from jax import lax
from jax.experimental import pallas as pl
from jax.experimental.pallas import tpu as pltpu

F32 = jnp.float32
BF16 = jnp.bfloat16
I32 = jnp.int32

RMS_EPS = 1e-6

A_HEADS = 8
A_HEAD_DIM = 64
A_WIDTH = A_HEADS * A_HEAD_DIM
IDX_HEADS = 8
IDX_DIM = 64
TOPK_MAX = 256
REL_BUCKETS = 32
REL_MAX_EXACT = 16
REL_MAX_DIST = 128
B_WIDTH = 512
CONV_K = 3
POOL_WINDOWS = (2, 4, 8, 16)
POOL_HALO = 16
X_HEADS = 4
X_HEAD_DIM = 128
X_WIDTH = X_HEADS * X_HEAD_DIM

LANES = 128
SUBLANES = 8
MXU_DIM = 256
VMEM_BYTES_V7X = 64 * 1024 * 1024
VMEM_LIMIT = VMEM_BYTES_V7X - 8 * 1024 * 1024

ATT_TILE = MXU_DIM
FFN_ROWS = 1024
FFN_COLS = MXU_DIM
ROW_TILE = 512

BF16_SUBLANES = 16
PV_ROWS = A_HEAD_DIM + BF16_SUBLANES
LOG2E = math.log2(math.e)
MASKED = -1e30
KEY_MIN = -(2 ** 31)
WORD = 32


def _t5_bucket_table(n):
    d = np.arange(n)
    nf = np.maximum(d, 1).astype(np.float32)
    ratio = np.log(nf / np.float32(REL_MAX_EXACT)) / np.float32(math.log(REL_MAX_DIST / REL_MAX_EXACT))
    large = REL_MAX_EXACT + (ratio * np.float32(REL_BUCKETS - REL_MAX_EXACT)).astype(np.int32)
    large = np.minimum(large, REL_BUCKETS - 1)
    return np.where(d < REL_MAX_EXACT, d, large)


def _rms(x, g):
    ms = jnp.mean(x * x, axis=-1, keepdims=True)
    return x * lax.rsqrt(ms + RMS_EPS) * g


def _params(*semantics):
    return pltpu.CompilerParams(dimension_semantics=semantics, vmem_limit_bytes=VMEM_LIMIT)


def _ffn_kernel(x_ref, gpre_ref, gpost_ref, wg_ref, wu_ref, wd_ref, o_ref, h_ref, a_ref):
    for rows in _halves(x_ref.shape[0]):
        x = x_ref[rows, :]
        h_ref[rows, :] = _rms(x, gpre_ref[...]).astype(BF16)
        for f in range(a_ref.shape[1] // FFN_COLS):
            cols = slice(f * FFN_COLS, (f + 1) * FFN_COLS)
            g = jnp.dot(h_ref[rows, :], wg_ref[:, cols], preferred_element_type=F32)
            u = jnp.dot(h_ref[rows, :], wu_ref[:, cols], preferred_element_type=F32)
            a_ref[rows, cols] = (g * jax.nn.sigmoid(g) * u).astype(BF16)
        y = jnp.dot(a_ref[rows, :], wd_ref[...], preferred_element_type=F32)
        o_ref[rows, :] = x + 0.5 * _rms(y, gpost_ref[...])


def _ffn(x, layer, j, g_pre, g_post, w_gate, w_up, w_down):
    n, d = x.shape
    dff = w_gate.shape[-1]
    assert dff % FFN_COLS == 0
    pick = lambda rows, cols: pl.BlockSpec((None, None, rows, cols), lambda i: (layer, j, 0, 0))
    depth = g_pre.shape[0]
    return pl.pallas_call(
        _ffn_kernel,
        out_shape=jax.ShapeDtypeStruct((n, d), F32),
        grid=(n // FFN_ROWS,),
        in_specs=[pl.BlockSpec((FFN_ROWS, d), lambda i: (i, 0)), pick(1, d), pick(1, d),
                  pick(d, dff), pick(d, dff), pick(dff, d)],
        out_specs=pl.BlockSpec((FFN_ROWS, d), lambda i: (i, 0)),
        scratch_shapes=[pltpu.VMEM((FFN_ROWS, d), BF16), pltpu.VMEM((FFN_ROWS, dff), BF16)],
        compiler_params=_params("parallel"),
        name="ffn",
    )(x, g_pre.reshape(depth, 2, 1, d), g_post.reshape(depth, 2, 1, d), w_gate, w_up, w_down)


def _proj_kernel(x_ref, g_ref, wk_ref, wik_ref, wugg_ref, wt_ref,
                 convw_ref, qt_ref, vt_ref, iqt_ref, iwt_ref, k_ref, ik_ref, c_ref, carry_ref):
    @pl.when(pl.program_id(1) == 0)
    def _():
        carry_ref[...] = jnp.zeros_like(carry_ref)

    h = _rms(x_ref[...], g_ref[...]).astype(BF16)
    ugg = jnp.dot(h, wugg_ref[...], preferred_element_type=F32)
    u = ugg[:, :B_WIDTH]
    gb = ugg[:, B_WIDTH:2 * B_WIDTH]
    gc = ugg[:, 2 * B_WIDTH:]
    gu = gc * u
    t = gu.shape[0]
    ext = jnp.concatenate([carry_ref[...], gu], axis=0)
    w = convw_ref[...]
    y = w[2:3] * gu
    for j in range(CONV_K - 1):
        lo = SUBLANES - (CONV_K - 1) + j
        y = y + w[j:j + 1] * ext[lo:lo + t]
    c_ref[...] = (gb * y).astype(BF16)
    carry_ref[...] = gu[t - SUBLANES:]

    tr = lax.dot_general(wt_ref[...], h, (((1,), (1,)), ((), ())), preferred_element_type=F32)
    qt = (tr[:A_WIDTH] * (A_HEAD_DIM ** -0.5 * LOG2E)).astype(BF16)
    vt = tr[A_WIDTH:2 * A_WIDTH].astype(BF16)
    iq_end = 2 * A_WIDTH + IDX_HEADS * IDX_DIM
    iqt = tr[2 * A_WIDTH:iq_end].astype(BF16)
    iwt = tr[iq_end:iq_end + IDX_HEADS] * (IDX_HEADS ** -0.5 * IDX_DIM ** -0.5)
    k = jnp.dot(h, wk_ref[...], preferred_element_type=F32).astype(BF16)
    ik = jnp.dot(h, wik_ref[...], preferred_element_type=F32).astype(BF16)
    tile = qt_ref.shape[-1]
    ones = jnp.ones((PV_ROWS - A_HEAD_DIM, tile), BF16)
    for c in range(t // tile):
        tok = slice(c * tile, (c + 1) * tile)
        qt_ref[c] = qt[:, tok]
        iqt_ref[c] = iqt[:, tok]
        iwt_ref[c] = iwt[:, tok]
        k_ref[c] = k[tok]
        ik_ref[c] = ik[tok]
        for hd in range(A_HEADS):
            vt_ref[c, hd * PV_ROWS:hd * PV_ROWS + A_HEAD_DIM, :] = vt[hd * A_HEAD_DIM:(hd + 1) * A_HEAD_DIM, tok]
            vt_ref[c, hd * PV_ROWS + A_HEAD_DIM:(hd + 1) * PV_ROWS, :] = ones


def _proj_even(x, g, w_in, conv_w):
    b, s, d = x.shape
    t = ATT_TILE
    nchunk = s // t
    per = ROW_TILE // t
    q_end, k_end, v_end = A_WIDTH, 2 * A_WIDTH, 3 * A_WIDTH
    iq_end = v_end + IDX_HEADS * IDX_DIM
    ik_end = iq_end + IDX_DIM
    iw_end = ik_end + IDX_HEADS
    wb = w_in.astype(BF16)
    wk = wb[:, q_end:k_end]
    wik = wb[:, iq_end:ik_end]
    wugg = wb[:, iw_end:]
    wt = jnp.concatenate([wb[:, :q_end], wb[:, k_end:v_end], wb[:, v_end:iq_end], wb[:, ik_end:iw_end]], axis=1).T
    wt = jnp.pad(wt, ((0, BF16_SUBLANES - IDX_HEADS), (0, 0)))
    const = lambda shape: pl.BlockSpec(shape, lambda bi, i: (0,) * len(shape))
    return pl.pallas_call(
        _proj_kernel,
        out_shape=(
            jax.ShapeDtypeStruct((b, nchunk, A_WIDTH, t), BF16),
            jax.ShapeDtypeStruct((b, nchunk, A_HEADS * PV_ROWS, t), BF16),
            jax.ShapeDtypeStruct((b, nchunk, IDX_HEADS * IDX_DIM, t), BF16),
            jax.ShapeDtypeStruct((b, nchunk, IDX_HEADS, t), F32),
            jax.ShapeDtypeStruct((b, nchunk, t, A_WIDTH), BF16),
            jax.ShapeDtypeStruct((b, nchunk, t, IDX_DIM), BF16),
            jax.ShapeDtypeStruct((b, s, B_WIDTH), BF16),
        ),
        grid=(b, nchunk // per),
        in_specs=[
            pl.BlockSpec((None, per * t, d), lambda bi, i: (bi, i, 0)),
            const((1, d)),
            const(wk.shape), const(wik.shape), const(wugg.shape), const(wt.shape), const(conv_w.shape),
        ],
        out_specs=(
            pl.BlockSpec((None, per, A_WIDTH, t), lambda bi, i: (bi, i, 0, 0)),
            pl.BlockSpec((None, per, A_HEADS * PV_ROWS, t), lambda bi, i: (bi, i, 0, 0)),
            pl.BlockSpec((None, per, IDX_HEADS * IDX_DIM, t), lambda bi, i: (bi, i, 0, 0)),
            pl.BlockSpec((None, per, IDX_HEADS, t), lambda bi, i: (bi, i, 0, 0)),
            pl.BlockSpec((None, per, t, A_WIDTH), lambda bi, i: (bi, i, 0, 0)),
            pl.BlockSpec((None, per, t, IDX_DIM), lambda bi, i: (bi, i, 0, 0)),
            pl.BlockSpec((None, per * t, B_WIDTH), lambda bi, i: (bi, i, 0)),
        ),
        scratch_shapes=[pltpu.VMEM((SUBLANES, B_WIDTH), F32)],
        compiler_params=_params("arbitrary", "arbitrary"),
        name="proj_even",
    )(x, g.reshape(1, d), wk, wik, wugg, wt, conv_w)


def _bias_kernel(rb_ref, o_ref, *, thresholds, far_bucket):
    hd = pl.program_id(0)
    t = o_ref.shape[-1]
    r = lax.broadcasted_iota(I32, (t, t), 1)
    c = lax.broadcasted_iota(I32, (t, t), 0)
    for m in range(2):
        dist = t * m + r - c
        val = jnp.full((t, t), rb_ref[0, hd], F32)
        for bucket in range(1, REL_BUCKETS):
            val = jnp.where(dist >= thresholds[bucket], rb_ref[bucket, hd], val)
        val = (val - rb_ref[far_bucket, hd]) * LOG2E
        if m == 0:
            val = jnp.where(dist < 0, MASKED, val)
        o_ref[m] = val
    o_ref[2] = jnp.zeros((t, t), F32)


def _bias_tiles(rel_bias, seq):
    t = ATT_TILE
    table = _t5_bucket_table(seq)
    assert np.all(np.diff(table) >= 0) and table[0] == 0 and table.max() == REL_BUCKETS - 1
    thresholds = tuple(int(np.argmax(table >= bkt)) for bkt in range(REL_BUCKETS))
    far_bucket = int(table[2 * t - 1])
    assert np.all(table[2 * t - 1:] == far_bucket)
    return pl.pallas_call(
        functools.partial(_bias_kernel, thresholds=thresholds, far_bucket=far_bucket),
        out_shape=jax.ShapeDtypeStruct((A_HEADS, 3, t, t), F32),
        grid=(A_HEADS,),
        in_specs=[pl.BlockSpec(memory_space=pltpu.SMEM)],
        out_specs=pl.BlockSpec((None, 3, t, t), lambda h: (h, 0, 0, 0)),
        compiler_params=_params("parallel"),
        name="rel_bias_tiles",
    )(rel_bias)


def _bit_transpose32(words):
    a = list(words)
    shift, mask = WORD // 2, 0x0000FFFF
    while shift:
        k = 0
        while k < WORD:
            swap = (a[k] ^ lax.shift_right_logical(a[k + shift], jnp.int32(shift))) & mask
            a[k] = a[k] ^ swap
            a[k + shift] = a[k + shift] ^ (swap << shift)
            k = (k + shift + 1) & ~shift
        shift //= 2
        mask ^= (mask << shift) & 0xFFFFFFFF
    return a


def _dsa_kernel(qt_ref, iqt_ref, iwt_ref, k_ref, vt_ref, ik_ref, bias_ref, o_ref,
                sc_ref, plane_ref, eq_ref, acc_ref, m_ref, qz_ref, mask_ref, s_ref, cm_ref):
    i = pl.program_id(1)
    t = ATT_TILE
    nk = i + 1
    iw = iwt_ref[...]

    def index_scores(j):
        ikc = ik_ref[j]
        sc = jnp.zeros((t, t), F32)
        for hd in range(IDX_HEADS):
            d = jnp.dot(ikc, iqt_ref[hd * IDX_DIM:(hd + 1) * IDX_DIM, :],
                        preferred_element_type=F32)
            sc = sc + jnp.maximum(d, 0.0) * iw[hd:hd + 1, :]
        return sc

    def store_scores(j, sc):
        sc_ref[j] = sc
        bits = pltpu.bitcast(sc + 0.0, I32)
        ukeys = bits ^ ((bits >> 31) | KEY_MIN)
        for col in range(t // LANES):
            lanes = slice(col * LANES, (col + 1) * LANES)
            planes = _bit_transpose32([ukeys[v * SUBLANES:(v + 1) * SUBLANES, lanes] for v in range(WORD)])
            for r in range(WORD):
                plane_ref[r, j, :, lanes] = planes[r]

    @pl.when(i == 0)
    def _():
        plane_ref[...] = jnp.zeros(plane_ref.shape, I32)

    def phase_a(pair, carry):
        store_scores(2 * pair, index_scores(2 * pair))
        store_scores(2 * pair + 1, index_scores(2 * pair + 1))
        return carry

    lax.fori_loop(0, lax.shift_right_logical(i, 1), phase_a, 0)

    @pl.when((i & 1) == 1)
    def _():
        store_scores(i - 1, index_scores(i - 1))

    qpos = lax.broadcasted_iota(I32, (t, t), 1)
    kpos = lax.broadcasted_iota(I32, (t, t), 0)
    store_scores(i, jnp.where(kpos <= qpos, index_scores(i), -jnp.inf))

    nchunk = eq_ref.shape[0]
    live = lax.broadcasted_iota(I32, eq_ref.shape, 0) < nk
    eq_ref[...] = jnp.where(live, jnp.int32(-1), jnp.int32(0))

    def search(span):
        def search_bit(r, carry):
            thr_u, above = carry
            cnt = jnp.zeros((SUBLANES, t), I32)
            for c in range(span):
                cnt = cnt + lax.population_count(eq_ref[c] & plane_ref[r, c])
            cnt = jnp.sum(cnt.astype(F32), axis=0, keepdims=True)
            take = above + cnt >= float(TOPK_MAX)
            for c in range(span):
                eq = eq_ref[c]
                with_bit = eq & plane_ref[r, c]
                eq_ref[c] = jnp.where(take, with_bit, eq ^ with_bit)
            bit = jnp.left_shift(jnp.int32(1), WORD - 1 - r)
            return jnp.where(take, thr_u | bit, thr_u), jnp.where(take, above, above + cnt)

        return lambda: lax.fori_loop(0, WORD, search_bit,
                                     (jnp.zeros((1, t), I32), jnp.zeros((1, t), F32)))

    quarter = nchunk // 4
    thr_u, _ = lax.cond(
        nk <= 2 * quarter,
        lambda: lax.cond(nk <= quarter, search(quarter), search(2 * quarter)),
        lambda: lax.cond(nk <= 3 * quarter, search(3 * quarter), search(nchunk)))

    def key_to_score(ukey):
        return pltpu.bitcast(ukey ^ (~(ukey >> 31) | KEY_MIN), F32)

    def count_scores(*preds):
        def body(j, cnts):
            sc = sc_ref[j]
            return tuple(cnt + jnp.sum(jnp.where(pred(sc), 1.0, 0.0).reshape(t // SUBLANES, SUBLANES, t), axis=0)
                         for cnt, pred in zip(cnts, preds))
        cnts = lax.fori_loop(0, nk, body, tuple(jnp.zeros((SUBLANES, t), F32) for _ in preds))
        return [jnp.sum(cnt, axis=0, keepdims=True) for cnt in cnts]

    thr_fast = key_to_score(thr_u)
    reach, exceed = count_scores(lambda sc: sc >= thr_fast, lambda sc: sc > thr_fast)
    good = jnp.logical_and(exceed < float(TOPK_MAX), reach >= float(TOPK_MAX))

    def search_by_value():
        def step(r, ukey):
            cand = ukey | jnp.left_shift(jnp.int32(1), WORD - 1 - r)
            cand_score = key_to_score(cand)
            reach, = count_scores(lambda sc: sc >= cand_score)
            return jnp.where(reach >= float(TOPK_MAX), cand, ukey)
        ukey = lax.fori_loop(0, WORD, step, jnp.zeros((1, t), I32))
        thr_slow = jnp.where(ukey == 0, -jnp.inf, key_to_score(ukey))
        exceed, = count_scores(lambda sc: sc > thr_slow)
        return thr_slow, exceed

    thr, exceed = lax.cond(jnp.min(jnp.where(good, 1.0, 0.0)) > 0.5,
                           lambda: (thr_fast, exceed), search_by_value)
    need = float(TOPK_MAX) - exceed

    m_ref[...] = jnp.full(m_ref.shape, MASKED, F32)
    acc_ref[...] = jnp.zeros(acc_ref.shape, F32)
    row = lax.broadcasted_iota(I32, (2 * A_HEAD_DIM, t), 0)
    for hd in range(A_HEADS):
        pair = hd // 2
        blk = qt_ref[2 * A_HEAD_DIM * pair:2 * A_HEAD_DIM * (pair + 1), :]
        own = (row >= A_HEAD_DIM) if hd % 2 else (row < A_HEAD_DIM)
        qz_ref[hd] = jnp.where(own, blk, jnp.zeros_like(blk))
    tri = (lax.broadcasted_iota(I32, (t, t), 0) >= lax.broadcasted_iota(I32, (t, t), 1)).astype(BF16)

    def select(j, ties_before):
        kk = sc_ref[j]
        is_tie = kk == thr
        tie_rank = jnp.dot(tri, jnp.where(is_tie, 1.0, 0.0).astype(BF16),
                           preferred_element_type=F32) + ties_before
        tie_add = jnp.where(is_tie, jnp.where(tie_rank <= need, 0.0, MASKED), MASKED)
        mask_ref[...] = jnp.where(kk > thr, 0.0, tie_add).astype(BF16)
        return tie_rank[t - 1:t, :]

    slabs = (t // BF16_SUBLANES, BF16_SUBLANES, t)

    def scores(with_bias, slot, j, hd):
        pair = hd // 2
        s = jnp.dot(k_ref[j, :, 2 * A_HEAD_DIM * pair:2 * A_HEAD_DIM * (pair + 1)], qz_ref[hd],
                    preferred_element_type=F32)
        if with_bias:
            s = s + bias_ref[hd, jnp.minimum(i - j, 2)]
        s = s.astype(BF16) + mask_ref[...]
        s_ref[slot, hd] = s
        parts = [s.reshape(slabs)[r] for r in range(slabs[0])]
        while len(parts) > 1:
            parts = [jnp.maximum(a, b) for a, b in zip(parts[::2], parts[1::2])]
        cm_ref[slot, hd] = jnp.max(parts[0].astype(F32), axis=0, keepdims=True)

    def consume(slot, j, hd):
        m_old = m_ref[hd]
        m_new = jnp.maximum(m_old, cm_ref[slot, hd])
        m_ref[hd] = m_new
        alpha = jnp.exp2(m_old - m_new)
        m_tile = jnp.broadcast_to(m_new, (BF16_SUBLANES, t)).astype(BF16)
        p = jnp.exp2(s_ref[slot, hd].reshape(slabs) - m_tile[None]).reshape(t, t)
        pv = jnp.dot(vt_ref[j, hd * PV_ROWS:(hd + 1) * PV_ROWS, :], p, preferred_element_type=F32)
        acc_ref[hd] = alpha * acc_ref[hd] + pv

    def half_step(with_bias, slot, j, ties):
        ties = select(j, ties)
        for hd in range(A_HEADS):
            scores(with_bias, slot, j, hd)
            consume(1 - slot, j - 1, hd)
        return ties

    def step(with_bias, j, ties_before):
        return lax.cond(lax.rem(j, 2) == 1, functools.partial(half_step, with_bias, 1, j),
                        functools.partial(half_step, with_bias, 0, j), ties_before)

    def far_pair(pair, ties):
        ties = half_step(False, 1, 2 * pair + 1, ties)
        return half_step(False, 0, 2 * pair + 2, ties)

    ties = select(0, jnp.zeros((1, t), F32))
    for hd in range(A_HEADS):
        scores(True, 0, 0, hd)
    first_near = jnp.maximum(i - 1, 1)
    far_pairs = lax.shift_right_logical(first_near - 1, 1)
    ties = lax.fori_loop(0, far_pairs, far_pair, ties)
    ties = lax.fori_loop(2 * far_pairs + 1, first_near, functools.partial(step, False), ties)
    lax.fori_loop(first_near, nk, functools.partial(step, True), ties)
    for slot in range(2):
        @pl.when(lax.rem(i, 2) == slot)
        def _():
            for hd in range(A_HEADS):
                consume(slot, i, hd)

    outs = []
    for hd in range(A_HEADS):
        outs.append(acc_ref[hd, :A_HEAD_DIM, :] / acc_ref[hd, A_HEAD_DIM:A_HEAD_DIM + 1, :])
    o_ref[...] = jnp.concatenate(outs, axis=0).T.astype(BF16)


def _dsa_attention(qt, vt, iqt, iwt, k, ik, bias):
    b, nchunk, t, _ = k.shape
    s = nchunk * t
    return pl.pallas_call(
        _dsa_kernel,
        out_shape=jax.ShapeDtypeStruct((b, s, A_WIDTH), BF16),
        grid=(b, nchunk),
        in_specs=[
            pl.BlockSpec((None, None, A_WIDTH, t), lambda bi, i: (bi, i, 0, 0)),
            pl.BlockSpec((None, None, IDX_HEADS * IDX_DIM, t), lambda bi, i: (bi, i, 0, 0)),
            pl.BlockSpec((None, None, IDX_HEADS, t), lambda bi, i: (bi, i, 0, 0)),
            pl.BlockSpec((None, nchunk, t, A_WIDTH), lambda bi, i: (bi, 0, 0, 0)),
            pl.BlockSpec((None, nchunk, A_HEADS * PV_ROWS, t), lambda bi, i: (bi, 0, 0, 0)),
            pl.BlockSpec((None, nchunk, t, IDX_DIM), lambda bi, i: (bi, 0, 0, 0)),
            pl.BlockSpec(bias.shape, lambda bi, i: (0, 0, 0, 0)),
        ],
        out_specs=pl.BlockSpec((None, t, A_WIDTH), lambda bi, i: (bi, i, 0)),
        scratch_shapes=[
            pltpu.VMEM((nchunk, t, t), F32),
            pltpu.VMEM((WORD, nchunk, t // WORD, t), I32),
            pltpu.VMEM((nchunk, t // WORD, t), I32),
            pltpu.VMEM((A_HEADS, PV_ROWS, t), F32),
            pltpu.VMEM((A_HEADS, 1, t), F32),
            pltpu.VMEM((A_HEADS, 2 * A_HEAD_DIM, t), BF16),
            pltpu.VMEM((t, t), BF16),
            pltpu.VMEM((2, A_HEADS, t, t), BF16),
            pltpu.VMEM((2, A_HEADS, 1, t), F32),
        ],
        compiler_params=_params("parallel", "arbitrary"),
        name="dsa_attention",
    )(qt, iqt, iwt, k, vt, ik, bias)


def _halves(n):
    return (slice(0, n // 2), slice(n // 2, n))


def _xattn_block(parts, gpre_ref, gpost_ref, wq_ref, kt_ref, v_ref, wo_ref):
    heads = [slice(hd * X_HEAD_DIM, (hd + 1) * X_HEAD_DIM) for hd in range(X_HEADS)]
    hs = [_rms(x, gpre_ref[...]).astype(BF16) for x in parts]
    qs = [jnp.dot(h, wq_ref[...], preferred_element_type=F32) for h in hs]
    logits = [[jnp.dot(q[:, cols].astype(BF16), kt_ref[cols, :], preferred_element_type=F32)
               * (X_HEAD_DIM ** -0.5) for cols in heads] for q in qs]
    os = []
    for part_logits in logits:
        outs = []
        for cols, lg in zip(heads, part_logits):
            p = jnp.exp(lg - jnp.max(lg, axis=-1, keepdims=True))
            den = jnp.sum(p, axis=-1, keepdims=True)
            outs.append(jnp.dot(p.astype(BF16), v_ref[:, cols], preferred_element_type=F32) / den)
        os.append(jnp.concatenate(outs, axis=-1).astype(BF16))
    ys = [jnp.dot(o, wo_ref[...], preferred_element_type=F32) for o in os]
    return [x + _rms(y, gpost_ref[...]) for x, y in zip(parts, ys)]


def _xattn_operands(d, m, g_pre, g_post, wq, kt, v, wo):
    const = lambda shape: pl.BlockSpec(shape, lambda bi, i: (0, 0))
    specs = [const((1, d)), const((1, d)), const((d, X_WIDTH)),
             pl.BlockSpec((None, X_WIDTH, m), lambda bi, i: (bi, 0, 0)),
             pl.BlockSpec((None, m, X_WIDTH), lambda bi, i: (bi, 0, 0)),
             const((X_WIDTH, d))]
    return specs, (g_pre.reshape(1, d), g_post.reshape(1, d), wq.astype(BF16), kt, v, wo.astype(BF16))


def _mix_out_xattn_kernel(x_ref, a_ref, c_ref, wa_ref, wc_ref, g_ref, *rest):
    *xattn_refs, o_ref = rest
    halves = _halves(x_ref.shape[0])
    ys = [jnp.dot(a_ref[rows, :], wa_ref[...], preferred_element_type=F32)
          + jnp.dot(c_ref[rows, :], wc_ref[...], preferred_element_type=F32) for rows in halves]
    x1 = [x_ref[rows, :] + _rms(y, g_ref[...]) for rows, y in zip(halves, ys)]
    for rows, out in zip(halves, _xattn_block(x1, *xattn_refs)):
        o_ref[rows, :] = out


def _mix_out_xattn(x, a, c, w_out, g_post, xattn_args):
    b, s, d = x.shape
    t = ROW_TILE
    wb = w_out.astype(BF16)
    row = lambda width: pl.BlockSpec((None, t, width), lambda bi, i: (bi, i, 0))
    const = lambda shape: pl.BlockSpec(shape, lambda bi, i: (0, 0))
    xspecs, xarrays = _xattn_operands(d, xattn_args[3].shape[-1], *xattn_args)
    return pl.pallas_call(
        _mix_out_xattn_kernel,
        out_shape=jax.ShapeDtypeStruct((b, s, d), F32),
        grid=(b, s // t),
        in_specs=[row(d), row(A_WIDTH), row(B_WIDTH), const((A_WIDTH, d)), const((B_WIDTH, d)),
                  const((1, d))] + xspecs,
        out_specs=row(d),
        compiler_params=_params("parallel", "parallel"),
        name="mix_out_xattn",
    )(x, a, c, wb[:A_WIDTH], wb[A_WIDTH:], g_post.reshape(1, d), *xarrays)


def _pool_xattn_kernel(x_ref, xprev_ref, gpre_ref, gpost_ref, w_ref, scale_ref, *rest):
    *xattn_refs, o_ref = rest
    i = pl.program_id(1)
    x = x_ref[...]
    t, d = x.shape
    gpre = gpre_ref[...]
    h = _rms(x, gpre)
    hp = _rms(xprev_ref[...], gpre) * jnp.where(i > 0, 1.0, 0.0)
    ext = jnp.concatenate([hp, h], axis=0)
    group = d // len(POOL_WINDOWS)
    pos = i * t + lax.broadcasted_iota(I32, (t, group), 0)
    pooled = []
    for gi, win in enumerate(POOL_WINDOWS):
        sums = ext[:, gi * group:(gi + 1) * group]
        width = 1
        while width < win:
            sums = sums[width:] + sums[:-width]
            width *= 2
        wsum = sums[POOL_HALO - (win - 1):POOL_HALO - (win - 1) + t]
        cnt = jnp.minimum(pos + 1, win).astype(F32)
        pooled.append((wsum / cnt - h[:, gi * group:(gi + 1) * group]).astype(BF16))
    halves = _halves(t)
    x1 = []
    for rows in halves:
        y = jnp.concatenate([jnp.dot(pg[rows], w_ref[gi], preferred_element_type=F32)
                             for gi, pg in enumerate(pooled)], axis=-1) * scale_ref[...]
        x1.append(x[rows] + _rms(y, gpost_ref[...]))
    for rows, out in zip(halves, _xattn_block(x1, *xattn_refs)):
        o_ref[rows, :] = out


def _pool_xattn(x, g_pre, g_post, pool_w, pool_scale, xattn_args):
    b, s, d = x.shape
    t = ROW_TILE
    const = lambda shape: pl.BlockSpec(shape, lambda bi, i: (0,) * len(shape))
    prev = t // POOL_HALO
    xspecs, xarrays = _xattn_operands(d, xattn_args[3].shape[-1], *xattn_args)
    return pl.pallas_call(
        _pool_xattn_kernel,
        out_shape=jax.ShapeDtypeStruct((b, s, d), F32),
        grid=(b, s // t),
        in_specs=[
            pl.BlockSpec((None, t, d), lambda bi, i: (bi, i, 0)),
            pl.BlockSpec((None, POOL_HALO, d), lambda bi, i: (bi, jnp.maximum(i * prev - 1, 0), 0)),
            const((1, d)), const((1, d)), const(pool_w.shape), const((1, d)),
        ] + xspecs,
        out_specs=pl.BlockSpec((None, t, d), lambda bi, i: (bi, i, 0)),
        compiler_params=_params("parallel", "parallel"),
        name="pool_xattn",
    )(x, x, g_pre.reshape(1, d), g_post.reshape(1, d), pool_w.astype(BF16), pool_scale.reshape(1, d),
      *xarrays)


def _mem_kv_kernel(mem_ref, g_ref, wkt_ref, wv_ref, kt_ref, v_ref):
    mem_n = _rms(mem_ref[...], g_ref[...]).astype(BF16)
    kt = lax.dot_general(wkt_ref[...], mem_n, (((1,), (1,)), ((), ())), preferred_element_type=F32)
    kt_ref[...] = kt.astype(BF16)
    v_ref[...] = jnp.dot(mem_n, wv_ref[...], preferred_element_type=F32).astype(BF16)


def _mem_kv(mem, g, wk, wv):
    b, m, d = mem.shape
    const = lambda shape: pl.BlockSpec(shape, lambda bi: (0, 0))
    return pl.pallas_call(
        _mem_kv_kernel,
        out_shape=(jax.ShapeDtypeStruct((b, X_WIDTH, m), BF16), jax.ShapeDtypeStruct((b, m, X_WIDTH), BF16)),
        grid=(b,),
        in_specs=[pl.BlockSpec((None, m, d), lambda bi: (bi, 0, 0)), const((1, d)), const((X_WIDTH, d)),
                  const((d, X_WIDTH))],
        out_specs=(pl.BlockSpec((None, X_WIDTH, m), lambda bi: (bi, 0, 0)),
                   pl.BlockSpec((None, m, X_WIDTH), lambda bi: (bi, 0, 0))),
        compiler_params=_params("parallel"),
        name="mem_kv",
    )(mem, g.reshape(1, d), wk.astype(BF16).T, wv.astype(BF16))


def kernel(x, mem, ffn_w_gate, ffn_w_up, ffn_w_down, ffn_norm_pre, ffn_norm_post, mix_norm_pre,
           mix_norm_post, even_w_in, even_conv_w, even_w_out, rel_bias, pool_w, pool_scale,
           xattn_norm_pre, xattn_mem_norm, xattn_norm_post, xattn_wq, xattn_wk, xattn_wv, xattn_wo):
    b, s, d = x.shape
    depth = ffn_w_gate.shape[0]
    assert s % ATT_TILE == 0 and s // 4 >= TOPK_MAX and (b * s) % FFN_ROWS == 0 and s % ROW_TILE == 0
    bias = _bias_tiles(rel_bias, s)

    ffn_weights = (ffn_w_gate.astype(BF16), ffn_w_up.astype(BF16), ffn_w_down.astype(BF16))

    def ffn(xx, layer, j):
        out = _ffn(xx.reshape(b * s, d), layer, j, ffn_norm_pre, ffn_norm_post, *ffn_weights)
        return out.reshape(b, s, d)

    for layer in range(depth):
        x = ffn(x, layer, 0)
        kt, v = _mem_kv(mem, xattn_mem_norm[layer], xattn_wk[layer], xattn_wv[layer])
        xattn_args = (xattn_norm_pre[layer], xattn_norm_post[layer], xattn_wq[layer], kt, v,
                      xattn_wo[layer])
        if layer % 2 == 0:
            e = layer // 2
            qt, vt, iqt, iwt, k, ik, c = _proj_even(x, mix_norm_pre[layer], even_w_in[e], even_conv_w[e])
            a = _dsa_attention(qt, vt, iqt, iwt, k, ik, bias)
            x = _mix_out_xattn(x, a, c, even_w_out[e], mix_norm_post[layer], xattn_args)
        else:
            o = layer // 2
            x = _pool_xattn(x, mix_norm_pre[layer], mix_norm_post[layer], pool_w[o], pool_scale[o],
                            xattn_args)
        x = ffn(x, layer, 1)
    return x
```

```python
import functools
import math

import jax
import jax.numpy as jnp
import numpy as np
from jax import lax
from jax.experimental import pallas as pl
from jax.experimental.pallas import tpu as pltpu

F32 = jnp.float32
BF16 = jnp.bfloat16
I32 = jnp.int32

RMS_EPS = 1e-6

A_HEADS = 8
A_HEAD_DIM = 64
A_WIDTH = A_HEADS * A_HEAD_DIM
IDX_HEADS = 8
IDX_DIM = 64
TOPK_MAX = 256
REL_BUCKETS = 32
REL_MAX_EXACT = 16
REL_MAX_DIST = 128
B_WIDTH = 512
CONV_K = 3
POOL_WINDOWS = (2, 4, 8, 16)
POOL_HALO = 16
X_HEADS = 4
X_HEAD_DIM = 128
X_WIDTH = X_HEADS * X_HEAD_DIM

LANES = 128
SUBLANES = 8
MXU_DIM = 256
VMEM_BYTES_V7X = 64 * 1024 * 1024
VMEM_LIMIT = VMEM_BYTES_V7X - 8 * 1024 * 1024

ATT_TILE = MXU_DIM
FFN_ROWS = 1024
FFN_COLS = MXU_DIM
FFN_LOAD_CHUNKS = 8
ROW_TILE = 512

BF16_SUBLANES = 16
PV_ROWS = A_HEAD_DIM + BF16_SUBLANES
LOG2E = math.log2(math.e)
MASKED = -1e30
KEY_MIN = -(2 ** 31)
WORD = 32


def _t5_bucket_table(n):
    d = np.arange(n)
    nf = np.maximum(d, 1).astype(np.float32)
    ratio = np.log(nf / np.float32(REL_MAX_EXACT)) / np.float32(math.log(REL_MAX_DIST / REL_MAX_EXACT))
    large = REL_MAX_EXACT + (ratio * np.float32(REL_BUCKETS - REL_MAX_EXACT)).astype(np.int32)
    large = np.minimum(large, REL_BUCKETS - 1)
    return np.where(d < REL_MAX_EXACT, d, large)


def _rms(x, g):
    ms = jnp.mean(x * x, axis=-1, keepdims=True)
    return x * lax.rsqrt(ms + RMS_EPS) * g


def _params(*semantics):
    return pltpu.CompilerParams(dimension_semantics=semantics, vmem_limit_bytes=VMEM_LIMIT)


def _ffn_kernel(layer, j, x_ref, gpre_ref, gpost_ref, wg_hbm, wu_hbm, wd_hbm, o_ref,
                h_ref, a_ref, wg_ref, wu_ref, wd_ref, stage_in_ref, stage_out_ref, sem_in, sem_out):
    @pl.when(pl.program_id(0) == 0)
    def _():
        plan = []
        for hbm, dst, stage, sems in ((wg_hbm, wg_ref, stage_in_ref, sem_in), (wu_hbm, wu_ref, stage_in_ref, sem_in),
                                      (wd_hbm, wd_ref, stage_out_ref, sem_out)):
            rows = stage.shape[1]
            for c in range(dst.shape[0] // rows):
                plan.append((hbm.at[layer, j, pl.ds(c * rows, rows), :], stage, sems, dst.at[pl.ds(c * rows, rows), :]))

        def copy(k):
            src, stage, sems, _ = plan[k]
            return pltpu.make_async_copy(src, stage.at[k % 2], sems.at[k % 2])

        copy(0).start()
        for k in range(len(plan)):
            if k + 1 < len(plan):
                copy(k + 1).start()
            copy(k).wait()
            plan[k][3][...] = plan[k][1][k % 2].astype(BF16)

    for rows in _halves(x_ref.shape[0]):
        x = x_ref[rows, :]
        h_ref[rows, :] = _rms(x, gpre_ref[...]).astype(BF16)
        for f in range(a_ref.shape[1] // FFN_COLS):
            cols = slice(f * FFN_COLS, (f + 1) * FFN_COLS)
            g = jnp.dot(h_ref[rows, :], wg_ref[:, cols], preferred_element_type=F32)
            u = jnp.dot(h_ref[rows, :], wu_ref[:, cols], preferred_element_type=F32)
            a_ref[rows, cols] = (g * jax.nn.sigmoid(g) * u).astype(BF16)
        y = jnp.dot(a_ref[rows, :], wd_ref[...], preferred_element_type=F32)
        o_ref[rows, :] = x + 0.5 * _rms(y, gpost_ref[...])


def _ffn(x, layer, j, g_pre, g_post, w_gate, w_up, w_down):
    n, d = x.shape
    dff = w_gate.shape[-1]
    assert dff % FFN_COLS == 0 and (d // FFN_LOAD_CHUNKS) % BF16_SUBLANES == 0
    assert (dff // FFN_LOAD_CHUNKS) % BF16_SUBLANES == 0
    pick = lambda rows, cols: pl.BlockSpec((None, None, rows, cols), lambda i: (layer, j, 0, 0))
    hbm = pl.BlockSpec(memory_space=pl.ANY)
    depth = g_pre.shape[0]
    return pl.pallas_call(
        functools.partial(_ffn_kernel, layer, j),
        out_shape=jax.ShapeDtypeStruct((n, d), F32),
        grid=(n // FFN_ROWS,),
        in_specs=[pl.BlockSpec((FFN_ROWS, d), lambda i: (i, 0)), pick(1, d), pick(1, d), hbm, hbm, hbm],
        out_specs=pl.BlockSpec((FFN_ROWS, d), lambda i: (i, 0)),
        scratch_shapes=[
            pltpu.VMEM((FFN_ROWS, d), BF16), pltpu.VMEM((FFN_ROWS, dff), BF16),
            pltpu.VMEM((d, dff), BF16), pltpu.VMEM((d, dff), BF16), pltpu.VMEM((dff, d), BF16),
            pltpu.VMEM((2, d // FFN_LOAD_CHUNKS, dff), F32), pltpu.VMEM((2, dff // FFN_LOAD_CHUNKS, d), F32),
            pltpu.SemaphoreType.DMA((2,)), pltpu.SemaphoreType.DMA((2,)),
        ],
        compiler_params=_params("arbitrary"),
        name="ffn",
    )(x, g_pre.reshape(depth, 2, 1, d), g_post.reshape(depth, 2, 1, d), w_gate, w_up, w_down)


def _proj_kernel(x_ref, g_ref, wk_ref, wik_ref, wugg_ref, wt_ref,
                 convw_ref, qt_ref, vt_ref, iqt_ref, iwt_ref, k_ref, ik_ref, c_ref, carry_ref):
    @pl.when(pl.program_id(1) == 0)
    def _():
        carry_ref[...] = jnp.zeros_like(carry_ref)

    h = _rms(x_ref[...], g_ref[...]).astype(BF16)
    ugg = jnp.dot(h, wugg_ref[...], preferred_element_type=F32)
    u = ugg[:, :B_WIDTH]
    gb = ugg[:, B_WIDTH:2 * B_WIDTH]
    gc = ugg[:, 2 * B_WIDTH:]
    gu = gc * u
    t = gu.shape[0]
    ext = jnp.concatenate([carry_ref[...], gu], axis=0)
    w = convw_ref[...]
    y = w[2:3] * gu
    for j in range(CONV_K - 1):
        lo = SUBLANES - (CONV_K - 1) + j
        y = y + w[j:j + 1] * ext[lo:lo + t]
    c_ref[...] = (gb * y).astype(BF16)
    carry_ref[...] = gu[t - SUBLANES:]

    tr = lax.dot_general(wt_ref[...], h, (((1,), (1,)), ((), ())), preferred_element_type=F32)
    qt = (tr[:A_WIDTH] * (A_HEAD_DIM ** -0.5 * LOG2E)).astype(BF16)
    vt = tr[A_WIDTH:2 * A_WIDTH].astype(BF16)
    iq_end = 2 * A_WIDTH + IDX_HEADS * IDX_DIM
    iqt = tr[2 * A_WIDTH:iq_end].astype(BF16)
    iwt = tr[iq_end:iq_end + IDX_HEADS] * (IDX_HEADS ** -0.5 * IDX_DIM ** -0.5)
    k = jnp.dot(h, wk_ref[...], preferred_element_type=F32).astype(BF16)
    ik = jnp.dot(h, wik_ref[...], preferred_element_type=F32).astype(BF16)
    tile = qt_ref.shape[-1]
    ones = jnp.ones((PV_ROWS - A_HEAD_DIM, tile), BF16)
    for c in range(t // tile):
        tok = slice(c * tile, (c + 1) * tile)
        qt_ref[c] = qt[:, tok]
        iqt_ref[c] = iqt[:, tok]
        iwt_ref[c] = iwt[:, tok]
        k_ref[c] = k[tok]
        ik_ref[c] = ik[tok]
        for hd in range(A_HEADS):
            vt_ref[c, hd * PV_ROWS:hd * PV_ROWS + A_HEAD_DIM, :] = vt[hd * A_HEAD_DIM:(hd + 1) * A_HEAD_DIM, tok]
            vt_ref[c, hd * PV_ROWS + A_HEAD_DIM:(hd + 1) * PV_ROWS, :] = ones


def _proj_even(x, g, w_in, conv_w):
    b, s, d = x.shape
    t = ATT_TILE
    nchunk = s // t
    per = ROW_TILE // t
    q_end, k_end, v_end = A_WIDTH, 2 * A_WIDTH, 3 * A_WIDTH
    iq_end = v_end + IDX_HEADS * IDX_DIM
    ik_end = iq_end + IDX_DIM
    iw_end = ik_end + IDX_HEADS
    wb = w_in.astype(BF16)
    wk = wb[:, q_end:k_end]
    wik = wb[:, iq_end:ik_end]
    wugg = wb[:, iw_end:]
    wt = jnp.concatenate([wb[:, :q_end], wb[:, k_end:v_end], wb[:, v_end:iq_end], wb[:, ik_end:iw_end]], axis=1).T
    wt = jnp.pad(wt, ((0, BF16_SUBLANES - IDX_HEADS), (0, 0)))
    const = lambda shape: pl.BlockSpec(shape, lambda bi, i: (0,) * len(shape))
    return pl.pallas_call(
        _proj_kernel,
        out_shape=(
            jax.ShapeDtypeStruct((b, nchunk, A_WIDTH, t), BF16),
            jax.ShapeDtypeStruct((b, nchunk, A_HEADS * PV_ROWS, t), BF16),
            jax.ShapeDtypeStruct((b, nchunk, IDX_HEADS * IDX_DIM, t), BF16),
            jax.ShapeDtypeStruct((b, nchunk, IDX_HEADS, t), F32),
            jax.ShapeDtypeStruct((b, nchunk, t, A_WIDTH), BF16),
            jax.ShapeDtypeStruct((b, nchunk, t, IDX_DIM), BF16),
            jax.ShapeDtypeStruct((b, s, B_WIDTH), BF16),
        ),
        grid=(b, nchunk // per),
        in_specs=[
            pl.BlockSpec((None, per * t, d), lambda bi, i: (bi, i, 0)),
            const((1, d)),
            const(wk.shape), const(wik.shape), const(wugg.shape), const(wt.shape), const(conv_w.shape),
        ],
        out_specs=(
            pl.BlockSpec((None, per, A_WIDTH, t), lambda bi, i: (bi, i, 0, 0)),
            pl.BlockSpec((None, per, A_HEADS * PV_ROWS, t), lambda bi, i: (bi, i, 0, 0)),
            pl.BlockSpec((None, per, IDX_HEADS * IDX_DIM, t), lambda bi, i: (bi, i, 0, 0)),
            pl.BlockSpec((None, per, IDX_HEADS, t), lambda bi, i: (bi, i, 0, 0)),
            pl.BlockSpec((None, per, t, A_WIDTH), lambda bi, i: (bi, i, 0, 0)),
            pl.BlockSpec((None, per, t, IDX_DIM), lambda bi, i: (bi, i, 0, 0)),
            pl.BlockSpec((None, per * t, B_WIDTH), lambda bi, i: (bi, i, 0)),
        ),
        scratch_shapes=[pltpu.VMEM((SUBLANES, B_WIDTH), F32)],
        compiler_params=_params("arbitrary", "arbitrary"),
        name="proj_even",
    )(x, g.reshape(1, d), wk, wik, wugg, wt, conv_w)


def _bias_kernel(rb_ref, o_ref, *, thresholds, far_bucket):
    hd = pl.program_id(0)
    t = o_ref.shape[-1]
    r = lax.broadcasted_iota(I32, (t, t), 1)
    c = lax.broadcasted_iota(I32, (t, t), 0)
    for m in range(2):
        dist = t * m + r - c
        val = jnp.full((t, t), rb_ref[0, hd], F32)
        for bucket in range(1, REL_BUCKETS):
            val = jnp.where(dist >= thresholds[bucket], rb_ref[bucket, hd], val)
        val = (val - rb_ref[far_bucket, hd]) * LOG2E
        if m == 0:
            val = jnp.where(dist < 0, MASKED, val)
        o_ref[m] = val
    o_ref[2] = jnp.zeros((t, t), F32)


def _bias_tiles(rel_bias, seq):
    t = ATT_TILE
    table = _t5_bucket_table(seq)
    assert np.all(np.diff(table) >= 0) and table[0] == 0 and table.max() == REL_BUCKETS - 1
    thresholds = tuple(int(np.argmax(table >= bkt)) for bkt in range(REL_BUCKETS))
    far_bucket = int(table[2 * t - 1])
    assert np.all(table[2 * t - 1:] == far_bucket)
    return pl.pallas_call(
        functools.partial(_bias_kernel, thresholds=thresholds, far_bucket=far_bucket),
        out_shape=jax.ShapeDtypeStruct((A_HEADS, 3, t, t), F32),
        grid=(A_HEADS,),
        in_specs=[pl.BlockSpec(memory_space=pltpu.SMEM)],
        out_specs=pl.BlockSpec((None, 3, t, t), lambda h: (h, 0, 0, 0)),
        compiler_params=_params("parallel"),
        name="rel_bias_tiles",
    )(rel_bias)


def _bit_transpose32(words):
    a = list(words)
    shift, mask = WORD // 2, 0x0000FFFF
    while shift:
        k = 0
        while k < WORD:
            swap = (a[k] ^ lax.shift_right_logical(a[k + shift], jnp.int32(shift))) & mask
            a[k] = a[k] ^ swap
            a[k + shift] = a[k + shift] ^ (swap << shift)
            k = (k + shift + 1) & ~shift
        shift //= 2
        mask ^= (mask << shift) & 0xFFFFFFFF
    return a


def _dsa_kernel(qt_ref, iqt_ref, iwt_ref, k_ref, vt_ref, ik_ref, bias_ref, o_ref,
                sc_ref, plane_ref, eq_ref, acc_ref, m_ref, qz_ref, mask_ref, s_ref, cm_ref):
    i = pl.program_id(1)
    t = ATT_TILE
    nk = i + 1
    iw = iwt_ref[...]

    def index_scores(j):
        ikc = ik_ref[j]
        sc = jnp.zeros((t, t), F32)
        for hd in range(IDX_HEADS):
            d = jnp.dot(ikc, iqt_ref[hd * IDX_DIM:(hd + 1) * IDX_DIM, :],
                        preferred_element_type=F32)
            sc = sc + jnp.maximum(d, 0.0) * iw[hd:hd + 1, :]
        return sc

    def store_scores(j, sc):
        sc_ref[j] = sc
        bits = pltpu.bitcast(sc + 0.0, I32)
        ukeys = bits ^ ((bits >> 31) | KEY_MIN)
        for col in range(t // LANES):
            lanes = slice(col * LANES, (col + 1) * LANES)
            planes = _bit_transpose32([ukeys[v * SUBLANES:(v + 1) * SUBLANES, lanes] for v in range(WORD)])
            for r in range(WORD):
                plane_ref[r, j, :, lanes] = planes[r]

    @pl.when(i == 0)
    def _():
        plane_ref[...] = jnp.zeros(plane_ref.shape, I32)

    def phase_a(pair, carry):
        store_scores(2 * pair, index_scores(2 * pair))
        store_scores(2 * pair + 1, index_scores(2 * pair + 1))
        return carry

    lax.fori_loop(0, lax.shift_right_logical(i, 1), phase_a, 0)

    @pl.when((i & 1) == 1)
    def _():
        store_scores(i - 1, index_scores(i - 1))

    qpos = lax.broadcasted_iota(I32, (t, t), 1)
    kpos = lax.broadcasted_iota(I32, (t, t), 0)
    store_scores(i, jnp.where(kpos <= qpos, index_scores(i), -jnp.inf))

    nchunk = eq_ref.shape[0]
    live = lax.broadcasted_iota(I32, eq_ref.shape, 0) < nk
    eq_ref[...] = jnp.where(live, jnp.int32(-1), jnp.int32(0))

    def search(span):
        def search_bit(r, carry):
            thr_u, above = carry
            cnt = jnp.zeros((SUBLANES, t), I32)
            for c in range(span):
                cnt = cnt + lax.population_count(eq_ref[c] & plane_ref[r, c])
            cnt = jnp.sum(cnt.astype(F32), axis=0, keepdims=True)
            take = above + cnt >= float(TOPK_MAX)
            for c in range(span):
                eq = eq_ref[c]
                with_bit = eq & plane_ref[r, c]
                eq_ref[c] = jnp.where(take, with_bit, eq ^ with_bit)
            bit = jnp.left_shift(jnp.int32(1), WORD - 1 - r)
            return jnp.where(take, thr_u | bit, thr_u), jnp.where(take, above, above + cnt)

        return lambda: lax.fori_loop(0, WORD, search_bit,
                                     (jnp.zeros((1, t), I32), jnp.zeros((1, t), F32)))

    quarter = nchunk // 4
    thr_u, _ = lax.cond(
        nk <= 2 * quarter,
        lambda: lax.cond(nk <= quarter, search(quarter), search(2 * quarter)),
        lambda: lax.cond(nk <= 3 * quarter, search(3 * quarter), search(nchunk)))

    def key_to_score(ukey):
        return pltpu.bitcast(ukey ^ (~(ukey >> 31) | KEY_MIN), F32)

    def count_scores(*preds):
        def body(j, cnts):
            sc = sc_ref[j]
            return tuple(cnt + jnp.sum(jnp.where(pred(sc), 1.0, 0.0).reshape(t // SUBLANES, SUBLANES, t), axis=0)
                         for cnt, pred in zip(cnts, preds))
        cnts = lax.fori_loop(0, nk, body, tuple(jnp.zeros((SUBLANES, t), F32) for _ in preds))
        return [jnp.sum(cnt, axis=0, keepdims=True) for cnt in cnts]

    thr_fast = key_to_score(thr_u)
    reach, exceed = count_scores(lambda sc: sc >= thr_fast, lambda sc: sc > thr_fast)
    good = jnp.logical_and(exceed < float(TOPK_MAX), reach >= float(TOPK_MAX))

    def search_by_value():
        def step(r, ukey):
            cand = ukey | jnp.left_shift(jnp.int32(1), WORD - 1 - r)
            cand_score = key_to_score(cand)
            reach, = count_scores(lambda sc: sc >= cand_score)
            return jnp.where(reach >= float(TOPK_MAX), cand, ukey)
        ukey = lax.fori_loop(0, WORD, step, jnp.zeros((1, t), I32))
        thr_slow = jnp.where(ukey == 0, -jnp.inf, key_to_score(ukey))
        exceed, = count_scores(lambda sc: sc > thr_slow)
        return thr_slow, exceed

    thr, exceed = lax.cond(jnp.min(jnp.where(good, 1.0, 0.0)) > 0.5,
                           lambda: (thr_fast, exceed), search_by_value)
    need = float(TOPK_MAX) - exceed

    m_ref[...] = jnp.full(m_ref.shape, MASKED, F32)
    acc_ref[...] = jnp.zeros(acc_ref.shape, F32)
    row = lax.broadcasted_iota(I32, (2 * A_HEAD_DIM, t), 0)
    for hd in range(A_HEADS):
        pair = hd // 2
        blk = qt_ref[2 * A_HEAD_DIM * pair:2 * A_HEAD_DIM * (pair + 1), :]
        own = (row >= A_HEAD_DIM) if hd % 2 else (row < A_HEAD_DIM)
        qz_ref[hd] = jnp.where(own, blk, jnp.zeros_like(blk))
    tri = (lax.broadcasted_iota(I32, (t, t), 0) >= lax.broadcasted_iota(I32, (t, t), 1)).astype(BF16)

    def select(j, ties_before):
        kk = sc_ref[j]
        is_tie = kk == thr
        tie_rank = jnp.dot(tri, jnp.where(is_tie, 1.0, 0.0).astype(BF16),
                           preferred_element_type=F32) + ties_before
        tie_add = jnp.where(is_tie, jnp.where(tie_rank <= need, 0.0, MASKED), MASKED)
        mask_ref[...] = jnp.where(kk > thr, 0.0, tie_add).astype(BF16)
        return tie_rank[t - 1:t, :]

    slabs = (t // BF16_SUBLANES, BF16_SUBLANES, t)

    def scores(with_bias, slot, j, hd):
        pair = hd // 2
        s = jnp.dot(k_ref[j, :, 2 * A_HEAD_DIM * pair:2 * A_HEAD_DIM * (pair + 1)], qz_ref[hd],
                    preferred_element_type=F32)
        if with_bias:
            s = s + bias_ref[hd, jnp.minimum(i - j, 2)]
        s = s.astype(BF16) + mask_ref[...]
        s_ref[slot, hd] = s
        parts = [s.reshape(slabs)[r] for r in range(slabs[0])]
        while len(parts) > 1:
            parts = [jnp.maximum(a, b) for a, b in zip(parts[::2], parts[1::2])]
        cm_ref[slot, hd] = jnp.max(parts[0].astype(F32), axis=0, keepdims=True)

    def consume(slot, j, hd):
        m_old = m_ref[hd]
        m_new = jnp.maximum(m_old, cm_ref[slot, hd])
        m_ref[hd] = m_new
        alpha = jnp.exp2(m_old - m_new)
        m_tile = jnp.broadcast_to(m_new, (BF16_SUBLANES, t)).astype(BF16)
        p = jnp.exp2(s_ref[slot, hd].reshape(slabs) - m_tile[None]).reshape(t, t)
        pv = jnp.dot(vt_ref[j, hd * PV_ROWS:(hd + 1) * PV_ROWS, :], p, preferred_element_type=F32)
        acc_ref[hd] = alpha * acc_ref[hd] + pv

    def half_step(with_bias, slot, j, ties):
        ties = select(j, ties)
        for hd in range(A_HEADS):
            scores(with_bias, slot, j, hd)
            consume(1 - slot, j - 1, hd)
        return ties

    def step(with_bias, j, ties_before):
        return lax.cond(lax.rem(j, 2) == 1, functools.partial(half_step, with_bias, 1, j),
                        functools.partial(half_step, with_bias, 0, j), ties_before)

    def far_pair(pair, ties):
        ties = half_step(False, 1, 2 * pair + 1, ties)
        return half_step(False, 0, 2 * pair + 2, ties)

    ties = select(0, jnp.zeros((1, t), F32))
    for hd in range(A_HEADS):
        scores(True, 0, 0, hd)
    first_near = jnp.maximum(i - 1, 1)
    far_pairs = lax.shift_right_logical(first_near - 1, 1)
    ties = lax.fori_loop(0, far_pairs, far_pair, ties)
    ties = lax.fori_loop(2 * far_pairs + 1, first_near, functools.partial(step, False), ties)
    lax.fori_loop(first_near, nk, functools.partial(step, True), ties)
    for slot in range(2):
        @pl.when(lax.rem(i, 2) == slot)
        def _():
            for hd in range(A_HEADS):
                consume(slot, i, hd)

    outs = []
    for hd in range(A_HEADS):
        outs.append(acc_ref[hd, :A_HEAD_DIM, :] / acc_ref[hd, A_HEAD_DIM:A_HEAD_DIM + 1, :])
    o_ref[...] = jnp.concatenate(outs, axis=0).T.astype(BF16)


def _dsa_attention(qt, vt, iqt, iwt, k, ik, bias):
    b, nchunk, t, _ = k.shape
    s = nchunk * t
    return pl.pallas_call(
        _dsa_kernel,
        out_shape=jax.ShapeDtypeStruct((b, s, A_WIDTH), BF16),
        grid=(b, nchunk),
        in_specs=[
            pl.BlockSpec((None, None, A_WIDTH, t), lambda bi, i: (bi, i, 0, 0)),
            pl.BlockSpec((None, None, IDX_HEADS * IDX_DIM, t), lambda bi, i: (bi, i, 0, 0)),
            pl.BlockSpec((None, None, IDX_HEADS, t), lambda bi, i: (bi, i, 0, 0)),
            pl.BlockSpec((None, nchunk, t, A_WIDTH), lambda bi, i: (bi, 0, 0, 0)),
            pl.BlockSpec((None, nchunk, A_HEADS * PV_ROWS, t), lambda bi, i: (bi, 0, 0, 0)),
            pl.BlockSpec((None, nchunk, t, IDX_DIM), lambda bi, i: (bi, 0, 0, 0)),
            pl.BlockSpec(bias.shape, lambda bi, i: (0, 0, 0, 0)),
        ],
        out_specs=pl.BlockSpec((None, t, A_WIDTH), lambda bi, i: (bi, i, 0)),
        scratch_shapes=[
            pltpu.VMEM((nchunk, t, t), F32),
            pltpu.VMEM((WORD, nchunk, t // WORD, t), I32),
            pltpu.VMEM((nchunk, t // WORD, t), I32),
            pltpu.VMEM((A_HEADS, PV_ROWS, t), F32),
            pltpu.VMEM((A_HEADS, 1, t), F32),
            pltpu.VMEM((A_HEADS, 2 * A_HEAD_DIM, t), BF16),
            pltpu.VMEM((t, t), BF16),
            pltpu.VMEM((2, A_HEADS, t, t), BF16),
            pltpu.VMEM((2, A_HEADS, 1, t), F32),
        ],
        compiler_params=_params("parallel", "arbitrary"),
        name="dsa_attention",
    )(qt, iqt, iwt, k, vt, ik, bias)


def _halves(n):
    return (slice(0, n // 2), slice(n // 2, n))


def _xattn_block(parts, gpre_ref, gpost_ref, wq_ref, kt_ref, v_ref, wo_ref):
    heads = [slice(hd * X_HEAD_DIM, (hd + 1) * X_HEAD_DIM) for hd in range(X_HEADS)]
    hs = [_rms(x, gpre_ref[...]).astype(BF16) for x in parts]
    qs = [jnp.dot(h, wq_ref[...], preferred_element_type=F32) for h in hs]
    logits = [[jnp.dot(q[:, cols].astype(BF16), kt_ref[cols, :], preferred_element_type=F32)
               * (X_HEAD_DIM ** -0.5) for cols in heads] for q in qs]
    os = []
    for part_logits in logits:
        outs = []
        for cols, lg in zip(heads, part_logits):
            p = jnp.exp(lg - jnp.max(lg, axis=-1, keepdims=True))
            den = jnp.sum(p, axis=-1, keepdims=True)
            outs.append(jnp.dot(p.astype(BF16), v_ref[:, cols], preferred_element_type=F32) / den)
        os.append(jnp.concatenate(outs, axis=-1).astype(BF16))
    ys = [jnp.dot(o, wo_ref[...], preferred_element_type=F32) for o in os]
    return [x + _rms(y, gpost_ref[...]) for x, y in zip(parts, ys)]


def _xattn_operands(d, m, g_pre, g_post, wq, kt, v, wo):
    const = lambda shape: pl.BlockSpec(shape, lambda bi, i: (0, 0))
    specs = [const((1, d)), const((1, d)), const((d, X_WIDTH)),
             pl.BlockSpec((None, X_WIDTH, m), lambda bi, i: (bi, 0, 0)),
             pl.BlockSpec((None, m, X_WIDTH), lambda bi, i: (bi, 0, 0)),
             const((X_WIDTH, d))]
    return specs, (g_pre.reshape(1, d), g_post.reshape(1, d), wq.astype(BF16), kt, v, wo.astype(BF16))


def _mix_out_xattn_kernel(x_ref, a_ref, c_ref, wa_ref, wc_ref, g_ref, *rest):
    *xattn_refs, o_ref = rest
    halves = _halves(x_ref.shape[0])
    ys = [jnp.dot(a_ref[rows, :], wa_ref[...], preferred_element_type=F32)
          + jnp.dot(c_ref[rows, :], wc_ref[...], preferred_element_type=F32) for rows in halves]
    x1 = [x_ref[rows, :] + _rms(y, g_ref[...]) for rows, y in zip(halves, ys)]
    for rows, out in zip(halves, _xattn_block(x1, *xattn_refs)):
        o_ref[rows, :] = out


def _mix_out_xattn(x, a, c, w_out, g_post, xattn_args):
    b, s, d = x.shape
    t = ROW_TILE
    wb = w_out.astype(BF16)
    row = lambda width: pl.BlockSpec((None, t, width), lambda bi, i: (bi, i, 0))
    const = lambda shape: pl.BlockSpec(shape, lambda bi, i: (0, 0))
    xspecs, xarrays = _xattn_operands(d, xattn_args[3].shape[-1], *xattn_args)
    return pl.pallas_call(
        _mix_out_xattn_kernel,
        out_shape=jax.ShapeDtypeStruct((b, s, d), F32),
        grid=(b, s // t),
        in_specs=[row(d), row(A_WIDTH), row(B_WIDTH), const((A_WIDTH, d)), const((B_WIDTH, d)),
                  const((1, d))] + xspecs,
        out_specs=row(d),
        compiler_params=_params("parallel", "parallel"),
        name="mix_out_xattn",
    )(x, a, c, wb[:A_WIDTH], wb[A_WIDTH:], g_post.reshape(1, d), *xarrays)


def _pool_xattn_kernel(x_ref, xprev_ref, gpre_ref, gpost_ref, w_ref, scale_ref, *rest):
    *xattn_refs, o_ref = rest
    i = pl.program_id(1)
    x = x_ref[...]
    t, d = x.shape
    gpre = gpre_ref[...]
    h = _rms(x, gpre)
    hp = _rms(xprev_ref[...], gpre) * jnp.where(i > 0, 1.0, 0.0)
    ext = jnp.concatenate([hp, h], axis=0)
    group = d // len(POOL_WINDOWS)
    pos = i * t + lax.broadcasted_iota(I32, (t, group), 0)
    pooled = []
    for gi, win in enumerate(POOL_WINDOWS):
        sums = ext[:, gi * group:(gi + 1) * group]
        width = 1
        while width < win:
            sums = sums[width:] + sums[:-width]
            width *= 2
        wsum = sums[POOL_HALO - (win - 1):POOL_HALO - (win - 1) + t]
        cnt = jnp.minimum(pos + 1, win).astype(F32)
        pooled.append((wsum / cnt - h[:, gi * group:(gi + 1) * group]).astype(BF16))
    halves = _halves(t)
    x1 = []
    for rows in halves:
        y = jnp.concatenate([jnp.dot(pg[rows], w_ref[gi], preferred_element_type=F32)
                             for gi, pg in enumerate(pooled)], axis=-1) * scale_ref[...]
        x1.append(x[rows] + _rms(y, gpost_ref[...]))
    for rows, out in zip(halves, _xattn_block(x1, *xattn_refs)):
        o_ref[rows, :] = out


def _pool_xattn(x, g_pre, g_post, pool_w, pool_scale, xattn_args):
    b, s, d = x.shape
    t = ROW_TILE
    const = lambda shape: pl.BlockSpec(shape, lambda bi, i: (0,) * len(shape))
    prev = t // POOL_HALO
    xspecs, xarrays = _xattn_operands(d, xattn_args[3].shape[-1], *xattn_args)
    return pl.pallas_call(
        _pool_xattn_kernel,
        out_shape=jax.ShapeDtypeStruct((b, s, d), F32),
        grid=(b, s // t),
        in_specs=[
            pl.BlockSpec((None, t, d), lambda bi, i: (bi, i, 0)),
            pl.BlockSpec((None, POOL_HALO, d), lambda bi, i: (bi, jnp.maximum(i * prev - 1, 0), 0)),
            const((1, d)), const((1, d)), const(pool_w.shape), const((1, d)),
        ] + xspecs,
        out_specs=pl.BlockSpec((None, t, d), lambda bi, i: (bi, i, 0)),
        compiler_params=_params("parallel", "parallel"),
        name="pool_xattn",
    )(x, x, g_pre.reshape(1, d), g_post.reshape(1, d), pool_w.astype(BF16), pool_scale.reshape(1, d),
      *xarrays)


def _mem_kv_kernel(mem_ref, g_ref, wkt_ref, wv_ref, kt_ref, v_ref):
    mem_n = _rms(mem_ref[...], g_ref[...]).astype(BF16)
    kt = lax.dot_general(wkt_ref[...], mem_n, (((1,), (1,)), ((), ())), preferred_element_type=F32)
    kt_ref[...] = kt.astype(BF16)
    v_ref[...] = jnp.dot(mem_n, wv_ref[...], preferred_element_type=F32).astype(BF16)


def _mem_kv(mem, g, wk, wv):
    b, m, d = mem.shape
    const = lambda shape: pl.BlockSpec(shape, lambda bi: (0, 0))
    return pl.pallas_call(
        _mem_kv_kernel,
        out_shape=(jax.ShapeDtypeStruct((b, X_WIDTH, m), BF16), jax.ShapeDtypeStruct((b, m, X_WIDTH), BF16)),
        grid=(b,),
        in_specs=[pl.BlockSpec((None, m, d), lambda bi: (bi, 0, 0)), const((1, d)), const((X_WIDTH, d)),
                  const((d, X_WIDTH))],
        out_specs=(pl.BlockSpec((None, X_WIDTH, m), lambda bi: (bi, 0, 0)),
                   pl.BlockSpec((None, m, X_WIDTH), lambda bi: (bi, 0, 0))),
        compiler_params=_params("parallel"),
        name="mem_kv",
    )(mem, g.reshape(1, d), wk.astype(BF16).T, wv.astype(BF16))


def kernel(x, mem, ffn_w_gate, ffn_w_up, ffn_w_down, ffn_norm_pre, ffn_norm_post, mix_norm_pre,
           mix_norm_post, even_w_in, even_conv_w, even_w_out, rel_bias, pool_w, pool_scale,
           xattn_norm_pre, xattn_mem_norm, xattn_norm_post, xattn_wq, xattn_wk, xattn_wv, xattn_wo):
    b, s, d = x.shape
    depth = ffn_w_gate.shape[0]
    assert s % ATT_TILE == 0 and s // 4 >= TOPK_MAX and (b * s) % FFN_ROWS == 0 and s % ROW_TILE == 0
    bias = _bias_tiles(rel_bias, s)

    ffn_weights = (ffn_w_gate, ffn_w_up, ffn_w_down)

    def ffn(xx, layer, j):
        out = _ffn(xx.reshape(b * s, d), layer, j, ffn_norm_pre, ffn_norm_post, *ffn_weights)
        return out.reshape(b, s, d)

    for layer in range(depth):
        x = ffn(x, layer, 0)
        kt, v = _mem_kv(mem, xattn_mem_norm[layer], xattn_wk[layer], xattn_wv[layer])
        xattn_args = (xattn_norm_pre[layer], xattn_norm_post[layer], xattn_wq[layer], kt, v,
                      xattn_wo[layer])
        if layer % 2 == 0:
            e = layer // 2
            qt, vt, iqt, iwt, k, ik, c = _proj_even(x, mix_norm_pre[layer], even_w_in[e], even_conv_w[e])
            a = _dsa_attention(qt, vt, iqt, iwt, k, ik, bias)
            x = _mix_out_xattn(x, a, c, even_w_out[e], mix_norm_post[layer], xattn_args)
        else:
            o = layer // 2
            x = _pool_xattn(x, mix_norm_pre[layer], mix_norm_post[layer], pool_w[o], pool_scale[o],
                            xattn_args)
        x = ffn(x, layer, 1)
    return x
```

```python
import functools
import math

import jax
import jax.numpy as jnp
import numpy as np
from jax import lax
from jax.experimental import pallas as pl
from jax.experimental.pallas import tpu as pltpu

F32 = jnp.float32
BF16 = jnp.bfloat16
I32 = jnp.int32

RMS_EPS = 1e-6

A_HEADS = 8
A_HEAD_DIM = 64
A_WIDTH = A_HEADS * A_HEAD_DIM
IDX_HEADS = 8
IDX_DIM = 64
TOPK_MAX = 256
REL_BUCKETS = 32
REL_MAX_EXACT = 16
REL_MAX_DIST = 128
B_WIDTH = 512
CONV_K = 3
POOL_WINDOWS = (2, 4, 8, 16)
POOL_HALO = 16
X_HEADS = 4
X_HEAD_DIM = 128
X_WIDTH = X_HEADS * X_HEAD_DIM

LANES = 128
SUBLANES = 8
MXU_DIM = 256
VMEM_BYTES_V7X = 64 * 1024 * 1024
VMEM_LIMIT = VMEM_BYTES_V7X - 8 * 1024 * 1024

ATT_TILE = MXU_DIM
FFN_ROWS = 1024
FFN_COLS = MXU_DIM
FFN_LOAD_CHUNKS = 8
FFN_LOAD_SLOTS = 4
ROW_TILE = 512

BF16_SUBLANES = 16
PV_ROWS = A_HEAD_DIM + BF16_SUBLANES
LOG2E = math.log2(math.e)
MASKED = -1e30
KEY_MIN = -(2 ** 31)
WORD = 32


def _t5_bucket_table(n):
    d = np.arange(n)
    nf = np.maximum(d, 1).astype(np.float32)
    ratio = np.log(nf / np.float32(REL_MAX_EXACT)) / np.float32(math.log(REL_MAX_DIST / REL_MAX_EXACT))
    large = REL_MAX_EXACT + (ratio * np.float32(REL_BUCKETS - REL_MAX_EXACT)).astype(np.int32)
    large = np.minimum(large, REL_BUCKETS - 1)
    return np.where(d < REL_MAX_EXACT, d, large)


def _rms(x, g):
    ms = jnp.mean(x * x, axis=-1, keepdims=True)
    return x * lax.rsqrt(ms + RMS_EPS) * g


def _params(*semantics):
    return pltpu.CompilerParams(dimension_semantics=semantics, vmem_limit_bytes=VMEM_LIMIT)


def _ffn_kernel(layer, j, x_ref, gpre_ref, gpost_ref, wg_hbm, wu_hbm, wd_hbm, o_ref,
                h_ref, a_ref, wg_ref, wu_ref, wd_ref, stage_in_ref, stage_out_ref, sem_in, sem_out):
    @pl.when(pl.program_id(0) == 0)
    def _():
        plan = []
        for hbm, dst, stage, sems in ((wg_hbm, wg_ref, stage_in_ref, sem_in), (wu_hbm, wu_ref, stage_in_ref, sem_in),
                                      (wd_hbm, wd_ref, stage_out_ref, sem_out)):
            rows = stage.shape[1]
            for c in range(dst.shape[0] // rows):
                plan.append((hbm.at[layer, j, pl.ds(c * rows, rows), :], stage, sems, dst.at[pl.ds(c * rows, rows), :]))

        slots = stage_in_ref.shape[0]

        def copy(k):
            src, stage, sems, _ = plan[k]
            return pltpu.make_async_copy(src, stage.at[k % slots], sems.at[k % slots])

        for k in range(slots - 1):
            copy(k).start()
        for k in range(len(plan)):
            if k + slots - 1 < len(plan):
                copy(k + slots - 1).start()
            copy(k).wait()
            plan[k][3][...] = plan[k][1][k % slots].astype(BF16)

    for rows in _halves(x_ref.shape[0]):
        x = x_ref[rows, :]
        h_ref[rows, :] = _rms(x, gpre_ref[...]).astype(BF16)
        for f in range(a_ref.shape[1] // FFN_COLS):
            cols = slice(f * FFN_COLS, (f + 1) * FFN_COLS)
            g = jnp.dot(h_ref[rows, :], wg_ref[:, cols], preferred_element_type=F32)
            u = jnp.dot(h_ref[rows, :], wu_ref[:, cols], preferred_element_type=F32)
            a_ref[rows, cols] = (g * jax.nn.sigmoid(g) * u).astype(BF16)
        y = jnp.dot(a_ref[rows, :], wd_ref[...], preferred_element_type=F32)
        o_ref[rows, :] = x + 0.5 * _rms(y, gpost_ref[...])


def _ffn(x, layer, j, g_pre, g_post, w_gate, w_up, w_down):
    n, d = x.shape
    dff = w_gate.shape[-1]
    assert dff % FFN_COLS == 0 and (d // FFN_LOAD_CHUNKS) % BF16_SUBLANES == 0
    assert (dff // FFN_LOAD_CHUNKS) % BF16_SUBLANES == 0
    pick = lambda rows, cols: pl.BlockSpec((None, None, rows, cols), lambda i: (layer, j, 0, 0))
    hbm = pl.BlockSpec(memory_space=pl.ANY)
    depth = g_pre.shape[0]
    return pl.pallas_call(
        functools.partial(_ffn_kernel, layer, j),
        out_shape=jax.ShapeDtypeStruct((n, d), F32),
        grid=(n // FFN_ROWS,),
        in_specs=[pl.BlockSpec((FFN_ROWS, d), lambda i: (i, 0)), pick(1, d), pick(1, d), hbm, hbm, hbm],
        out_specs=pl.BlockSpec((FFN_ROWS, d), lambda i: (i, 0)),
        scratch_shapes=[
            pltpu.VMEM((FFN_ROWS, d), BF16), pltpu.VMEM((FFN_ROWS, dff), BF16),
            pltpu.VMEM((d, dff), BF16), pltpu.VMEM((d, dff), BF16), pltpu.VMEM((dff, d), BF16),
            pltpu.VMEM((FFN_LOAD_SLOTS, d // FFN_LOAD_CHUNKS, dff), F32),
            pltpu.VMEM((FFN_LOAD_SLOTS, dff // FFN_LOAD_CHUNKS, d), F32),
            pltpu.SemaphoreType.DMA((FFN_LOAD_SLOTS,)), pltpu.SemaphoreType.DMA((FFN_LOAD_SLOTS,)),
        ],
        compiler_params=_params("arbitrary"),
        name="ffn",
    )(x, g_pre.reshape(depth, 2, 1, d), g_post.reshape(depth, 2, 1, d), w_gate, w_up, w_down)


def _proj_kernel(x_ref, g_ref, wk_ref, wik_ref, wugg_ref, wt_ref,
                 convw_ref, qt_ref, vt_ref, iqt_ref, iwt_ref, k_ref, ik_ref, c_ref, carry_ref):
    @pl.when(pl.program_id(1) == 0)
    def _():
        carry_ref[...] = jnp.zeros_like(carry_ref)

    h = _rms(x_ref[...], g_ref[...]).astype(BF16)
    ugg = jnp.dot(h, wugg_ref[...], preferred_element_type=F32)
    u = ugg[:, :B_WIDTH]
    gb = ugg[:, B_WIDTH:2 * B_WIDTH]
    gc = ugg[:, 2 * B_WIDTH:]
    gu = gc * u
    t = gu.shape[0]
    ext = jnp.concatenate([carry_ref[...], gu], axis=0)
    w = convw_ref[...]
    y = w[2:3] * gu
    for j in range(CONV_K - 1):
        lo = SUBLANES - (CONV_K - 1) + j
        y = y + w[j:j + 1] * ext[lo:lo + t]
    c_ref[...] = (gb * y).astype(BF16)
    carry_ref[...] = gu[t - SUBLANES:]

    tr = lax.dot_general(wt_ref[...], h, (((1,), (1,)), ((), ())), preferred_element_type=F32)
    qt = (tr[:A_WIDTH] * (A_HEAD_DIM ** -0.5 * LOG2E)).astype(BF16)
    vt = tr[A_WIDTH:2 * A_WIDTH].astype(BF16)
    iq_end = 2 * A_WIDTH + IDX_HEADS * IDX_DIM
    iqt = tr[2 * A_WIDTH:iq_end].astype(BF16)
    iwt = tr[iq_end:iq_end + IDX_HEADS] * (IDX_HEADS ** -0.5 * IDX_DIM ** -0.5)
    k = jnp.dot(h, wk_ref[...], preferred_element_type=F32).astype(BF16)
    ik = jnp.dot(h, wik_ref[...], preferred_element_type=F32).astype(BF16)
    tile = qt_ref.shape[-1]
    ones = jnp.ones((PV_ROWS - A_HEAD_DIM, tile), BF16)
    for c in range(t // tile):
        tok = slice(c * tile, (c + 1) * tile)
        qt_ref[c] = qt[:, tok]
        iqt_ref[c] = iqt[:, tok]
        iwt_ref[c] = iwt[:, tok]
        k_ref[c] = k[tok]
        ik_ref[c] = ik[tok]
        for hd in range(A_HEADS):
            vt_ref[c, hd * PV_ROWS:hd * PV_ROWS + A_HEAD_DIM, :] = vt[hd * A_HEAD_DIM:(hd + 1) * A_HEAD_DIM, tok]
            vt_ref[c, hd * PV_ROWS + A_HEAD_DIM:(hd + 1) * PV_ROWS, :] = ones


def _proj_even(x, g, w_in, conv_w):
    b, s, d = x.shape
    t = ATT_TILE
    nchunk = s // t
    per = ROW_TILE // t
    q_end, k_end, v_end = A_WIDTH, 2 * A_WIDTH, 3 * A_WIDTH
    iq_end = v_end + IDX_HEADS * IDX_DIM
    ik_end = iq_end + IDX_DIM
    iw_end = ik_end + IDX_HEADS
    wb = w_in.astype(BF16)
    wk = wb[:, q_end:k_end]
    wik = wb[:, iq_end:ik_end]
    wugg = wb[:, iw_end:]
    wt = jnp.concatenate([wb[:, :q_end], wb[:, k_end:v_end], wb[:, v_end:iq_end], wb[:, ik_end:iw_end]], axis=1).T
    wt = jnp.pad(wt, ((0, BF16_SUBLANES - IDX_HEADS), (0, 0)))
    const = lambda shape: pl.BlockSpec(shape, lambda bi, i: (0,) * len(shape))
    return pl.pallas_call(
        _proj_kernel,
        out_shape=(
            jax.ShapeDtypeStruct((b, nchunk, A_WIDTH, t), BF16),
            jax.ShapeDtypeStruct((b, nchunk, A_HEADS * PV_ROWS, t), BF16),
            jax.ShapeDtypeStruct((b, nchunk, IDX_HEADS * IDX_DIM, t), BF16),
            jax.ShapeDtypeStruct((b, nchunk, IDX_HEADS, t), F32),
            jax.ShapeDtypeStruct((b, nchunk, t, A_WIDTH), BF16),
            jax.ShapeDtypeStruct((b, nchunk, t, IDX_DIM), BF16),
            jax.ShapeDtypeStruct((b, s, B_WIDTH), BF16),
        ),
        grid=(b, nchunk // per),
        in_specs=[
            pl.BlockSpec((None, per * t, d), lambda bi, i: (bi, i, 0)),
            const((1, d)),
            const(wk.shape), const(wik.shape), const(wugg.shape), const(wt.shape), const(conv_w.shape),
        ],
        out_specs=(
            pl.BlockSpec((None, per, A_WIDTH, t), lambda bi, i: (bi, i, 0, 0)),
            pl.BlockSpec((None, per, A_HEADS * PV_ROWS, t), lambda bi, i: (bi, i, 0, 0)),
            pl.BlockSpec((None, per, IDX_HEADS * IDX_DIM, t), lambda bi, i: (bi, i, 0, 0)),
            pl.BlockSpec((None, per, IDX_HEADS, t), lambda bi, i: (bi, i, 0, 0)),
            pl.BlockSpec((None, per, t, A_WIDTH), lambda bi, i: (bi, i, 0, 0)),
            pl.BlockSpec((None, per, t, IDX_DIM), lambda bi, i: (bi, i, 0, 0)),
            pl.BlockSpec((None, per * t, B_WIDTH), lambda bi, i: (bi, i, 0)),
        ),
        scratch_shapes=[pltpu.VMEM((SUBLANES, B_WIDTH), F32)],
        compiler_params=_params("arbitrary", "arbitrary"),
        name="proj_even",
    )(x, g.reshape(1, d), wk, wik, wugg, wt, conv_w)


def _bias_kernel(rb_ref, o_ref, *, thresholds, far_bucket):
    hd = pl.program_id(0)
    t = o_ref.shape[-1]
    r = lax.broadcasted_iota(I32, (t, t), 1)
    c = lax.broadcasted_iota(I32, (t, t), 0)
    for m in range(2):
        dist = t * m + r - c
        val = jnp.full((t, t), rb_ref[0, hd], F32)
        for bucket in range(1, REL_BUCKETS):
            val = jnp.where(dist >= thresholds[bucket], rb_ref[bucket, hd], val)
        val = (val - rb_ref[far_bucket, hd]) * LOG2E
        if m == 0:
            val = jnp.where(dist < 0, MASKED, val)
        o_ref[m] = val
    o_ref[2] = jnp.zeros((t, t), F32)


def _bias_tiles(rel_bias, seq):
    t = ATT_TILE
    table = _t5_bucket_table(seq)
    assert np.all(np.diff(table) >= 0) and table[0] == 0 and table.max() == REL_BUCKETS - 1
    thresholds = tuple(int(np.argmax(table >= bkt)) for bkt in range(REL_BUCKETS))
    far_bucket = int(table[2 * t - 1])
    assert np.all(table[2 * t - 1:] == far_bucket)
    return pl.pallas_call(
        functools.partial(_bias_kernel, thresholds=thresholds, far_bucket=far_bucket),
        out_shape=jax.ShapeDtypeStruct((A_HEADS, 3, t, t), F32),
        grid=(A_HEADS,),
        in_specs=[pl.BlockSpec(memory_space=pltpu.SMEM)],
        out_specs=pl.BlockSpec((None, 3, t, t), lambda h: (h, 0, 0, 0)),
        compiler_params=_params("parallel"),
        name="rel_bias_tiles",
    )(rel_bias)


def _bit_transpose32(words):
    a = list(words)
    shift, mask = WORD // 2, 0x0000FFFF
    while shift:
        k = 0
        while k < WORD:
            swap = (a[k] ^ lax.shift_right_logical(a[k + shift], jnp.int32(shift))) & mask
            a[k] = a[k] ^ swap
            a[k + shift] = a[k + shift] ^ (swap << shift)
            k = (k + shift + 1) & ~shift
        shift //= 2
        mask ^= (mask << shift) & 0xFFFFFFFF
    return a


def _dsa_kernel(qt_ref, iqt_ref, iwt_ref, k_ref, vt_ref, ik_ref, bias_ref, o_ref,
                sc_ref, plane_ref, eq_ref, acc_ref, m_ref, qz_ref, mask_ref, s_ref, cm_ref):
    i = pl.program_id(1)
    t = ATT_TILE
    nk = i + 1
    iw = iwt_ref[...]

    def index_scores(j):
        ikc = ik_ref[j]
        sc = jnp.zeros((t, t), F32)
        for hd in range(IDX_HEADS):
            d = jnp.dot(ikc, iqt_ref[hd * IDX_DIM:(hd + 1) * IDX_DIM, :],
                        preferred_element_type=F32)
            sc = sc + jnp.maximum(d, 0.0) * iw[hd:hd + 1, :]
        return sc

    def store_scores(j, sc):
        sc_ref[j] = sc
        bits = pltpu.bitcast(sc + 0.0, I32)
        ukeys = bits ^ ((bits >> 31) | KEY_MIN)
        for col in range(t // LANES):
            lanes = slice(col * LANES, (col + 1) * LANES)
            planes = _bit_transpose32([ukeys[v * SUBLANES:(v + 1) * SUBLANES, lanes] for v in range(WORD)])
            for r in range(WORD):
                plane_ref[r, j, :, lanes] = planes[r]

    @pl.when(i == 0)
    def _():
        plane_ref[...] = jnp.zeros(plane_ref.shape, I32)

    def phase_a(pair, carry):
        store_scores(2 * pair, index_scores(2 * pair))
        store_scores(2 * pair + 1, index_scores(2 * pair + 1))
        return carry

    lax.fori_loop(0, lax.shift_right_logical(i, 1), phase_a, 0)

    @pl.when((i & 1) == 1)
    def _():
        store_scores(i - 1, index_scores(i - 1))

    qpos = lax.broadcasted_iota(I32, (t, t), 1)
    kpos = lax.broadcasted_iota(I32, (t, t), 0)
    store_scores(i, jnp.where(kpos <= qpos, index_scores(i), -jnp.inf))

    nchunk = eq_ref.shape[0]
    live = lax.broadcasted_iota(I32, eq_ref.shape, 0) < nk
    eq_ref[...] = jnp.where(live, jnp.int32(-1), jnp.int32(0))

    def search(span):
        def search_bit(r, carry):
            thr_u, above = carry
            cnt = jnp.zeros((SUBLANES, t), I32)
            for c in range(span):
                cnt = cnt + lax.population_count(eq_ref[c] & plane_ref[r, c])
            cnt = jnp.sum(cnt.astype(F32), axis=0, keepdims=True)
            take = above + cnt >= float(TOPK_MAX)
            for c in range(span):
                eq = eq_ref[c]
                with_bit = eq & plane_ref[r, c]
                eq_ref[c] = jnp.where(take, with_bit, eq ^ with_bit)
            bit = jnp.left_shift(jnp.int32(1), WORD - 1 - r)
            return jnp.where(take, thr_u | bit, thr_u), jnp.where(take, above, above + cnt)

        return lambda: lax.fori_loop(0, WORD, search_bit,
                                     (jnp.zeros((1, t), I32), jnp.zeros((1, t), F32)))

    quarter = nchunk // 4
    thr_u, _ = lax.cond(
        nk <= 2 * quarter,
        lambda: lax.cond(nk <= quarter, search(quarter), search(2 * quarter)),
        lambda: lax.cond(nk <= 3 * quarter, search(3 * quarter), search(nchunk)))

    def key_to_score(ukey):
        return pltpu.bitcast(ukey ^ (~(ukey >> 31) | KEY_MIN), F32)

    def count_scores(*preds):
        def body(j, cnts):
            sc = sc_ref[j]
            return tuple(cnt + jnp.sum(jnp.where(pred(sc), 1.0, 0.0).reshape(t // SUBLANES, SUBLANES, t), axis=0)
                         for cnt, pred in zip(cnts, preds))
        cnts = lax.fori_loop(0, nk, body, tuple(jnp.zeros((SUBLANES, t), F32) for _ in preds))
        return [jnp.sum(cnt, axis=0, keepdims=True) for cnt in cnts]

    thr_fast = key_to_score(thr_u)
    reach, exceed = count_scores(lambda sc: sc >= thr_fast, lambda sc: sc > thr_fast)
    good = jnp.logical_and(exceed < float(TOPK_MAX), reach >= float(TOPK_MAX))

    def search_by_value():
        def step(r, ukey):
            cand = ukey | jnp.left_shift(jnp.int32(1), WORD - 1 - r)
            cand_score = key_to_score(cand)
            reach, = count_scores(lambda sc: sc >= cand_score)
            return jnp.where(reach >= float(TOPK_MAX), cand, ukey)
        ukey = lax.fori_loop(0, WORD, step, jnp.zeros((1, t), I32))
        thr_slow = jnp.where(ukey == 0, -jnp.inf, key_to_score(ukey))
        exceed, = count_scores(lambda sc: sc > thr_slow)
        return thr_slow, exceed

    thr, exceed = lax.cond(jnp.min(jnp.where(good, 1.0, 0.0)) > 0.5,
                           lambda: (thr_fast, exceed), search_by_value)
    need = float(TOPK_MAX) - exceed

    m_ref[...] = jnp.full(m_ref.shape, MASKED, F32)
    acc_ref[...] = jnp.zeros(acc_ref.shape, F32)
    row = lax.broadcasted_iota(I32, (2 * A_HEAD_DIM, t), 0)
    for hd in range(A_HEADS):
        pair = hd // 2
        blk = qt_ref[2 * A_HEAD_DIM * pair:2 * A_HEAD_DIM * (pair + 1), :]
        own = (row >= A_HEAD_DIM) if hd % 2 else (row < A_HEAD_DIM)
        qz_ref[hd] = jnp.where(own, blk, jnp.zeros_like(blk))
    tri = (lax.broadcasted_iota(I32, (t, t), 0) >= lax.broadcasted_iota(I32, (t, t), 1)).astype(BF16)

    def select(j, ties_before):
        kk = sc_ref[j]
        is_tie = kk == thr
        tie_rank = jnp.dot(tri, jnp.where(is_tie, 1.0, 0.0).astype(BF16),
                           preferred_element_type=F32) + ties_before
        tie_add = jnp.where(is_tie, jnp.where(tie_rank <= need, 0.0, MASKED), MASKED)
        mask_ref[...] = jnp.where(kk > thr, 0.0, tie_add).astype(BF16)
        return tie_rank[t - 1:t, :]

    slabs = (t // BF16_SUBLANES, BF16_SUBLANES, t)

    def scores(with_bias, slot, j, hd):
        pair = hd // 2
        s = jnp.dot(k_ref[j, :, 2 * A_HEAD_DIM * pair:2 * A_HEAD_DIM * (pair + 1)], qz_ref[hd],
                    preferred_element_type=F32)
        if with_bias:
            s = s + bias_ref[hd, jnp.minimum(i - j, 2)]
        s = s.astype(BF16) + mask_ref[...]
        s_ref[slot, hd] = s
        parts = [s.reshape(slabs)[r] for r in range(slabs[0])]
        while len(parts) > 1:
            parts = [jnp.maximum(a, b) for a, b in zip(parts[::2], parts[1::2])]
        cm_ref[slot, hd] = jnp.max(parts[0].astype(F32), axis=0, keepdims=True)

    def consume(slot, j, hd):
        m_old = m_ref[hd]
        m_new = jnp.maximum(m_old, cm_ref[slot, hd])
        m_ref[hd] = m_new
        alpha = jnp.exp2(m_old - m_new)
        m_tile = jnp.broadcast_to(m_new, (BF16_SUBLANES, t)).astype(BF16)
        p = jnp.exp2(s_ref[slot, hd].reshape(slabs) - m_tile[None]).reshape(t, t)
        pv = jnp.dot(vt_ref[j, hd * PV_ROWS:(hd + 1) * PV_ROWS, :], p, preferred_element_type=F32)
        acc_ref[hd] = alpha * acc_ref[hd] + pv

    def half_step(with_bias, slot, j, ties):
        ties = select(j, ties)
        for hd in range(A_HEADS):
            scores(with_bias, slot, j, hd)
            consume(1 - slot, j - 1, hd)
        return ties

    def step(with_bias, j, ties_before):
        return lax.cond(lax.rem(j, 2) == 1, functools.partial(half_step, with_bias, 1, j),
                        functools.partial(half_step, with_bias, 0, j), ties_before)

    def far_pair(pair, ties):
        ties = half_step(False, 1, 2 * pair + 1, ties)
        return half_step(False, 0, 2 * pair + 2, ties)

    ties = select(0, jnp.zeros((1, t), F32))
    for hd in range(A_HEADS):
        scores(True, 0, 0, hd)
    first_near = jnp.maximum(i - 1, 1)
    far_pairs = lax.shift_right_logical(first_near - 1, 1)
    ties = lax.fori_loop(0, far_pairs, far_pair, ties)
    ties = lax.fori_loop(2 * far_pairs + 1, first_near, functools.partial(step, False), ties)
    lax.fori_loop(first_near, nk, functools.partial(step, True), ties)
    for slot in range(2):
        @pl.when(lax.rem(i, 2) == slot)
        def _():
            for hd in range(A_HEADS):
                consume(slot, i, hd)

    outs = []
    for hd in range(A_HEADS):
        outs.append(acc_ref[hd, :A_HEAD_DIM, :] / acc_ref[hd, A_HEAD_DIM:A_HEAD_DIM + 1, :])
    o_ref[...] = jnp.concatenate(outs, axis=0).T.astype(BF16)


def _dsa_attention(qt, vt, iqt, iwt, k, ik, bias):
    b, nchunk, t, _ = k.shape
    s = nchunk * t
    return pl.pallas_call(
        _dsa_kernel,
        out_shape=jax.ShapeDtypeStruct((b, s, A_WIDTH), BF16),
        grid=(b, nchunk),
        in_specs=[
            pl.BlockSpec((None, None, A_WIDTH, t), lambda bi, i: (bi, i, 0, 0)),
            pl.BlockSpec((None, None, IDX_HEADS * IDX_DIM, t), lambda bi, i: (bi, i, 0, 0)),
            pl.BlockSpec((None, None, IDX_HEADS, t), lambda bi, i: (bi, i, 0, 0)),
            pl.BlockSpec((None, nchunk, t, A_WIDTH), lambda bi, i: (bi, 0, 0, 0)),
            pl.BlockSpec((None, nchunk, A_HEADS * PV_ROWS, t), lambda bi, i: (bi, 0, 0, 0)),
            pl.BlockSpec((None, nchunk, t, IDX_DIM), lambda bi, i: (bi, 0, 0, 0)),
            pl.BlockSpec(bias.shape, lambda bi, i: (0, 0, 0, 0)),
        ],
        out_specs=pl.BlockSpec((None, t, A_WIDTH), lambda bi, i: (bi, i, 0)),
        scratch_shapes=[
            pltpu.VMEM((nchunk, t, t), F32),
            pltpu.VMEM((WORD, nchunk, t // WORD, t), I32),
            pltpu.VMEM((nchunk, t // WORD, t), I32),
            pltpu.VMEM((A_HEADS, PV_ROWS, t), F32),
            pltpu.VMEM((A_HEADS, 1, t), F32),
            pltpu.VMEM((A_HEADS, 2 * A_HEAD_DIM, t), BF16),
            pltpu.VMEM((t, t), BF16),
            pltpu.VMEM((2, A_HEADS, t, t), BF16),
            pltpu.VMEM((2, A_HEADS, 1, t), F32),
        ],
        compiler_params=_params("parallel", "arbitrary"),
        name="dsa_attention",
    )(qt, iqt, iwt, k, vt, ik, bias)


def _halves(n):
    return (slice(0, n // 2), slice(n // 2, n))


def _xattn_block(parts, gpre_ref, gpost_ref, wq_ref, kt_ref, v_ref, wo_ref):
    heads = [slice(hd * X_HEAD_DIM, (hd + 1) * X_HEAD_DIM) for hd in range(X_HEADS)]
    hs = [_rms(x, gpre_ref[...]).astype(BF16) for x in parts]
    qs = [jnp.dot(h, wq_ref[...], preferred_element_type=F32) for h in hs]
    logits = [[jnp.dot(q[:, cols].astype(BF16), kt_ref[cols, :], preferred_element_type=F32)
               * (X_HEAD_DIM ** -0.5) for cols in heads] for q in qs]
    os = []
    for part_logits in logits:
        outs = []
        for cols, lg in zip(heads, part_logits):
            p = jnp.exp(lg - jnp.max(lg, axis=-1, keepdims=True))
            den = jnp.sum(p, axis=-1, keepdims=True)
            outs.append(jnp.dot(p.astype(BF16), v_ref[:, cols], preferred_element_type=F32) / den)
        os.append(jnp.concatenate(outs, axis=-1).astype(BF16))
    ys = [jnp.dot(o, wo_ref[...], preferred_element_type=F32) for o in os]
    return [x + _rms(y, gpost_ref[...]) for x, y in zip(parts, ys)]


def _xattn_operands(d, m, g_pre, g_post, wq, kt, v, wo):
    const = lambda shape: pl.BlockSpec(shape, lambda bi, i: (0, 0))
    specs = [const((1, d)), const((1, d)), const((d, X_WIDTH)),
             pl.BlockSpec((None, X_WIDTH, m), lambda bi, i: (bi, 0, 0)),
             pl.BlockSpec((None, m, X_WIDTH), lambda bi, i: (bi, 0, 0)),
             const((X_WIDTH, d))]
    return specs, (g_pre.reshape(1, d), g_post.reshape(1, d), wq.astype(BF16), kt, v, wo.astype(BF16))


def _mix_out_xattn_kernel(x_ref, a_ref, c_ref, wa_ref, wc_ref, g_ref, *rest):
    *xattn_refs, o_ref = rest
    halves = _halves(x_ref.shape[0])
    ys = [jnp.dot(a_ref[rows, :], wa_ref[...], preferred_element_type=F32)
          + jnp.dot(c_ref[rows, :], wc_ref[...], preferred_element_type=F32) for rows in halves]
    x1 = [x_ref[rows, :] + _rms(y, g_ref[...]) for rows, y in zip(halves, ys)]
    for rows, out in zip(halves, _xattn_block(x1, *xattn_refs)):
        o_ref[rows, :] = out


def _mix_out_xattn(x, a, c, w_out, g_post, xattn_args):
    b, s, d = x.shape
    t = ROW_TILE
    wb = w_out.astype(BF16)
    row = lambda width: pl.BlockSpec((None, t, width), lambda bi, i: (bi, i, 0))
    const = lambda shape: pl.BlockSpec(shape, lambda bi, i: (0, 0))
    xspecs, xarrays = _xattn_operands(d, xattn_args[3].shape[-1], *xattn_args)
    return pl.pallas_call(
        _mix_out_xattn_kernel,
        out_shape=jax.ShapeDtypeStruct((b, s, d), F32),
        grid=(b, s // t),
        in_specs=[row(d), row(A_WIDTH), row(B_WIDTH), const((A_WIDTH, d)), const((B_WIDTH, d)),
                  const((1, d))] + xspecs,
        out_specs=row(d),
        compiler_params=_params("parallel", "parallel"),
        name="mix_out_xattn",
    )(x, a, c, wb[:A_WIDTH], wb[A_WIDTH:], g_post.reshape(1, d), *xarrays)


def _pool_xattn_kernel(x_ref, xprev_ref, gpre_ref, gpost_ref, w_ref, scale_ref, *rest):
    *xattn_refs, o_ref = rest
    i = pl.program_id(1)
    x = x_ref[...]
    t, d = x.shape
    gpre = gpre_ref[...]
    h = _rms(x, gpre)
    hp = _rms(xprev_ref[...], gpre) * jnp.where(i > 0, 1.0, 0.0)
    ext = jnp.concatenate([hp, h], axis=0)
    group = d // len(POOL_WINDOWS)
    pos = i * t + lax.broadcasted_iota(I32, (t, group), 0)
    pooled = []
    for gi, win in enumerate(POOL_WINDOWS):
        sums = ext[:, gi * group:(gi + 1) * group]
        width = 1
        while width < win:
            sums = sums[width:] + sums[:-width]
            width *= 2
        wsum = sums[POOL_HALO - (win - 1):POOL_HALO - (win - 1) + t]
        cnt = jnp.minimum(pos + 1, win).astype(F32)
        pooled.append((wsum / cnt - h[:, gi * group:(gi + 1) * group]).astype(BF16))
    halves = _halves(t)
    x1 = []
    for rows in halves:
        y = jnp.concatenate([jnp.dot(pg[rows], w_ref[gi], preferred_element_type=F32)
                             for gi, pg in enumerate(pooled)], axis=-1) * scale_ref[...]
        x1.append(x[rows] + _rms(y, gpost_ref[...]))
    for rows, out in zip(halves, _xattn_block(x1, *xattn_refs)):
        o_ref[rows, :] = out


def _pool_xattn(x, g_pre, g_post, pool_w, pool_scale, xattn_args):
    b, s, d = x.shape
    t = ROW_TILE
    const = lambda shape: pl.BlockSpec(shape, lambda bi, i: (0,) * len(shape))
    prev = t // POOL_HALO
    xspecs, xarrays = _xattn_operands(d, xattn_args[3].shape[-1], *xattn_args)
    return pl.pallas_call(
        _pool_xattn_kernel,
        out_shape=jax.ShapeDtypeStruct((b, s, d), F32),
        grid=(b, s // t),
        in_specs=[
            pl.BlockSpec((None, t, d), lambda bi, i: (bi, i, 0)),
            pl.BlockSpec((None, POOL_HALO, d), lambda bi, i: (bi, jnp.maximum(i * prev - 1, 0), 0)),
            const((1, d)), const((1, d)), const(pool_w.shape), const((1, d)),
        ] + xspecs,
        out_specs=pl.BlockSpec((None, t, d), lambda bi, i: (bi, i, 0)),
        compiler_params=_params("parallel", "parallel"),
        name="pool_xattn",
    )(x, x, g_pre.reshape(1, d), g_post.reshape(1, d), pool_w.astype(BF16), pool_scale.reshape(1, d),
      *xarrays)


def _mem_kv_kernel(mem_ref, g_ref, wkt_ref, wv_ref, kt_ref, v_ref):
    mem_n = _rms(mem_ref[...], g_ref[...]).astype(BF16)
    kt = lax.dot_general(wkt_ref[...], mem_n, (((1,), (1,)), ((), ())), preferred_element_type=F32)
    kt_ref[...] = kt.astype(BF16)
    v_ref[...] = jnp.dot(mem_n, wv_ref[...], preferred_element_type=F32).astype(BF16)


def _mem_kv(mem, g, wk, wv):
    b, m, d = mem.shape
    const = lambda shape: pl.BlockSpec(shape, lambda bi: (0, 0))
    return pl.pallas_call(
        _mem_kv_kernel,
        out_shape=(jax.ShapeDtypeStruct((b, X_WIDTH, m), BF16), jax.ShapeDtypeStruct((b, m, X_WIDTH), BF16)),
        grid=(b,),
        in_specs=[pl.BlockSpec((None, m, d), lambda bi: (bi, 0, 0)), const((1, d)), const((X_WIDTH, d)),
                  const((d, X_WIDTH))],
        out_specs=(pl.BlockSpec((None, X_WIDTH, m), lambda bi: (bi, 0, 0)),
                   pl.BlockSpec((None, m, X_WIDTH), lambda bi: (bi, 0, 0))),
        compiler_params=_params("parallel"),
        name="mem_kv",
    )(mem, g.reshape(1, d), wk.astype(BF16).T, wv.astype(BF16))


def kernel(x, mem, ffn_w_gate, ffn_w_up, ffn_w_down, ffn_norm_pre, ffn_norm_post, mix_norm_pre,
           mix_norm_post, even_w_in, even_conv_w, even_w_out, rel_bias, pool_w, pool_scale,
           xattn_norm_pre, xattn_mem_norm, xattn_norm_post, xattn_wq, xattn_wk, xattn_wv, xattn_wo):
    b, s, d = x.shape
    depth = ffn_w_gate.shape[0]
    assert s % ATT_TILE == 0 and s // 4 >= TOPK_MAX and (b * s) % FFN_ROWS == 0 and s % ROW_TILE == 0
    bias = _bias_tiles(rel_bias, s)

    ffn_weights = (ffn_w_gate, ffn_w_up, ffn_w_down)

    def ffn(xx, layer, j):
        out = _ffn(xx.reshape(b * s, d), layer, j, ffn_norm_pre, ffn_norm_post, *ffn_weights)
        return out.reshape(b, s, d)

    for layer in range(depth):
        x = ffn(x, layer, 0)
        kt, v = _mem_kv(mem, xattn_mem_norm[layer], xattn_wk[layer], xattn_wv[layer])
        xattn_args = (xattn_norm_pre[layer], xattn_norm_post[layer], xattn_wq[layer], kt, v,
                      xattn_wo[layer])
        if layer % 2 == 0:
            e = layer // 2
            qt, vt, iqt, iwt, k, ik, c = _proj_even(x, mix_norm_pre[layer], even_w_in[e], even_conv_w[e])
            a = _dsa_attention(qt, vt, iqt, iwt, k, ik, bias)
            x = _mix_out_xattn(x, a, c, even_w_out[e], mix_norm_post[layer], xattn_args)
        else:
            o = layer // 2
            x = _pool_xattn(x, mix_norm_pre[layer], mix_norm_post[layer], pool_w[o], pool_scale[o],
                            xattn_args)
        x = ffn(x, layer, 1)
    return x
```

```python
import functools
import math

import jax
import jax.numpy as jnp
import numpy as np
from jax import lax
from jax.experimental import pallas as pl
from jax.experimental.pallas import tpu as pltpu

F32 = jnp.float32
BF16 = jnp.bfloat16
I32 = jnp.int32

RMS_EPS = 1e-6

A_HEADS = 8
A_HEAD_DIM = 64
A_WIDTH = A_HEADS * A_HEAD_DIM
IDX_HEADS = 8
IDX_DIM = 64
TOPK_MAX = 256
REL_BUCKETS = 32
REL_MAX_EXACT = 16
REL_MAX_DIST = 128
B_WIDTH = 512
CONV_K = 3
POOL_WINDOWS = (2, 4, 8, 16)
POOL_HALO = 16
X_HEADS = 4
X_HEAD_DIM = 128
X_WIDTH = X_HEADS * X_HEAD_DIM

LANES = 128
SUBLANES = 8
MXU_DIM = 256
VMEM_BYTES_V7X = 64 * 1024 * 1024
VMEM_LIMIT = VMEM_BYTES_V7X - 8 * 1024 * 1024

ATT_TILE = MXU_DIM
FFN_ROWS = 1024
FFN_COLS = MXU_DIM
FFN_LOAD_CHUNKS = 8
FFN_LOAD_SLOTS = 4
ROW_TILE = 1024

BF16_SUBLANES = 16
PV_ROWS = A_HEAD_DIM + BF16_SUBLANES
LOG2E = math.log2(math.e)
MASKED = -1e30
KEY_MIN = -(2 ** 31)
WORD = 32


def _t5_bucket_table(n):
    d = np.arange(n)
    nf = np.maximum(d, 1).astype(np.float32)
    ratio = np.log(nf / np.float32(REL_MAX_EXACT)) / np.float32(math.log(REL_MAX_DIST / REL_MAX_EXACT))
    large = REL_MAX_EXACT + (ratio * np.float32(REL_BUCKETS - REL_MAX_EXACT)).astype(np.int32)
    large = np.minimum(large, REL_BUCKETS - 1)
    return np.where(d < REL_MAX_EXACT, d, large)


def _rms(x, g):
    ms = jnp.mean(x * x, axis=-1, keepdims=True)
    return x * lax.rsqrt(ms + RMS_EPS) * g


def _params(*semantics):
    return pltpu.CompilerParams(dimension_semantics=semantics, vmem_limit_bytes=VMEM_LIMIT)


def _ffn_kernel(layer, j, x_ref, gpre_ref, gpost_ref, wg_hbm, wu_hbm, wd_hbm, o_ref,
                h_ref, a_ref, wg_ref, wu_ref, wd_ref, stage_in_ref, stage_out_ref, sem_in, sem_out):
    @pl.when(pl.program_id(0) == 0)
    def _():
        plan = []
        for hbm, dst, stage, sems in ((wg_hbm, wg_ref, stage_in_ref, sem_in), (wu_hbm, wu_ref, stage_in_ref, sem_in),
                                      (wd_hbm, wd_ref, stage_out_ref, sem_out)):
            rows = stage.shape[1]
            for c in range(dst.shape[0] // rows):
                plan.append((hbm.at[layer, j, pl.ds(c * rows, rows), :], stage, sems, dst.at[pl.ds(c * rows, rows), :]))

        slots = stage_in_ref.shape[0]

        def copy(k):
            src, stage, sems, _ = plan[k]
            return pltpu.make_async_copy(src, stage.at[k % slots], sems.at[k % slots])

        for k in range(slots - 1):
            copy(k).start()
        for k in range(len(plan)):
            if k + slots - 1 < len(plan):
                copy(k + slots - 1).start()
            copy(k).wait()
            plan[k][3][...] = plan[k][1][k % slots].astype(BF16)

    for rows in _halves(x_ref.shape[0]):
        x = x_ref[rows, :]
        h_ref[rows, :] = _rms(x, gpre_ref[...]).astype(BF16)
        for f in range(a_ref.shape[1] // FFN_COLS):
            cols = slice(f * FFN_COLS, (f + 1) * FFN_COLS)
            g = jnp.dot(h_ref[rows, :], wg_ref[:, cols], preferred_element_type=F32)
            u = jnp.dot(h_ref[rows, :], wu_ref[:, cols], preferred_element_type=F32)
            a_ref[rows, cols] = (g * jax.nn.sigmoid(g) * u).astype(BF16)
        y = jnp.dot(a_ref[rows, :], wd_ref[...], preferred_element_type=F32)
        o_ref[rows, :] = x + 0.5 * _rms(y, gpost_ref[...])


def _ffn(x, layer, j, g_pre, g_post, w_gate, w_up, w_down):
    n, d = x.shape
    dff = w_gate.shape[-1]
    assert dff % FFN_COLS == 0 and (d // FFN_LOAD_CHUNKS) % BF16_SUBLANES == 0
    assert (dff // FFN_LOAD_CHUNKS) % BF16_SUBLANES == 0
    pick = lambda rows, cols: pl.BlockSpec((None, None, rows, cols), lambda i: (layer, j, 0, 0))
    hbm = pl.BlockSpec(memory_space=pl.ANY)
    depth = g_pre.shape[0]
    return pl.pallas_call(
        functools.partial(_ffn_kernel, layer, j),
        out_shape=jax.ShapeDtypeStruct((n, d), F32),
        grid=(n // FFN_ROWS,),
        in_specs=[pl.BlockSpec((FFN_ROWS, d), lambda i: (i, 0)), pick(1, d), pick(1, d), hbm, hbm, hbm],
        out_specs=pl.BlockSpec((FFN_ROWS, d), lambda i: (i, 0)),
        scratch_shapes=[
            pltpu.VMEM((FFN_ROWS, d), BF16), pltpu.VMEM((FFN_ROWS, dff), BF16),
            pltpu.VMEM((d, dff), BF16), pltpu.VMEM((d, dff), BF16), pltpu.VMEM((dff, d), BF16),
            pltpu.VMEM((FFN_LOAD_SLOTS, d // FFN_LOAD_CHUNKS, dff), F32),
            pltpu.VMEM((FFN_LOAD_SLOTS, dff // FFN_LOAD_CHUNKS, d), F32),
            pltpu.SemaphoreType.DMA((FFN_LOAD_SLOTS,)), pltpu.SemaphoreType.DMA((FFN_LOAD_SLOTS,)),
        ],
        compiler_params=_params("arbitrary"),
        name="ffn",
    )(x, g_pre.reshape(depth, 2, 1, d), g_post.reshape(depth, 2, 1, d), w_gate, w_up, w_down)


def _proj_kernel(x_ref, g_ref, wk_ref, wik_ref, wugg_ref, wt_ref,
                 convw_ref, qt_ref, vt_ref, iqt_ref, iwt_ref, k_ref, ik_ref, c_ref, carry_ref):
    @pl.when(pl.program_id(1) == 0)
    def _():
        carry_ref[...] = jnp.zeros_like(carry_ref)

    h = _rms(x_ref[...], g_ref[...]).astype(BF16)
    ugg = jnp.dot(h, wugg_ref[...], preferred_element_type=F32)
    u = ugg[:, :B_WIDTH]
    gb = ugg[:, B_WIDTH:2 * B_WIDTH]
    gc = ugg[:, 2 * B_WIDTH:]
    gu = gc * u
    t = gu.shape[0]
    ext = jnp.concatenate([carry_ref[...], gu], axis=0)
    w = convw_ref[...]
    y = w[2:3] * gu
    for j in range(CONV_K - 1):
        lo = SUBLANES - (CONV_K - 1) + j
        y = y + w[j:j + 1] * ext[lo:lo + t]
    c_ref[...] = (gb * y).astype(BF16)
    carry_ref[...] = gu[t - SUBLANES:]

    tr = lax.dot_general(wt_ref[...], h, (((1,), (1,)), ((), ())), preferred_element_type=F32)
    qt = (tr[:A_WIDTH] * (A_HEAD_DIM ** -0.5 * LOG2E)).astype(BF16)
    vt = tr[A_WIDTH:2 * A_WIDTH].astype(BF16)
    iq_end = 2 * A_WIDTH + IDX_HEADS * IDX_DIM
    iqt = tr[2 * A_WIDTH:iq_end].astype(BF16)
    iwt = tr[iq_end:iq_end + IDX_HEADS] * (IDX_HEADS ** -0.5 * IDX_DIM ** -0.5)
    k = jnp.dot(h, wk_ref[...], preferred_element_type=F32).astype(BF16)
    ik = jnp.dot(h, wik_ref[...], preferred_element_type=F32).astype(BF16)
    tile = qt_ref.shape[-1]
    ones = jnp.ones((PV_ROWS - A_HEAD_DIM, tile), BF16)
    for c in range(t // tile):
        tok = slice(c * tile, (c + 1) * tile)
        qt_ref[c] = qt[:, tok]
        iqt_ref[c] = iqt[:, tok]
        iwt_ref[c] = iwt[:, tok]
        k_ref[c] = k[tok]
        ik_ref[c] = ik[tok]
        for hd in range(A_HEADS):
            vt_ref[c, hd * PV_ROWS:hd * PV_ROWS + A_HEAD_DIM, :] = vt[hd * A_HEAD_DIM:(hd + 1) * A_HEAD_DIM, tok]
            vt_ref[c, hd * PV_ROWS + A_HEAD_DIM:(hd + 1) * PV_ROWS, :] = ones


def _proj_even(x, g, w_in, conv_w):
    b, s, d = x.shape
    t = ATT_TILE
    nchunk = s // t
    per = ROW_TILE // t
    q_end, k_end, v_end = A_WIDTH, 2 * A_WIDTH, 3 * A_WIDTH
    iq_end = v_end + IDX_HEADS * IDX_DIM
    ik_end = iq_end + IDX_DIM
    iw_end = ik_end + IDX_HEADS
    wb = w_in.astype(BF16)
    wk = wb[:, q_end:k_end]
    wik = wb[:, iq_end:ik_end]
    wugg = wb[:, iw_end:]
    wt = jnp.concatenate([wb[:, :q_end], wb[:, k_end:v_end], wb[:, v_end:iq_end], wb[:, ik_end:iw_end]], axis=1).T
    wt = jnp.pad(wt, ((0, BF16_SUBLANES - IDX_HEADS), (0, 0)))
    const = lambda shape: pl.BlockSpec(shape, lambda bi, i: (0,) * len(shape))
    return pl.pallas_call(
        _proj_kernel,
        out_shape=(
            jax.ShapeDtypeStruct((b, nchunk, A_WIDTH, t), BF16),
            jax.ShapeDtypeStruct((b, nchunk, A_HEADS * PV_ROWS, t), BF16),
            jax.ShapeDtypeStruct((b, nchunk, IDX_HEADS * IDX_DIM, t), BF16),
            jax.ShapeDtypeStruct((b, nchunk, IDX_HEADS, t), F32),
            jax.ShapeDtypeStruct((b, nchunk, t, A_WIDTH), BF16),
            jax.ShapeDtypeStruct((b, nchunk, t, IDX_DIM), BF16),
            jax.ShapeDtypeStruct((b, s, B_WIDTH), BF16),
        ),
        grid=(b, nchunk // per),
        in_specs=[
            pl.BlockSpec((None, per * t, d), lambda bi, i: (bi, i, 0)),
            const((1, d)),
            const(wk.shape), const(wik.shape), const(wugg.shape), const(wt.shape), const(conv_w.shape),
        ],
        out_specs=(
            pl.BlockSpec((None, per, A_WIDTH, t), lambda bi, i: (bi, i, 0, 0)),
            pl.BlockSpec((None, per, A_HEADS * PV_ROWS, t), lambda bi, i: (bi, i, 0, 0)),
            pl.BlockSpec((None, per, IDX_HEADS * IDX_DIM, t), lambda bi, i: (bi, i, 0, 0)),
            pl.BlockSpec((None, per, IDX_HEADS, t), lambda bi, i: (bi, i, 0, 0)),
            pl.BlockSpec((None, per, t, A_WIDTH), lambda bi, i: (bi, i, 0, 0)),
            pl.BlockSpec((None, per, t, IDX_DIM), lambda bi, i: (bi, i, 0, 0)),
            pl.BlockSpec((None, per * t, B_WIDTH), lambda bi, i: (bi, i, 0)),
        ),
        scratch_shapes=[pltpu.VMEM((SUBLANES, B_WIDTH), F32)],
        compiler_params=_params("arbitrary", "arbitrary"),
        name="proj_even",
    )(x, g.reshape(1, d), wk, wik, wugg, wt, conv_w)


def _bias_kernel(rb_ref, o_ref, *, thresholds, far_bucket):
    hd = pl.program_id(0)
    t = o_ref.shape[-1]
    r = lax.broadcasted_iota(I32, (t, t), 1)
    c = lax.broadcasted_iota(I32, (t, t), 0)
    for m in range(2):
        dist = t * m + r - c
        val = jnp.full((t, t), rb_ref[0, hd], F32)
        for bucket in range(1, REL_BUCKETS):
            val = jnp.where(dist >= thresholds[bucket], rb_ref[bucket, hd], val)
        val = (val - rb_ref[far_bucket, hd]) * LOG2E
        if m == 0:
            val = jnp.where(dist < 0, MASKED, val)
        o_ref[m] = val
    o_ref[2] = jnp.zeros((t, t), F32)


def _bias_tiles(rel_bias, seq):
    t = ATT_TILE
    table = _t5_bucket_table(seq)
    assert np.all(np.diff(table) >= 0) and table[0] == 0 and table.max() == REL_BUCKETS - 1
    thresholds = tuple(int(np.argmax(table >= bkt)) for bkt in range(REL_BUCKETS))
    far_bucket = int(table[2 * t - 1])
    assert np.all(table[2 * t - 1:] == far_bucket)
    return pl.pallas_call(
        functools.partial(_bias_kernel, thresholds=thresholds, far_bucket=far_bucket),
        out_shape=jax.ShapeDtypeStruct((A_HEADS, 3, t, t), F32),
        grid=(A_HEADS,),
        in_specs=[pl.BlockSpec(memory_space=pltpu.SMEM)],
        out_specs=pl.BlockSpec((None, 3, t, t), lambda h: (h, 0, 0, 0)),
        compiler_params=_params("parallel"),
        name="rel_bias_tiles",
    )(rel_bias)


def _bit_transpose32(words):
    a = list(words)
    shift, mask = WORD // 2, 0x0000FFFF
    while shift:
        k = 0
        while k < WORD:
            swap = (a[k] ^ lax.shift_right_logical(a[k + shift], jnp.int32(shift))) & mask
            a[k] = a[k] ^ swap
            a[k + shift] = a[k + shift] ^ (swap << shift)
            k = (k + shift + 1) & ~shift
        shift //= 2
        mask ^= (mask << shift) & 0xFFFFFFFF
    return a


def _dsa_kernel(qt_ref, iqt_ref, iwt_ref, k_ref, vt_ref, ik_ref, bias_ref, o_ref,
                sc_ref, plane_ref, eq_ref, acc_ref, m_ref, qz_ref, mask_ref, s_ref, cm_ref):
    i = pl.program_id(1)
    t = ATT_TILE
    nk = i + 1
    iw = iwt_ref[...]

    def index_scores(j):
        ikc = ik_ref[j]
        sc = jnp.zeros((t, t), F32)
        for hd in range(IDX_HEADS):
            d = jnp.dot(ikc, iqt_ref[hd * IDX_DIM:(hd + 1) * IDX_DIM, :],
                        preferred_element_type=F32)
            sc = sc + jnp.maximum(d, 0.0) * iw[hd:hd + 1, :]
        return sc

    def store_scores(j, sc):
        sc_ref[j] = sc
        bits = pltpu.bitcast(sc + 0.0, I32)
        ukeys = bits ^ ((bits >> 31) | KEY_MIN)
        for col in range(t // LANES):
            lanes = slice(col * LANES, (col + 1) * LANES)
            planes = _bit_transpose32([ukeys[v * SUBLANES:(v + 1) * SUBLANES, lanes] for v in range(WORD)])
            for r in range(WORD):
                plane_ref[r, j, :, lanes] = planes[r]

    @pl.when(i == 0)
    def _():
        plane_ref[...] = jnp.zeros(plane_ref.shape, I32)

    def phase_a(pair, carry):
        store_scores(2 * pair, index_scores(2 * pair))
        store_scores(2 * pair + 1, index_scores(2 * pair + 1))
        return carry

    lax.fori_loop(0, lax.shift_right_logical(i, 1), phase_a, 0)

    @pl.when((i & 1) == 1)
    def _():
        store_scores(i - 1, index_scores(i - 1))

    qpos = lax.broadcasted_iota(I32, (t, t), 1)
    kpos = lax.broadcasted_iota(I32, (t, t), 0)
    store_scores(i, jnp.where(kpos <= qpos, index_scores(i), -jnp.inf))

    nchunk = eq_ref.shape[0]
    live = lax.broadcasted_iota(I32, eq_ref.shape, 0) < nk
    eq_ref[...] = jnp.where(live, jnp.int32(-1), jnp.int32(0))

    def search(span):
        def search_bit(r, carry):
            thr_u, above = carry
            cnt = jnp.zeros((SUBLANES, t), I32)
            for c in range(span):
                cnt = cnt + lax.population_count(eq_ref[c] & plane_ref[r, c])
            cnt = jnp.sum(cnt.astype(F32), axis=0, keepdims=True)
            take = above + cnt >= float(TOPK_MAX)
            for c in range(span):
                eq = eq_ref[c]
                with_bit = eq & plane_ref[r, c]
                eq_ref[c] = jnp.where(take, with_bit, eq ^ with_bit)
            bit = jnp.left_shift(jnp.int32(1), WORD - 1 - r)
            return jnp.where(take, thr_u | bit, thr_u), jnp.where(take, above, above + cnt)

        return lambda: lax.fori_loop(0, WORD, search_bit,
                                     (jnp.zeros((1, t), I32), jnp.zeros((1, t), F32)))

    quarter = nchunk // 4
    thr_u, _ = lax.cond(
        nk <= 2 * quarter,
        lambda: lax.cond(nk <= quarter, search(quarter), search(2 * quarter)),
        lambda: lax.cond(nk <= 3 * quarter, search(3 * quarter), search(nchunk)))

    def key_to_score(ukey):
        return pltpu.bitcast(ukey ^ (~(ukey >> 31) | KEY_MIN), F32)

    def count_scores(*preds):
        def body(j, cnts):
            sc = sc_ref[j]
            return tuple(cnt + jnp.sum(jnp.where(pred(sc), 1.0, 0.0).reshape(t // SUBLANES, SUBLANES, t), axis=0)
                         for cnt, pred in zip(cnts, preds))
        cnts = lax.fori_loop(0, nk, body, tuple(jnp.zeros((SUBLANES, t), F32) for _ in preds))
        return [jnp.sum(cnt, axis=0, keepdims=True) for cnt in cnts]

    thr_fast = key_to_score(thr_u)
    reach, exceed = count_scores(lambda sc: sc >= thr_fast, lambda sc: sc > thr_fast)
    good = jnp.logical_and(exceed < float(TOPK_MAX), reach >= float(TOPK_MAX))

    def search_by_value():
        def step(r, ukey):
            cand = ukey | jnp.left_shift(jnp.int32(1), WORD - 1 - r)
            cand_score = key_to_score(cand)
            reach, = count_scores(lambda sc: sc >= cand_score)
            return jnp.where(reach >= float(TOPK_MAX), cand, ukey)
        ukey = lax.fori_loop(0, WORD, step, jnp.zeros((1, t), I32))
        thr_slow = jnp.where(ukey == 0, -jnp.inf, key_to_score(ukey))
        exceed, = count_scores(lambda sc: sc > thr_slow)
        return thr_slow, exceed

    thr, exceed = lax.cond(jnp.min(jnp.where(good, 1.0, 0.0)) > 0.5,
                           lambda: (thr_fast, exceed), search_by_value)
    need = float(TOPK_MAX) - exceed

    m_ref[...] = jnp.full(m_ref.shape, MASKED, F32)
    acc_ref[...] = jnp.zeros(acc_ref.shape, F32)
    row = lax.broadcasted_iota(I32, (2 * A_HEAD_DIM, t), 0)
    for hd in range(A_HEADS):
        pair = hd // 2
        blk = qt_ref[2 * A_HEAD_DIM * pair:2 * A_HEAD_DIM * (pair + 1), :]
        own = (row >= A_HEAD_DIM) if hd % 2 else (row < A_HEAD_DIM)
        qz_ref[hd] = jnp.where(own, blk, jnp.zeros_like(blk))
    tri = (lax.broadcasted_iota(I32, (t, t), 0) >= lax.broadcasted_iota(I32, (t, t), 1)).astype(BF16)

    def select(j, ties_before):
        kk = sc_ref[j]
        is_tie = kk == thr
        tie_rank = jnp.dot(tri, jnp.where(is_tie, 1.0, 0.0).astype(BF16),
                           preferred_element_type=F32) + ties_before
        tie_add = jnp.where(is_tie, jnp.where(tie_rank <= need, 0.0, MASKED), MASKED)
        mask_ref[...] = jnp.where(kk > thr, 0.0, tie_add).astype(BF16)
        return tie_rank[t - 1:t, :]

    slabs = (t // BF16_SUBLANES, BF16_SUBLANES, t)

    def scores(with_bias, slot, j, hd):
        pair = hd // 2
        s = jnp.dot(k_ref[j, :, 2 * A_HEAD_DIM * pair:2 * A_HEAD_DIM * (pair + 1)], qz_ref[hd],
                    preferred_element_type=F32)
        if with_bias:
            s = s + bias_ref[hd, jnp.minimum(i - j, 2)]
        s = s.astype(BF16) + mask_ref[...]
        s_ref[slot, hd] = s
        parts = [s.reshape(slabs)[r] for r in range(slabs[0])]
        while len(parts) > 1:
            parts = [jnp.maximum(a, b) for a, b in zip(parts[::2], parts[1::2])]
        cm_ref[slot, hd] = jnp.max(parts[0].astype(F32), axis=0, keepdims=True)

    def consume(slot, j, hd):
        m_old = m_ref[hd]
        m_new = jnp.maximum(m_old, cm_ref[slot, hd])
        m_ref[hd] = m_new
        alpha = jnp.exp2(m_old - m_new)
        m_tile = jnp.broadcast_to(m_new, (BF16_SUBLANES, t)).astype(BF16)
        p = jnp.exp2(s_ref[slot, hd].reshape(slabs) - m_tile[None]).reshape(t, t)
        pv = jnp.dot(vt_ref[j, hd * PV_ROWS:(hd + 1) * PV_ROWS, :], p, preferred_element_type=F32)
        acc_ref[hd] = alpha * acc_ref[hd] + pv

    def half_step(with_bias, slot, j, ties):
        ties = select(j, ties)
        for hd in range(A_HEADS):
            scores(with_bias, slot, j, hd)
            consume(1 - slot, j - 1, hd)
        return ties

    def step(with_bias, j, ties_before):
        return lax.cond(lax.rem(j, 2) == 1, functools.partial(half_step, with_bias, 1, j),
                        functools.partial(half_step, with_bias, 0, j), ties_before)

    def far_pair(pair, ties):
        ties = half_step(False, 1, 2 * pair + 1, ties)
        return half_step(False, 0, 2 * pair + 2, ties)

    ties = select(0, jnp.zeros((1, t), F32))
    for hd in range(A_HEADS):
        scores(True, 0, 0, hd)
    first_near = jnp.maximum(i - 1, 1)
    far_pairs = lax.shift_right_logical(first_near - 1, 1)
    ties = lax.fori_loop(0, far_pairs, far_pair, ties)
    ties = lax.fori_loop(2 * far_pairs + 1, first_near, functools.partial(step, False), ties)
    lax.fori_loop(first_near, nk, functools.partial(step, True), ties)
    for slot in range(2):
        @pl.when(lax.rem(i, 2) == slot)
        def _():
            for hd in range(A_HEADS):
                consume(slot, i, hd)

    outs = []
    for hd in range(A_HEADS):
        outs.append(acc_ref[hd, :A_HEAD_DIM, :] / acc_ref[hd, A_HEAD_DIM:A_HEAD_DIM + 1, :])
    o_ref[...] = jnp.concatenate(outs, axis=0).T.astype(BF16)


def _dsa_attention(qt, vt, iqt, iwt, k, ik, bias):
    b, nchunk, t, _ = k.shape
    s = nchunk * t
    return pl.pallas_call(
        _dsa_kernel,
        out_shape=jax.ShapeDtypeStruct((b, s, A_WIDTH), BF16),
        grid=(b, nchunk),
        in_specs=[
            pl.BlockSpec((None, None, A_WIDTH, t), lambda bi, i: (bi, i, 0, 0)),
            pl.BlockSpec((None, None, IDX_HEADS * IDX_DIM, t), lambda bi, i: (bi, i, 0, 0)),
            pl.BlockSpec((None, None, IDX_HEADS, t), lambda bi, i: (bi, i, 0, 0)),
            pl.BlockSpec((None, nchunk, t, A_WIDTH), lambda bi, i: (bi, 0, 0, 0)),
            pl.BlockSpec((None, nchunk, A_HEADS * PV_ROWS, t), lambda bi, i: (bi, 0, 0, 0)),
            pl.BlockSpec((None, nchunk, t, IDX_DIM), lambda bi, i: (bi, 0, 0, 0)),
            pl.BlockSpec(bias.shape, lambda bi, i: (0, 0, 0, 0)),
        ],
        out_specs=pl.BlockSpec((None, t, A_WIDTH), lambda bi, i: (bi, i, 0)),
        scratch_shapes=[
            pltpu.VMEM((nchunk, t, t), F32),
            pltpu.VMEM((WORD, nchunk, t // WORD, t), I32),
            pltpu.VMEM((nchunk, t // WORD, t), I32),
            pltpu.VMEM((A_HEADS, PV_ROWS, t), F32),
            pltpu.VMEM((A_HEADS, 1, t), F32),
            pltpu.VMEM((A_HEADS, 2 * A_HEAD_DIM, t), BF16),
            pltpu.VMEM((t, t), BF16),
            pltpu.VMEM((2, A_HEADS, t, t), BF16),
            pltpu.VMEM((2, A_HEADS, 1, t), F32),
        ],
        compiler_params=_params("parallel", "arbitrary"),
        name="dsa_attention",
    )(qt, iqt, iwt, k, vt, ik, bias)


def _halves(n):
    return (slice(0, n // 2), slice(n // 2, n))


def _xattn_block(parts, gpre_ref, gpost_ref, wq_ref, kt_ref, v_ref, wo_ref):
    heads = [slice(hd * X_HEAD_DIM, (hd + 1) * X_HEAD_DIM) for hd in range(X_HEADS)]
    hs = [_rms(x, gpre_ref[...]).astype(BF16) for x in parts]
    qs = [jnp.dot(h, wq_ref[...], preferred_element_type=F32) for h in hs]
    logits = [[jnp.dot(q[:, cols].astype(BF16), kt_ref[cols, :], preferred_element_type=F32)
               * (X_HEAD_DIM ** -0.5) for cols in heads] for q in qs]
    os = []
    for part_logits in logits:
        outs = []
        for cols, lg in zip(heads, part_logits):
            p = jnp.exp(lg - jnp.max(lg, axis=-1, keepdims=True))
            den = jnp.sum(p, axis=-1, keepdims=True)
            outs.append(jnp.dot(p.astype(BF16), v_ref[:, cols], preferred_element_type=F32) / den)
        os.append(jnp.concatenate(outs, axis=-1).astype(BF16))
    ys = [jnp.dot(o, wo_ref[...], preferred_element_type=F32) for o in os]
    return [x + _rms(y, gpost_ref[...]) for x, y in zip(parts, ys)]


def _xattn_operands(d, m, g_pre, g_post, wq, kt, v, wo):
    const = lambda shape: pl.BlockSpec(shape, lambda bi, i: (0, 0))
    specs = [const((1, d)), const((1, d)), const((d, X_WIDTH)),
             pl.BlockSpec((None, X_WIDTH, m), lambda bi, i: (bi, 0, 0)),
             pl.BlockSpec((None, m, X_WIDTH), lambda bi, i: (bi, 0, 0)),
             const((X_WIDTH, d))]
    return specs, (g_pre.reshape(1, d), g_post.reshape(1, d), wq.astype(BF16), kt, v, wo.astype(BF16))


def _mix_out_xattn_kernel(x_ref, a_ref, c_ref, wa_ref, wc_ref, g_ref, *rest):
    *xattn_refs, o_ref = rest
    halves = _halves(x_ref.shape[0])
    ys = [jnp.dot(a_ref[rows, :], wa_ref[...], preferred_element_type=F32)
          + jnp.dot(c_ref[rows, :], wc_ref[...], preferred_element_type=F32) for rows in halves]
    x1 = [x_ref[rows, :] + _rms(y, g_ref[...]) for rows, y in zip(halves, ys)]
    for rows, out in zip(halves, _xattn_block(x1, *xattn_refs)):
        o_ref[rows, :] = out


def _mix_out_xattn(x, a, c, w_out, g_post, xattn_args):
    b, s, d = x.shape
    t = ROW_TILE
    wb = w_out.astype(BF16)
    row = lambda width: pl.BlockSpec((None, t, width), lambda bi, i: (bi, i, 0))
    const = lambda shape: pl.BlockSpec(shape, lambda bi, i: (0, 0))
    xspecs, xarrays = _xattn_operands(d, xattn_args[3].shape[-1], *xattn_args)
    return pl.pallas_call(
        _mix_out_xattn_kernel,
        out_shape=jax.ShapeDtypeStruct((b, s, d), F32),
        grid=(b, s // t),
        in_specs=[row(d), row(A_WIDTH), row(B_WIDTH), const((A_WIDTH, d)), const((B_WIDTH, d)),
                  const((1, d))] + xspecs,
        out_specs=row(d),
        compiler_params=_params("parallel", "parallel"),
        name="mix_out_xattn",
    )(x, a, c, wb[:A_WIDTH], wb[A_WIDTH:], g_post.reshape(1, d), *xarrays)


def _pool_xattn_kernel(x_ref, xprev_ref, gpre_ref, gpost_ref, w_ref, scale_ref, *rest):
    *xattn_refs, o_ref = rest
    i = pl.program_id(1)
    x = x_ref[...]
    t, d = x.shape
    gpre = gpre_ref[...]
    h = _rms(x, gpre)
    hp = _rms(xprev_ref[...], gpre) * jnp.where(i > 0, 1.0, 0.0)
    ext = jnp.concatenate([hp, h], axis=0)
    group = d // len(POOL_WINDOWS)
    pos = i * t + lax.broadcasted_iota(I32, (t, group), 0)
    pooled = []
    for gi, win in enumerate(POOL_WINDOWS):
        sums = ext[:, gi * group:(gi + 1) * group]
        width = 1
        while width < win:
            sums = sums[width:] + sums[:-width]
            width *= 2
        wsum = sums[POOL_HALO - (win - 1):POOL_HALO - (win - 1) + t]
        cnt = jnp.minimum(pos + 1, win).astype(F32)
        pooled.append((wsum / cnt - h[:, gi * group:(gi + 1) * group]).astype(BF16))
    halves = _halves(t)
    x1 = []
    for rows in halves:
        y = jnp.concatenate([jnp.dot(pg[rows], w_ref[gi], preferred_element_type=F32)
                             for gi, pg in enumerate(pooled)], axis=-1) * scale_ref[...]
        x1.append(x[rows] + _rms(y, gpost_ref[...]))
    for rows, out in zip(halves, _xattn_block(x1, *xattn_refs)):
        o_ref[rows, :] = out


def _pool_xattn(x, g_pre, g_post, pool_w, pool_scale, xattn_args):
    b, s, d = x.shape
    t = ROW_TILE
    const = lambda shape: pl.BlockSpec(shape, lambda bi, i: (0,) * len(shape))
    prev = t // POOL_HALO
    xspecs, xarrays = _xattn_operands(d, xattn_args[3].shape[-1], *xattn_args)
    return pl.pallas_call(
        _pool_xattn_kernel,
        out_shape=jax.ShapeDtypeStruct((b, s, d), F32),
        grid=(b, s // t),
        in_specs=[
            pl.BlockSpec((None, t, d), lambda bi, i: (bi, i, 0)),
            pl.BlockSpec((None, POOL_HALO, d), lambda bi, i: (bi, jnp.maximum(i * prev - 1, 0), 0)),
            const((1, d)), const((1, d)), const(pool_w.shape), const((1, d)),
        ] + xspecs,
        out_specs=pl.BlockSpec((None, t, d), lambda bi, i: (bi, i, 0)),
        compiler_params=_params("parallel", "parallel"),
        name="pool_xattn",
    )(x, x, g_pre.reshape(1, d), g_post.reshape(1, d), pool_w.astype(BF16), pool_scale.reshape(1, d),
      *xarrays)


def _mem_kv_kernel(mem_ref, g_ref, wkt_ref, wv_ref, kt_ref, v_ref):
    mem_n = _rms(mem_ref[...], g_ref[...]).astype(BF16)
    kt = lax.dot_general(wkt_ref[...], mem_n, (((1,), (1,)), ((), ())), preferred_element_type=F32)
    kt_ref[...] = kt.astype(BF16)
    v_ref[...] = jnp.dot(mem_n, wv_ref[...], preferred_element_type=F32).astype(BF16)


def _mem_kv(mem, g, wk, wv):
    b, m, d = mem.shape
    const = lambda shape: pl.BlockSpec(shape, lambda bi: (0, 0))
    return pl.pallas_call(
        _mem_kv_kernel,
        out_shape=(jax.ShapeDtypeStruct((b, X_WIDTH, m), BF16), jax.ShapeDtypeStruct((b, m, X_WIDTH), BF16)),
        grid=(b,),
        in_specs=[pl.BlockSpec((None, m, d), lambda bi: (bi, 0, 0)), const((1, d)), const((X_WIDTH, d)),
                  const((d, X_WIDTH))],
        out_specs=(pl.BlockSpec((None, X_WIDTH, m), lambda bi: (bi, 0, 0)),
                   pl.BlockSpec((None, m, X_WIDTH), lambda bi: (bi, 0, 0))),
        compiler_params=_params("parallel"),
        name="mem_kv",
    )(mem, g.reshape(1, d), wk.astype(BF16).T, wv.astype(BF16))


def kernel(x, mem, ffn_w_gate, ffn_w_up, ffn_w_down, ffn_norm_pre, ffn_norm_post, mix_norm_pre,
           mix_norm_post, even_w_in, even_conv_w, even_w_out, rel_bias, pool_w, pool_scale,
           xattn_norm_pre, xattn_mem_norm, xattn_norm_post, xattn_wq, xattn_wk, xattn_wv, xattn_wo):
    b, s, d = x.shape
    depth = ffn_w_gate.shape[0]
    assert s % ATT_TILE == 0 and s // 4 >= TOPK_MAX and (b * s) % FFN_ROWS == 0 and s % ROW_TILE == 0
    bias = _bias_tiles(rel_bias, s)

    ffn_weights = (ffn_w_gate, ffn_w_up, ffn_w_down)

    def ffn(xx, layer, j):
        out = _ffn(xx.reshape(b * s, d), layer, j, ffn_norm_pre, ffn_norm_post, *ffn_weights)
        return out.reshape(b, s, d)

    for layer in range(depth):
        x = ffn(x, layer, 0)
        kt, v = _mem_kv(mem, xattn_mem_norm[layer], xattn_wk[layer], xattn_wv[layer])
        xattn_args = (xattn_norm_pre[layer], xattn_norm_post[layer], xattn_wq[layer], kt, v,
                      xattn_wo[layer])
        if layer % 2 == 0:
            e = layer // 2
            qt, vt, iqt, iwt, k, ik, c = _proj_even(x, mix_norm_pre[layer], even_w_in[e], even_conv_w[e])
            a = _dsa_attention(qt, vt, iqt, iwt, k, ik, bias)
            x = _mix_out_xattn(x, a, c, even_w_out[e], mix_norm_post[layer], xattn_args)
        else:
            o = layer // 2
            x = _pool_xattn(x, mix_norm_pre[layer], mix_norm_post[layer], pool_w[o], pool_scale[o],
                            xattn_args)
        x = ffn(x, layer, 1)
    return x
```
